```python
import math
import jax, jax.numpy as jnp
from jax import lax
import numpy as np

D_MODEL = 1024
BATCH = 8
SEQ = 4096
DEPTH = 4

CTX_LEN = 256
GRID_W = 64
N_MIXERS = 3
Q_BLOCK = 128
NORM_EPS = 1e-6
ROPE_BASE = 10000.0
N_MOD = 6

DIFF_HEADS = 8
DIFF_HEAD_DIM = 64
DIFF_QK = 2 * DIFF_HEADS * DIFF_HEAD_DIM
DIFF_V = 2 * DIFF_HEADS * DIFF_HEAD_DIM

GLA_HEADS = 4
GLA_KEY = D_MODEL // 2
GLA_VAL = D_MODEL
GLA_DK = GLA_KEY // GLA_HEADS
GLA_DV = GLA_VAL // GLA_HEADS
GLA_GATE_RANK = 16
GLA_GATE_NORM = 16.0
GLA_CHUNK = 64
GLA_IN = 2 * GLA_KEY + 2 * GLA_VAL + 2 * GLA_GATE_RANK

MLA_HEADS = 8
MLA_Q_LORA = 384
MLA_KV_LORA = 256
MLA_NOPE = 128
MLA_ROPE = 64
MLA_V = 128
MLA_IN = MLA_Q_LORA + MLA_KV_LORA + MLA_ROPE
MLA_SCALE = (MLA_NOPE + MLA_ROPE) ** -0.5

N_EXPERTS = 16
N_GROUPS = 4
EXPERTS_PER_GROUP = N_EXPERTS // N_GROUPS
TOP_K = 2
EXPERT_FF = 256
SHARED_FF = 256

N_DIFF_LAYERS = (DEPTH + 2) // 3
N_GLA_LAYERS = (DEPTH + 1) // 3
N_MLA_LAYERS = DEPTH // 3

kernel_name = "hybrid_diffattn_gla_mla_grouped_moe"


def rmsnorm(x, g):
    xf = x.astype(jnp.float32)
    y = xf * lax.rsqrt(jnp.mean(xf * xf, axis=-1, keepdims=True) + NORM_EPS)
    return (y * g.astype(jnp.float32)).astype(x.dtype)


def modulate(h, shift, scale):
    return h * (1 + scale) + shift


def ada_params(cond, w, b):
    m = (cond @ w + b).reshape(cond.shape[:-1] + (N_MOD, D_MODEL))
    return tuple(m[..., j, :][..., None, :] for j in range(N_MOD))


def axis_angles(pos, dim):
    inv_freq = ROPE_BASE ** (-jnp.arange(0, dim, 2, dtype=jnp.float32) / dim)
    return pos[:, None] * inv_freq[None, :]


def rope_1d(x, ang):
    xf = x.astype(jnp.float32)
    x1, x2 = jnp.split(xf, 2, axis=-1)
    shape = (1, ang.shape[0]) + (1,) * (x.ndim - 3) + (ang.shape[1],)
    cos = jnp.cos(ang).reshape(shape)
    sin = jnp.sin(ang).reshape(shape)
    return jnp.concatenate([x1 * cos - x2 * sin, x1 * sin + x2 * cos], axis=-1).astype(x.dtype)


def rope_2d(x, pos_row, pos_col):
    half = x.shape[-1] // 2
    return jnp.concatenate([rope_1d(x[..., :half], axis_angles(pos_row, half)),
                            rope_1d(x[..., half:], axis_angles(pos_col, half))], axis=-1)


def sweep_query_blocks(fn, *qs):
    b, s = qs[0].shape[:2]
    nb = s // Q_BLOCK
    to_blocks = lambda t: jnp.moveaxis(t.reshape((b, nb, Q_BLOCK) + t.shape[2:]), 1, 0)
    out = lax.map(lambda blk: fn(*blk), tuple(to_blocks(t) for t in qs))
    return jnp.moveaxis(out, 0, 1).reshape((b, s) + out.shape[3:])


def diff_attention(q, k, v, lam):
    s = jnp.einsum('bqmhd,bkmhd->bmhqk', q, k).astype(jnp.float32) * DIFF_HEAD_DIM ** -0.5
    p = jax.nn.softmax(s, axis=-1)
    w = p[:, 0] - lam * p[:, 1]
    return jnp.einsum('bhqk,bkhe->bqhe', w.astype(v.dtype), v)


def diff_mixer(h_ctx, h_lat, pos_row, pos_col, w_in, lam_p, norm_g, w_out, lam_init, need_ctx):
    def project(h):
        b, n, _ = h.shape
        q, k, v = jnp.split(h @ w_in, [DIFF_QK, 2 * DIFF_QK], axis=-1)
        return (q.reshape(b, n, 2, DIFF_HEADS, DIFF_HEAD_DIM),
                k.reshape(b, n, 2, DIFF_HEADS, DIFF_HEAD_DIM),
                v.reshape(b, n, DIFF_HEADS, 2 * DIFF_HEAD_DIM))

    lf = lam_p.astype(jnp.float32)
    lam = jnp.exp(jnp.sum(lf[0] * lf[1])) - jnp.exp(jnp.sum(lf[2] * lf[3])) + lam_init

    def finish(o):
        b, n = o.shape[:2]
        o = rmsnorm(o, norm_g) * (1.0 - lam_init)
        return o.reshape(b, n, DIFF_V) @ w_out

    q_c, k_c, v_c = project(h_ctx)
    q_l, k_l, v_l = project(h_lat)
    q_l = rope_2d(q_l, pos_row, pos_col)
    k_l = rope_2d(k_l, pos_row, pos_col)
    k_all = jnp.concatenate([k_c, k_l], axis=1)
    v_all = jnp.concatenate([v_c, v_l], axis=1)
    y_lat = finish(sweep_query_blocks(lambda qb: diff_attention(qb, k_all, v_all, lam), q_l))
    y_ctx = finish(diff_attention(q_c, k_c, v_c, lam)) if need_ctx else None
    return y_ctx, y_lat


def gla_chunk_scan(q, k, v, g, s0):
    b, n, h, _ = q.shape
    dv = v.shape[-1]
    nc = n // GLA_CHUNK

    def chunks(t):
        return t.astype(jnp.float32).reshape(b, nc, GLA_CHUNK, h, t.shape[-1]).transpose(1, 0, 3, 2, 4)

    qc, kc, vc, gc = chunks(q), chunks(k), chunks(v), chunks(g)
    cum = jnp.cumsum(gc, axis=-2)
    last = cum[..., -1:, :]
    q_dec = qc * jnp.exp(cum)
    k_inv = kc * jnp.exp(-cum)
    k_end = kc * jnp.exp(last - cum)
    lower = jnp.tril(jnp.ones((GLA_CHUNK, GLA_CHUNK), dtype=bool))
    a = jnp.where(lower, jnp.einsum('cbhtd,cbhsd->cbhts', q_dec, k_inv), 0.0)
    o_intra = jnp.einsum('cbhts,cbhse->cbhte', a, vc)

    def step(state, inp):
        q_i, k_i, v_i, decay_i = inp
        o_i = jnp.einsum('bhtd,bhde->bhte', q_i, state)
        state = state * jnp.swapaxes(decay_i, -1, -2) + jnp.einsum('bhtd,bhte->bhde', k_i, v_i)
        return state, o_i

    s_fin, o_inter = lax.scan(step, s0, (q_dec, k_end, vc, jnp.exp(last)))
    o = (o_intra + o_inter).transpose(1, 0, 3, 2, 4).reshape(b, n, h, dv)
    return o.astype(v.dtype), s_fin


def gla_mixer(h_ctx, h_lat, w_in, gate_w, gate_b, norm_g, w_out, need_ctx):
    splits = [GLA_KEY, 2 * GLA_KEY, 2 * GLA_KEY + GLA_VAL, 2 * GLA_KEY + 2 * GLA_VAL,
              2 * GLA_KEY + 2 * GLA_VAL + GLA_GATE_RANK]

    def project(h):
        b, n, _ = h.shape
        q, k, v, og, r_f, r_b = jnp.split(h @ w_in, splits, axis=-1)
        q = q.reshape(b, n, GLA_HEADS, GLA_DK) * GLA_DK ** -0.5
        k = k.reshape(b, n, GLA_HEADS, GLA_DK)
        v = v.reshape(b, n, GLA_HEADS, GLA_DV)
        og = og.reshape(b, n, GLA_HEADS, GLA_DV)
        g_f = jax.nn.log_sigmoid((r_f @ gate_w[0] + gate_b[0]).astype(jnp.float32)) / GLA_GATE_NORM
        g_b = jax.nn.log_sigmoid((r_b @ gate_w[1] + gate_b[1]).astype(jnp.float32)) / GLA_GATE_NORM
        return q, k, v, og, g_f.reshape(b, n, GLA_HEADS, GLA_DK), g_b.reshape(b, n, GLA_HEADS, GLA_DK)

    def finish(o, og):
        b, n = o.shape[:2]
        o = rmsnorm(o, norm_g) * jax.nn.silu(og)
        return o.reshape(b, n, GLA_VAL) @ w_out

    flip = lambda t: t[:, ::-1]
    q_c, k_c, v_c, og_c, gf_c, gb_c = project(h_ctx)
    q_l, k_l, v_l, og_l, gf_l, gb_l = project(h_lat)
    s0 = jnp.zeros((h_lat.shape[0], GLA_HEADS, GLA_DK, GLA_DV), jnp.float32)
    o_cf, s_cf = gla_chunk_scan(q_c, k_c, v_c, gf_c, s0)
    o_cb, s_cb = gla_chunk_scan(flip(q_c), flip(k_c), flip(v_c), flip(gb_c), s0)
    o_lf, _ = gla_chunk_scan(q_l, k_l, v_l, gf_l, s_cf)
    o_lb, _ = gla_chunk_scan(flip(q_l), flip(k_l), flip(v_l), flip(gb_l), s_cb)
    y_lat = finish(o_lf + flip(o_lb), og_l)
    y_ctx = finish(o_cf + flip(o_cb), og_c) if need_ctx else None
    return y_ctx, y_lat


def mla_attention(q_nope, q_rope, k_nope, k_rope, v):
    s = (jnp.einsum('bqhd,bkhd->bhqk', q_nope, k_nope)
         + jnp.einsum('bqhd,bkd->bhqk', q_rope, k_rope)).astype(jnp.float32) * MLA_SCALE
    p = jax.nn.softmax(s, axis=-1)
    return jnp.einsum('bhqk,bkhe->bqhe', p.astype(v.dtype), v)


def mla_mixer(h_ctx, h_lat, pos_row, pos_col, w_in, q_norm_g, w_uq, kv_norm_g, w_ukv, w_out, need_ctx):
    def project(h, rotary):
        b, n, _ = h.shape
        c_q, c_kv, k_rope = jnp.split(h @ w_in, [MLA_Q_LORA, MLA_Q_LORA + MLA_KV_LORA], axis=-1)
        q = (rmsnorm(c_q, q_norm_g) @ w_uq).reshape(b, n, MLA_HEADS, MLA_NOPE + MLA_ROPE)
        kv = (rmsnorm(c_kv, kv_norm_g) @ w_ukv).reshape(b, n, MLA_HEADS, MLA_NOPE + MLA_V)
        q_nope, q_rope = jnp.split(q, [MLA_NOPE], axis=-1)
        k_nope, v = jnp.split(kv, [MLA_NOPE], axis=-1)
        k_rope = k_rope[:, :, None, :]
        if rotary:
            q_rope = rope_2d(q_rope, pos_row, pos_col)
            k_rope = rope_2d(k_rope, pos_row, pos_col)
        return q_nope, q_rope, k_nope, k_rope[:, :, 0, :], v

    def finish(o):
        b, n = o.shape[:2]
        return o.reshape(b, n, MLA_HEADS * MLA_V) @ w_out

    qn_c, qr_c, kn_c, kr_c, v_c = project(h_ctx, False)
    qn_l, qr_l, kn_l, kr_l, v_l = project(h_lat, True)
    kn_all = jnp.concatenate([kn_c, kn_l], axis=1)
    kr_all = jnp.concatenate([kr_c, kr_l], axis=1)
    v_all = jnp.concatenate([v_c, v_l], axis=1)
    y_lat = finish(sweep_query_blocks(
        lambda qn, qr: mla_attention(qn, qr, kn_all, kr_all, v_all), qn_l, qr_l))
    y_ctx = finish(mla_attention(qn_c, qr_c, kn_c, kr_c, v_c)) if need_ctx else None
    return y_ctx, y_lat


def route(h, router_w, router_b):
    scores = jax.nn.sigmoid((h @ router_w).astype(jnp.float32))
    sel = scores + router_b.astype(jnp.float32)
    sel_g = sel.reshape(sel.shape[:-1] + (N_GROUPS, EXPERTS_PER_GROUP))
    group_score = jnp.sum(lax.top_k(sel_g, TOP_K)[0], axis=-1)
    g_idx = jnp.argmax(group_score, axis=-1)
    in_group = jnp.take_along_axis(sel_g, g_idx[..., None, None], axis=-2)[..., 0, :]
    _, local = lax.top_k(in_group, TOP_K)
    expert = g_idx[..., None] * EXPERTS_PER_GROUP + local
    w = jnp.take_along_axis(scores, expert, axis=-1)
    w = w / jnp.sum(w, axis=-1, keepdims=True)
    return jnp.sum(jax.nn.one_hot(expert, N_EXPERTS, dtype=jnp.float32) * w[..., None], axis=-2)


def moe_ffn(h, router_w, router_b, w_up, w_down, sh_up, sh_down):
    combine = route(h, router_w, router_b).astype(h.dtype)
    sa, su = jnp.split(h @ sh_up, 2, axis=-1)
    y = (jax.nn.silu(sa) * su) @ sh_down
    for e in range(N_EXPERTS):
        a, u = jnp.split(h @ w_up[e], 2, axis=-1)
        y = y + combine[..., e:e + 1] * ((jax.nn.silu(a) * u) @ w_down[e])
    return y


def setup_inputs(seed: int = 0) -> dict:
    key = jax.random.key(seed)
    ks = iter(jax.random.split(key, 40))
    D = D_MODEL

    def nrm(shape, scale):
        return jax.random.normal(next(ks), shape, jnp.float32) * scale

    def gain(shape):
        return 1.0 + nrm(shape, 0.02)

    return {
        "x": nrm((BATCH, SEQ, D), 1.0),
        "c": nrm((BATCH, D), 1.0),
        "ctx": nrm((BATCH, CTX_LEN, D), 1.0),
        "c_ctx": nrm((D,), 1.0),
        "ada_w": nrm((DEPTH, D, N_MOD * D), 0.5 * D ** -0.5),
        "ada_b": nrm((DEPTH, N_MOD * D), 0.02),
        "norm_g": gain((DEPTH, 2, D)),
        "router_w": nrm((D, N_EXPERTS), D ** -0.5),
        "router_b": nrm((N_EXPERTS,), 0.01),
        "moe_w_up": nrm((DEPTH, N_EXPERTS, D, 2 * EXPERT_FF), D ** -0.5),
        "moe_w_down": nrm((DEPTH, N_EXPERTS, EXPERT_FF, D), EXPERT_FF ** -0.5),
        "shared_w_up": nrm((DEPTH, D, 2 * SHARED_FF), D ** -0.5),
        "shared_w_down": nrm((DEPTH, SHARED_FF, D), SHARED_FF ** -0.5),
        "diff_w_in": nrm((N_DIFF_LAYERS, D, 2 * DIFF_QK + DIFF_V), D ** -0.5),
        "diff_lam": nrm((N_DIFF_LAYERS, 4, DIFF_HEAD_DIM), 0.1),
        "diff_norm_g": gain((N_DIFF_LAYERS, 2 * DIFF_HEAD_DIM)),
        "diff_w_out": nrm((N_DIFF_LAYERS, DIFF_V, D), DIFF_V ** -0.5),
        "gla_w_in": nrm((N_GLA_LAYERS, D, GLA_IN), D ** -0.5),
        "gla_gate_w": nrm((N_GLA_LAYERS, 2, GLA_GATE_RANK, GLA_KEY), GLA_GATE_RANK ** -0.5),
        "gla_gate_b": nrm((N_GLA_LAYERS, 2, GLA_KEY), 0.1),
        "gla_norm_g": gain((N_GLA_LAYERS, GLA_DV)),
        "gla_w_out": nrm((N_GLA_LAYERS, GLA_VAL, D), GLA_VAL ** -0.5),
        "mla_w_in": nrm((N_MLA_LAYERS, D, MLA_IN), D ** -0.5),
        "mla_q_norm_g": gain((N_MLA_LAYERS, MLA_Q_LORA)),
        "mla_w_uq": nrm((N_MLA_LAYERS, MLA_Q_LORA, MLA_HEADS * (MLA_NOPE + MLA_ROPE)), MLA_Q_LORA ** -0.5),
        "mla_kv_norm_g": gain((N_MLA_LAYERS, MLA_KV_LORA)),
        "mla_w_ukv": nrm((N_MLA_LAYERS, MLA_KV_LORA, MLA_HEADS * (MLA_NOPE + MLA_V)), MLA_KV_LORA ** -0.5),
        "mla_w_out": nrm((N_MLA_LAYERS, MLA_HEADS * MLA_V, D), (MLA_HEADS * MLA_V) ** -0.5),
        "final_norm_g": gain((D,)),
    }


def reference(x, c, ctx, c_ctx, ada_w, ada_b, norm_g, router_w, router_b, moe_w_up, moe_w_down,
              shared_w_up, shared_w_down, diff_w_in, diff_lam, diff_norm_g, diff_w_out,
              gla_w_in, gla_gate_w, gla_gate_b, gla_norm_g, gla_w_out,
              mla_w_in, mla_q_norm_g, mla_w_uq, mla_kv_norm_g, mla_w_ukv, mla_w_out, final_norm_g):
    rows = x.shape[1] // GRID_W
    t = jnp.arange(rows * GRID_W)
    pos_row = (t // GRID_W).astype(jnp.float32)
    pos_col = (t % GRID_W).astype(jnp.float32)
    cond_lat = jax.nn.silu(c)
    cond_ctx = jax.nn.silu(c_ctx)
    x_lat, x_ctx = x, ctx
    for i in range(DEPTH):
        need_ctx = i < DEPTH - 1
        sh1, sc1, g1, sh2, sc2, g2 = ada_params(cond_lat, ada_w[i], ada_b[i])
        sh1c, sc1c, g1c, sh2c, sc2c, g2c = ada_params(cond_ctx, ada_w[i], ada_b[i])
        h_lat = modulate(rmsnorm(x_lat, norm_g[i, 0]), sh1, sc1)
        h_ctx = modulate(rmsnorm(x_ctx, norm_g[i, 0]), sh1c, sc1c)
        kind, j = i % N_MIXERS, i // N_MIXERS
        if kind == 0:
            lam_init = 0.8 - 0.6 * math.exp(-0.3 * i)
            y_ctx, y_lat = diff_mixer(h_ctx, h_lat, pos_row, pos_col, diff_w_in[j], diff_lam[j],
                                      diff_norm_g[j], diff_w_out[j], lam_init, need_ctx)
        elif kind == 1:
            y_ctx, y_lat = gla_mixer(h_ctx, h_lat, gla_w_in[j], gla_gate_w[j], gla_gate_b[j],
                                     gla_norm_g[j], gla_w_out[j], need_ctx)
        else:
            y_ctx, y_lat = mla_mixer(h_ctx, h_lat, pos_row, pos_col, mla_w_in[j], mla_q_norm_g[j],
                                     mla_w_uq[j], mla_kv_norm_g[j], mla_w_ukv[j], mla_w_out[j], need_ctx)
        x_lat = x_lat + g1 * y_lat
        x_lat = x_lat + g2 * moe_ffn(modulate(rmsnorm(x_lat, norm_g[i, 1]), sh2, sc2), router_w, router_b,
                                     moe_w_up[i], moe_w_down[i], shared_w_up[i], shared_w_down[i])
        if need_ctx:
            x_ctx = x_ctx + g1c * y_ctx
            x_ctx = x_ctx + g2c * moe_ffn(modulate(rmsnorm(x_ctx, norm_g[i, 1]), sh2c, sc2c), router_w, router_b,
                                          moe_w_up[i], moe_w_down[i], shared_w_up[i], shared_w_down[i])
    return rmsnorm(x_lat, final_norm_g)
```

```python
import functools
import math

import jax
import jax.numpy as jnp
from jax import lax
from jax.experimental import pallas as pl
from jax.experimental.pallas import tpu as pltpu

F32 = jnp.float32
BF16 = jnp.bfloat16

D_MODEL = 1024
DEPTH = 4
GRID_W = 64
N_MIXERS = 3
NORM_EPS = 1e-6
ROPE_BASE = 10000.0
N_MOD = 6

DIFF_HEADS = 8
DIFF_HEAD_DIM = 64
DIFF_QK = 2 * DIFF_HEADS * DIFF_HEAD_DIM
DIFF_V = 2 * DIFF_HEADS * DIFF_HEAD_DIM

GLA_HEADS = 4
GLA_KEY = D_MODEL // 2
GLA_VAL = D_MODEL
GLA_DK = GLA_KEY // GLA_HEADS
GLA_DV = GLA_VAL // GLA_HEADS
GLA_GATE_RANK = 16
GLA_GATE_NORM = 16.0
GLA_CHUNK = 64

MLA_HEADS = 8
MLA_Q_LORA = 384
MLA_KV_LORA = 256
MLA_NOPE = 128
MLA_ROPE = 64
MLA_V = 128
MLA_SCALE = (MLA_NOPE + MLA_ROPE) ** -0.5
MLA_QK_PAD = 256

N_EXPERTS = 16
N_GROUPS = 4
EXPERTS_PER_GROUP = N_EXPERTS // N_GROUPS
EXPERT_FF = 256
SHARED_FF = 256

LANES = 128
COND_ROWS = 16
NEG_BIG = -1e30
VMEM_LIMIT = 56 * 1024 * 1024

NT_DIMS = (((1,), (1,)), ((), ()))
TN_DIMS = (((0,), (0,)), ((), ()))


def _dot(a, b):
    return jnp.dot(a, b, preferred_element_type=F32)


def _dot_nt(a, b):
    return lax.dot_general(a, b, NT_DIMS, preferred_element_type=F32)


def _dot_tn(a, b):
    return lax.dot_general(a, b, TN_DIMS, preferred_element_type=F32)


def _split_bf16(a):
    hi = a.astype(BF16)
    lo = (a - hi.astype(F32)).astype(BF16)
    return hi, lo


def _silu(a):
    return a * jax.nn.sigmoid(a)


def _params(*sem):
    return pltpu.CompilerParams(dimension_semantics=sem, vmem_limit_bytes=VMEM_LIMIT)


def _const_spec(shape):
    nd = len(shape)
    return pl.BlockSpec(shape, lambda *_: (0,) * nd)


def _ada_kernel(cond_ref, w_ref, b_ref, o_ref):
    a_hi, a_lo = _split_bf16(_silu(cond_ref[...]))
    w_hi, w_lo = _split_bf16(w_ref[0])
    o_ref[0] = _dot(a_hi, w_hi) + _dot(a_lo, w_hi) + _dot(a_hi, w_lo) + b_ref[0]


def _ada_params(cond, ada_w, ada_b):
    depth, d, n = ada_w.shape
    tn = 1024
    return pl.pallas_call(
        _ada_kernel,
        grid=(depth, n // tn),
        in_specs=[
            pl.BlockSpec((COND_ROWS, d), lambda i, j: (0, 0)),
            pl.BlockSpec((1, d, tn), lambda i, j: (i, 0, j)),
            pl.BlockSpec((1, 1, tn), lambda i, j: (i, 0, j)),
        ],
        out_specs=pl.BlockSpec((1, COND_ROWS, tn), lambda i, j: (i, 0, j)),
        out_shape=jax.ShapeDtypeStruct((depth, COND_ROWS, n), F32),
        compiler_params=_params("parallel", "parallel"),
        name="ada_params",
    )(cond, ada_w, ada_b.reshape(depth, 1, n))


def _norm_mod(x, g, shift, scale):
    ms = jnp.mean(x * x, axis=-1, keepdims=True)
    return (x * lax.rsqrt(ms + NORM_EPS) * g) * (1.0 + scale) + shift


def _rms(x, g):
    ms = jnp.mean(x * x, axis=-1, keepdims=True)
    return x * lax.rsqrt(ms + NORM_EPS) * g


def _rope_chunk(c, cos, sa, sb):
    return c * cos + pltpu.roll(c, LANES - 16, 1) * sa + pltpu.roll(c, 16, 1) * sb


def _rope_tables(seq):
    t = jnp.arange(seq)
    pos_row = (t // GRID_W).astype(F32)
    pos_col = (t % GRID_W).astype(F32)
    inv = ROPE_BASE ** (-jnp.arange(0, 32, 2, dtype=F32) / 32)
    lane = jnp.arange(LANES)
    d = lane % 64
    r = d % 32
    first = (r < 16)[None, :]
    pos = jnp.where((d // 32)[None, :] == 0, pos_row[:, None], pos_col[:, None])
    ang = pos * inv[r % 16][None, :]
    cos, sin = jnp.cos(ang), jnp.sin(ang)
    return cos, jnp.where(first, -sin, 0.0), jnp.where(first, 0.0, sin)


def _token_specs(tm, d, mod_row):
    x_spec = pl.BlockSpec((1, tm, d), lambda b, t: (b, t, 0))
    if mod_row is None:
        mod_spec = pl.BlockSpec((1, N_MOD, d), lambda b, t: (b, 0, 0))
    else:
        mod_spec = pl.BlockSpec((1, N_MOD, d), lambda b, t: (mod_row, 0, 0))
    return x_spec, mod_spec


def _rope_specs(tm):
    return [pl.BlockSpec((tm, LANES), lambda b, t: (t, 0))] * 3


def _out_spec(tm, n):
    return pl.BlockSpec((1, tm, n), lambda b, t: (b, t, 0))


def _diff_proj_kernel(*refs, rope):
    if rope:
        x_ref, mod_ref, g_ref, w_ref, cos_ref, sa_ref, sb_ref, q_ref, k_ref, v_ref = refs
        cos, sa, sb = cos_ref[...], sa_ref[...], sb_ref[...]
    else:
        x_ref, mod_ref, g_ref, w_ref, q_ref, k_ref, v_ref = refs
    h = _norm_mod(x_ref[0], g_ref[...], mod_ref[0, 0:1, :], mod_ref[0, 1:2, :]).astype(BF16)
    for idx, o_ref in enumerate((q_ref, k_ref)):
        for c in range(DIFF_QK // 256):
            y = _dot(h, w_ref[:, idx * DIFF_QK + c * 256: idx * DIFF_QK + (c + 1) * 256])
            for s in range(2):
                ys = y[:, s * LANES:(s + 1) * LANES]
                if rope:
                    ys = _rope_chunk(ys, cos, sa, sb)
                if idx == 0:
                    ys = ys * DIFF_HEAD_DIM ** -0.5
                o_ref[0, :, c * 256 + s * LANES: c * 256 + (s + 1) * LANES] = ys.astype(BF16)
    for c in range(DIFF_V // 256):
        y = _dot(h, w_ref[:, 2 * DIFF_QK + c * 256: 2 * DIFF_QK + (c + 1) * 256])
        v_ref[0, :, c * 256:(c + 1) * 256] = y.astype(BF16)


def _diff_proj(x, mod, mod_row, norm_g, w, tables, tm):
    bx, s, d = x.shape
    rope = tables is not None
    x_spec, mod_spec = _token_specs(tm, d, mod_row)
    in_specs = [x_spec, mod_spec, _const_spec((1, d)), _const_spec(w.shape)]
    args = [x, mod, norm_g, w]
    if rope:
        in_specs += _rope_specs(tm)
        args += list(tables)
    out = jax.ShapeDtypeStruct((bx, s, DIFF_QK), BF16)
    return pl.pallas_call(
        functools.partial(_diff_proj_kernel, rope=rope),
        grid=(bx, s // tm),
        in_specs=in_specs,
        out_specs=[_out_spec(tm, DIFF_QK)] * 3,
        out_shape=[out, out, out],
        compiler_params=_params("parallel", "parallel"),
        name="diff_proj",
    )(*args)


def _flash_kernel(*refs, n_maps, has_prefix, lam_init):
    refs = list(refs)
    q_ref, k_ref, v_ref = refs[:3]
    pos = 3
    if has_prefix:
        kc_ref, vc_ref = refs[pos:pos + 2]
        pos += 2
    if n_maps == 2:
        lam_ref, ng_ref = refs[pos:pos + 2]
        pos += 2
    o_ref = refs[pos]
    scratch = refs[pos + 1:]
    if n_maps == 2:
        qm_ref, m_ref, acc_ref = scratch
    else:
        m_ref, acc_ref = scratch
    kv = pl.program_id(3)
    tq = q_ref.shape[1]

    def step(mi, k, v):
        q = qm_ref[mi] if n_maps == 2 else q_ref[0]
        s = _dot_nt(q, k)
        m_prev = m_ref[mi]
        m_new = jnp.maximum(m_prev, jnp.max(s, axis=1, keepdims=True))
        alpha = jnp.exp(m_prev - m_new)
        p = jnp.exp(s - jnp.tile(m_new, (1, s.shape[1] // LANES)))
        v1 = jnp.concatenate([v, jnp.ones_like(v)], axis=1)
        acc_ref[mi] = acc_ref[mi] * jnp.tile(alpha, (1, 2)) + _dot(p.astype(BF16), v1)
        m_ref[mi] = m_new

    @pl.when(kv == 0)
    def _init():
        m_ref[...] = jnp.full(m_ref.shape, NEG_BIG, F32)
        acc_ref[...] = jnp.zeros(acc_ref.shape, F32)
        if n_maps == 2:
            q = q_ref[0]
            lane = lax.broadcasted_iota(jnp.int32, q.shape, 1)
            qm_ref[0] = jnp.where(lane < DIFF_HEAD_DIM, q, jnp.zeros_like(q))
            qm_ref[1] = jnp.where(lane >= DIFF_HEAD_DIM, q, jnp.zeros_like(q))
        if has_prefix:
            for mi in range(n_maps):
                step(mi, kc_ref[0], vc_ref[0])

    for mi in range(n_maps):
        step(mi, k_ref[0], v_ref[0])

    @pl.when(kv == pl.num_programs(3) - 1)
    def _finish():
        o = acc_ref[0][:, :LANES] / acc_ref[0][:, LANES:]
        if n_maps == 2:
            lf = lam_ref[...]
            l1 = jnp.sum(lf[0:1] * lf[1:2], axis=1, keepdims=True)
            l2 = jnp.sum(lf[2:3] * lf[3:4], axis=1, keepdims=True)
            lam = jnp.exp(l1) - jnp.exp(l2) + lam_init
            o = o - lam * (acc_ref[1][:, :LANES] / acc_ref[1][:, LANES:])
            o = _rms(o, ng_ref[...]) * (1.0 - lam_init)
        o_ref[0] = o.astype(o_ref.dtype)


def _flash(q, k, v, prefix, extras, *, heads, dq, tq, tk, lam_init=0.0):
    b, s, _ = q.shape
    n_maps = 2 if extras is not None else 1
    in_specs = [
        pl.BlockSpec((1, tq, dq), lambda bi, h, qi, kv: (bi, qi, h)),
        pl.BlockSpec((1, tk, dq), lambda bi, h, qi, kv: (bi, kv, h)),
        pl.BlockSpec((1, tk, LANES), lambda bi, h, qi, kv: (bi, kv, h)),
    ]
    args = [q, k, v]
    if prefix is not None:
        cl = prefix[0].shape[1]
        in_specs += [
            pl.BlockSpec((1, cl, dq), lambda bi, h, qi, kv: (bi, 0, h)),
            pl.BlockSpec((1, cl, LANES), lambda bi, h, qi, kv: (bi, 0, h)),
        ]
        args += list(prefix)
    scratch = []
    if n_maps == 2:
        in_specs += [_const_spec(extras[0].shape), _const_spec(extras[1].shape)]
        args += list(extras)
        scratch.append(pltpu.VMEM((2, tq, dq), BF16))
    scratch += [pltpu.VMEM((n_maps, tq, LANES), F32), pltpu.VMEM((n_maps, tq, 2 * LANES), F32)]
    return pl.pallas_call(
        functools.partial(_flash_kernel, n_maps=n_maps, has_prefix=prefix is not None, lam_init=lam_init),
        grid=(b, heads, s // tq, k.shape[1] // tk),
        in_specs=in_specs,
        out_specs=pl.BlockSpec((1, tq, LANES), lambda bi, h, qi, kv: (bi, qi, h)),
        out_shape=jax.ShapeDtypeStruct((b, s, heads * LANES), BF16),
        scratch_shapes=scratch,
        compiler_params=_params("parallel", "parallel", "parallel", "arbitrary"),
        name="flash_diff" if n_maps == 2 else "flash_mla",
    )(*args)


def _out_proj_kernel(o_ref, w_ref, x_ref, mod_ref, out_ref):
    y = _dot(o_ref[0], w_ref[...])
    out_ref[0] = x_ref[0] + mod_ref[0, 2:3, :] * y


def _out_proj(o, w, x, mod, mod_row, tm):
    bx, s, d = x.shape
    x_spec, mod_spec = _token_specs(tm, d, mod_row)
    return pl.pallas_call(
        _out_proj_kernel,
        grid=(bx, s // tm),
        in_specs=[_out_spec(tm, o.shape[2]), _const_spec(w.shape), x_spec, mod_spec],
        out_specs=x_spec,
        out_shape=jax.ShapeDtypeStruct(x.shape, F32),
        compiler_params=_params("parallel", "parallel"),
        name="out_proj",
    )(o, w, x, mod)


def _gla_proj_kernel(x_ref, mod_ref, g_ref, w_ref, wr_ref, wg_ref, bg_ref, q_ref, k_ref, v_ref, og_ref, gate_ref):
    h = _norm_mod(x_ref[0], g_ref[...], mod_ref[0, 0:1, :], mod_ref[0, 1:2, :]).astype(BF16)
    col = 0
    for o_ref, width, scale in ((q_ref, GLA_KEY, GLA_DK ** -0.5), (k_ref, GLA_KEY, None),
                                (v_ref, GLA_VAL, None), (og_ref, GLA_VAL, None)):
        for c in range(width // 256):
            y = _dot(h, w_ref[:, col + c * 256: col + (c + 1) * 256])
            if scale is not None:
                y = y * scale
            o_ref[0, :, c * 256:(c + 1) * 256] = y.astype(BF16)
        col += width
    r_hi, r_lo = _split_bf16(_dot(h, wr_ref[...]))
    for c in range(2 * GLA_KEY // 256):
        wg = wg_ref[:, c * 256:(c + 1) * 256]
        z = _dot(r_hi, wg) + _dot(r_lo, wg) + bg_ref[:, c * 256:(c + 1) * 256]
        log_sig = jnp.minimum(z, 0.0) - jnp.log(1.0 + jnp.exp(-jnp.abs(z)))
        gate_ref[0, :, c * 256:(c + 1) * 256] = log_sig / GLA_GATE_NORM


def _gla_proj(x, mod, mod_row, norm_g, w, wr, wg, bg, tm):
    bx, s, d = x.shape
    x_spec, mod_spec = _token_specs(tm, d, mod_row)
    return pl.pallas_call(
        _gla_proj_kernel,
        grid=(bx, s // tm),
        in_specs=[x_spec, mod_spec, _const_spec((1, d)), _const_spec(w.shape), _const_spec(wr.shape),
                  _const_spec(wg.shape), _const_spec(bg.shape)],
        out_specs=[_out_spec(tm, GLA_KEY), _out_spec(tm, GLA_KEY), _out_spec(tm, GLA_VAL),
                   _out_spec(tm, GLA_VAL), _out_spec(tm, 2 * GLA_KEY)],
        out_shape=[jax.ShapeDtypeStruct((bx, s, GLA_KEY), BF16), jax.ShapeDtypeStruct((bx, s, GLA_KEY), BF16),
                   jax.ShapeDtypeStruct((bx, s, GLA_VAL), BF16), jax.ShapeDtypeStruct((bx, s, GLA_VAL), BF16),
                   jax.ShapeDtypeStruct((bx, s, 2 * GLA_KEY), F32)],
        compiler_params=_params("parallel", "parallel"),
        name="gla_proj",
    )(x, mod, norm_g, w, wr, wg, bg)


def _gla_scan_kernel(q_ref, k_ref, v_ref, g_ref, s0_ref, o_ref, sfin_ref, st_ref):
    direction = pl.program_id(2)
    i = pl.program_id(3)
    n_chunks = q_ref.shape[1] // GLA_CHUNK

    @pl.when(i == 0)
    def _load_state():
        st_ref[...] = s0_ref[0, 0, 0]

    def run(backward):
        row = lax.broadcasted_iota(jnp.int32, (GLA_CHUNK, GLA_CHUNK), 0)
        colm = lax.broadcasted_iota(jnp.int32, (GLA_CHUNK, GLA_CHUNK), 1)
        allowed = (colm >= row) if backward else (colm <= row)
        cmat = jnp.where(allowed, 1.0, 0.0).astype(BF16)
        order = range(n_chunks - 1, -1, -1) if backward else range(n_chunks)
        for c in order:
            sl = slice(c * GLA_CHUNK, (c + 1) * GLA_CHUNK)
            q = q_ref[0, sl, :].astype(F32)
            k = k_ref[0, sl, :].astype(F32)
            v = v_ref[0, sl, :]
            g_hi, g_lo = _split_bf16(g_ref[0, sl, :])
            cum = _dot(cmat, g_hi) + _dot(cmat, g_lo)
            tot = cum[0:1] if backward else cum[GLA_CHUNK - 1:GLA_CHUNK]
            q_dec = (q * jnp.exp(cum)).astype(BF16)
            k_inv = (k * jnp.exp(-cum)).astype(BF16)
            k_end = (k * jnp.exp(tot - cum)).astype(BF16)
            a = jnp.where(allowed, _dot_nt(q_dec, k_inv), 0.0)
            st = st_ref[...]
            o_ref[0, 0, sl, :] = _dot(a.astype(BF16), v) + _dot_nt(q_dec, st.astype(BF16))
            st_ref[...] = st * jnp.exp(tot) + _dot_tn(v, k_end)

    pl.when(direction == 0)(lambda: run(False))
    pl.when(direction == 1)(lambda: run(True))

    @pl.when(i == pl.num_programs(3) - 1)
    def _store_state():
        sfin_ref[0, 0, 0] = st_ref[...]


def _gla_scan(q, k, v, g, s0, tb):
    b, s, _ = q.shape
    nb = s // tb

    def blk(d, i):
        return i + d * (nb - 1 - 2 * i)

    st_spec = pl.BlockSpec((1, 1, 1, GLA_DV, GLA_DK), lambda bi, h, d, i: (bi, h, d, 0, 0))
    return pl.pallas_call(
        _gla_scan_kernel,
        grid=(b, GLA_HEADS, 2, nb),
        in_specs=[
            pl.BlockSpec((1, tb, GLA_DK), lambda bi, h, d, i: (bi, blk(d, i), h)),
            pl.BlockSpec((1, tb, GLA_DK), lambda bi, h, d, i: (bi, blk(d, i), h)),
            pl.BlockSpec((1, tb, GLA_DV), lambda bi, h, d, i: (bi, blk(d, i), h)),
            pl.BlockSpec((1, tb, GLA_DK), lambda bi, h, d, i: (bi, blk(d, i), d * GLA_HEADS + h)),
            st_spec,
        ],
        out_specs=[pl.BlockSpec((1, 1, tb, GLA_DV), lambda bi, h, d, i: (d, bi, blk(d, i), h)), st_spec],
        out_shape=[jax.ShapeDtypeStruct((2, b, s, GLA_VAL), F32),
                   jax.ShapeDtypeStruct((b, GLA_HEADS, 2, GLA_DV, GLA_DK), F32)],
        scratch_shapes=[pltpu.VMEM((GLA_DV, GLA_DK), F32)],
        compiler_params=_params("parallel", "parallel", "parallel", "arbitrary"),
        name="gla_scan",
    )(q, k, v, g, s0)


def _gla_out_kernel(of_ref, ob_ref, og_ref, ng_ref, w_ref, x_ref, mod_ref, out_ref):
    o = of_ref[0, 0] + ob_ref[0, 0]
    parts = []
    for h in range(GLA_HEADS):
        sl = slice(h * GLA_DV, (h + 1) * GLA_DV)
        parts.append((_rms(o[:, sl], ng_ref[...]) * _silu(og_ref[0, :, sl].astype(F32))).astype(BF16))
    y = _dot(jnp.concatenate(parts, axis=1), w_ref[...])
    out_ref[0] = x_ref[0] + mod_ref[0, 2:3, :] * y


def _gla_out(o2, og, norm_g, w, x, mod, mod_row, tm):
    bx, s, d = x.shape
    x_spec, mod_spec = _token_specs(tm, d, mod_row)
    o2 = o2.reshape(2, bx, s, GLA_VAL)
    return pl.pallas_call(
        _gla_out_kernel,
        grid=(bx, s // tm),
        in_specs=[pl.BlockSpec((1, 1, tm, GLA_VAL), lambda b, t: (0, b, t, 0)),
                  pl.BlockSpec((1, 1, tm, GLA_VAL), lambda b, t: (1, b, t, 0)),
                  _out_spec(tm, GLA_VAL), _const_spec(norm_g.shape), _const_spec(w.shape), x_spec, mod_spec],
        out_specs=x_spec,
        out_shape=jax.ShapeDtypeStruct(x.shape, F32),
        compiler_params=_params("parallel", "parallel"),
        name="gla_out",
    )(o2, o2, og, norm_g, w, x, mod)


def _mla_proj_kernel(*refs, rope):
    if rope:
        (x_ref, mod_ref, g_ref, w_ref, qg_ref, wq_ref, kvg_ref, wkn_ref, wv_ref,
         cos_ref, sa_ref, sb_ref, q_ref, k_ref, v_ref) = refs
        cos, sa, sb = cos_ref[...], sa_ref[...], sb_ref[...]
    else:
        x_ref, mod_ref, g_ref, w_ref, qg_ref, wq_ref, kvg_ref, wkn_ref, wv_ref, q_ref, k_ref, v_ref = refs
    h = _norm_mod(x_ref[0], g_ref[...], mod_ref[0, 0:1, :], mod_ref[0, 1:2, :]).astype(BF16)
    y = _dot(h, w_ref[...])
    c_q = _rms(y[:, :MLA_Q_LORA], qg_ref[...]).astype(BF16)
    c_kv = _rms(y[:, MLA_Q_LORA:MLA_Q_LORA + MLA_KV_LORA], kvg_ref[...]).astype(BF16)
    k_rope = y[:, MLA_Q_LORA + MLA_KV_LORA:]
    if rope:
        k_rope = _rope_chunk(k_rope, cos, sa, sb)
    k_rope = k_rope.astype(BF16)
    for hd in range(MLA_HEADS):
        base = hd * MLA_QK_PAD
        qh = _dot(c_q, wq_ref[:, base: base + MLA_QK_PAD])
        q_rope = qh[:, LANES:]
        if rope:
            q_rope = _rope_chunk(q_rope, cos, sa, sb)
        q_ref[0, :, base: base + LANES] = (qh[:, :LANES] * MLA_SCALE).astype(BF16)
        q_ref[0, :, base + LANES: base + MLA_QK_PAD] = (q_rope * MLA_SCALE).astype(BF16)
        k_ref[0, :, base + LANES: base + MLA_QK_PAD] = k_rope
    for c in range(MLA_HEADS * MLA_NOPE // 256):
        kn = _dot(c_kv, wkn_ref[:, c * 256:(c + 1) * 256]).astype(BF16)
        for s in range(2):
            hd = 2 * c + s
            k_ref[0, :, hd * MLA_QK_PAD: hd * MLA_QK_PAD + LANES] = kn[:, s * LANES:(s + 1) * LANES]
        v_ref[0, :, c * 256:(c + 1) * 256] = _dot(c_kv, wv_ref[:, c * 256:(c + 1) * 256]).astype(BF16)


def _mla_proj(x, mod, mod_row, norm_g, w, qg, wq, kvg, wkn, wv, tables, tm):
    bx, s, d = x.shape
    rope = tables is not None
    x_spec, mod_spec = _token_specs(tm, d, mod_row)
    consts = [norm_g, w, qg, wq, kvg, wkn, wv]
    in_specs = [x_spec, mod_spec] + [_const_spec(a.shape) for a in consts]
    args = [x, mod] + consts
    if rope:
        in_specs += _rope_specs(tm)
        args += list(tables)
    qk_w = MLA_HEADS * MLA_QK_PAD
    return pl.pallas_call(
        functools.partial(_mla_proj_kernel, rope=rope),
        grid=(bx, s // tm),
        in_specs=in_specs,
        out_specs=[_out_spec(tm, qk_w), _out_spec(tm, qk_w), _out_spec(tm, MLA_HEADS * MLA_V)],
        out_shape=[jax.ShapeDtypeStruct((bx, s, qk_w), BF16), jax.ShapeDtypeStruct((bx, s, qk_w), BF16),
                   jax.ShapeDtypeStruct((bx, s, MLA_HEADS * MLA_V), BF16)],
        compiler_params=_params("parallel", "parallel"),
        name="mla_proj",
    )(*args)


def _top2_sum(a, b, c, d):
    hi1, lo1 = jnp.maximum(a, b), jnp.minimum(a, b)
    hi2, lo2 = jnp.maximum(c, d), jnp.minimum(c, d)
    return jnp.maximum(hi1, hi2) + jnp.maximum(jnp.minimum(hi1, hi2), jnp.maximum(lo1, lo2))


def _route_rows(scores, sel):
    eg = EXPERTS_PER_GROUP
    gs = [_top2_sum(*sel[g * eg:(g + 1) * eg]) for g in range(N_GROUPS)]
    best = jnp.maximum(jnp.maximum(gs[0], gs[1]), jnp.maximum(gs[2], gs[3]))
    gidx = jnp.where(gs[0] >= best, 0, jnp.where(gs[1] >= best, 1, jnp.where(gs[2] >= best, 2, 3)))

    def pick(rows, j):
        out = rows[(N_GROUPS - 1) * eg + j]
        for g in range(N_GROUPS - 2, -1, -1):
            out = jnp.where(gidx == g, rows[g * eg + j], out)
        return out

    loc_sel = [pick(sel, j) for j in range(eg)]
    loc_sc = [pick(scores, j) for j in range(eg)]
    weights = []
    for j in range(eg):
        rank = jnp.zeros_like(loc_sel[j])
        for i in range(eg):
            if i == j:
                continue
            beats = (loc_sel[i] >= loc_sel[j]) if i < j else (loc_sel[i] > loc_sel[j])
            rank = rank + jnp.where(beats, 1.0, 0.0)
        weights.append(jnp.where(rank < 1.5, loc_sc[j], 0.0))
    den = weights[0] + weights[1] + weights[2] + weights[3]
    weights = [w / den for w in weights]
    return [jnp.where(gidx == e // eg, weights[e % eg], 0.0) for e in range(N_EXPERTS)]


def _moe_kernel(x_ref, mod_ref, g_ref, rwh_ref, rwl_ref, rb_ref, shu_ref, shd_ref, exp_ref, wup_ref, wdn_ref,
                fg_ref, out_ref, h_ref, comb_ref, acc_ref, *, final_norm):
    grp = pl.program_id(2)

    @pl.when(grp == 0)
    def _prepare():
        hf = _norm_mod(x_ref[0], g_ref[...], mod_ref[0, 3:4, :], mod_ref[0, 4:5, :])
        h_hi, h_lo = _split_bf16(hf)
        h_ref[...] = h_hi
        logits = _dot_nt(rwh_ref[...], h_hi) + _dot_nt(rwh_ref[...], h_lo) + _dot_nt(rwl_ref[...], h_hi)
        scores = jax.nn.sigmoid(logits)
        sel = scores + rb_ref[...]
        rows = _route_rows([scores[e:e + 1, :] for e in range(N_EXPERTS)],
                           [sel[e:e + 1, :] for e in range(N_EXPERTS)])
        comb_ref[...] = jnp.concatenate(rows, axis=0)
        a = _dot(h_hi, shu_ref[:, :SHARED_FF])
        u = _dot(h_hi, shu_ref[:, SHARED_FF:])
        acc_ref[...] = _dot((_silu(a) * u).astype(BF16), shd_ref[...])

    h = h_ref[...]
    c_hi, c_lo = _split_bf16(comb_ref[...])
    acts = []
    for e in range(EXPERTS_PER_GROUP):
        a = _dot(h, wup_ref[0, :, e * EXPERT_FF:(e + 1) * EXPERT_FF])
        u = _dot(h, wup_ref[0, :, (EXPERTS_PER_GROUP + e) * EXPERT_FF:(EXPERTS_PER_GROUP + e + 1) * EXPERT_FF])
        ex = exp_ref[0, :, e * EXPERT_FF:(e + 1) * EXPERT_FF]
        scale = _dot_tn(c_hi, ex) + _dot_tn(c_lo, ex)
        acts.append((_silu(a) * u * scale).astype(BF16))
    acc_ref[...] += _dot(jnp.concatenate(acts, axis=1), wdn_ref[0])

    @pl.when(grp == pl.num_programs(2) - 1)
    def _finish():
        out = x_ref[0] + mod_ref[0, 5:6, :] * acc_ref[...]
        if final_norm:
            out = _rms(out, fg_ref[...])
        out_ref[0] = out


def _moe(x, mod, mod_row, norm_g, rw_hi, rw_lo, rb, sh_up, sh_dn, expand, w_up, w_dn, final_g, final_norm, tm):
    bx, s, d = x.shape
    if mod_row is None:
        mod_map = lambda b, t, g: (b, 0, 0)
    else:
        mod_map = lambda b, t, g: (mod_row, 0, 0)
    x_spec = pl.BlockSpec((1, tm, d), lambda b, t, g: (b, t, 0))
    consts = [norm_g, rw_hi, rw_lo, rb, sh_up, sh_dn]
    in_specs = ([x_spec, pl.BlockSpec((1, N_MOD, d), mod_map)] + [_const_spec(a.shape) for a in consts] + [
        pl.BlockSpec((1,) + expand.shape[1:], lambda b, t, g: (g, 0, 0)),
        pl.BlockSpec((1,) + w_up.shape[1:], lambda b, t, g: (g, 0, 0)),
        pl.BlockSpec((1,) + w_dn.shape[1:], lambda b, t, g: (g, 0, 0)),
        _const_spec(final_g.shape)])
    return pl.pallas_call(
        functools.partial(_moe_kernel, final_norm=final_norm),
        grid=(bx, s // tm, N_GROUPS),
        in_specs=in_specs,
        out_specs=x_spec,
        out_shape=jax.ShapeDtypeStruct(x.shape, F32),
        scratch_shapes=[pltpu.VMEM((tm, d), BF16), pltpu.VMEM((N_EXPERTS, tm), F32), pltpu.VMEM((tm, d), F32)],
        compiler_params=_params("parallel", "parallel", "arbitrary"),
        name="moe",
    )(x, mod, *consts, expand, w_up, w_dn, final_g)


def _diff_weights(w_in):
    d = w_in.shape[0]

    def regroup(w):
        return w.reshape(d, 2, DIFF_HEADS, DIFF_HEAD_DIM).transpose(0, 2, 1, 3).reshape(d, DIFF_QK)

    return jnp.concatenate([regroup(w_in[:, :DIFF_QK]), regroup(w_in[:, DIFF_QK:2 * DIFF_QK]),
                            w_in[:, 2 * DIFF_QK:]], axis=1).astype(BF16)


def _gla_weights(w_in, gate_w, gate_b):
    d = w_in.shape[0]
    main = 2 * GLA_KEY + 2 * GLA_VAL
    w_r = jnp.zeros((d, LANES), F32).at[:, :2 * GLA_GATE_RANK].set(w_in[:, main:]).astype(BF16)
    w_g = jnp.zeros((LANES, 2 * GLA_KEY), F32)
    w_g = w_g.at[:GLA_GATE_RANK, :GLA_KEY].set(gate_w[0])
    w_g = w_g.at[GLA_GATE_RANK:2 * GLA_GATE_RANK, GLA_KEY:].set(gate_w[1]).astype(BF16)
    return w_in[:, :main].astype(BF16), w_r, w_g, gate_b.reshape(1, 2 * GLA_KEY)


def _mla_weights(w_in, w_uq, w_ukv):
    d = w_in.shape[0]
    w = jnp.zeros((d, MLA_Q_LORA + MLA_KV_LORA + LANES), F32).at[:, :w_in.shape[1]].set(w_in).astype(BF16)
    wq = w_uq.reshape(MLA_Q_LORA, MLA_HEADS, MLA_NOPE + MLA_ROPE)
    wq = jnp.pad(wq, ((0, 0), (0, 0), (0, MLA_QK_PAD - MLA_NOPE - MLA_ROPE)))
    wq = wq.reshape(MLA_Q_LORA, MLA_HEADS * MLA_QK_PAD).astype(BF16)
    wkv = w_ukv.reshape(MLA_KV_LORA, MLA_HEADS, MLA_NOPE + MLA_V)
    wkn = wkv[:, :, :MLA_NOPE].reshape(MLA_KV_LORA, MLA_HEADS * MLA_NOPE).astype(BF16)
    wv = wkv[:, :, MLA_NOPE:].reshape(MLA_KV_LORA, MLA_HEADS * MLA_V).astype(BF16)
    return w, wq, wkn, wv


def _moe_weights(w_up, w_down):
    e, d, _ = w_up.shape
    eg = EXPERTS_PER_GROUP
    wu = w_up.reshape(N_GROUPS, eg, d, 2, EXPERT_FF).transpose(0, 2, 3, 1, 4).reshape(N_GROUPS, d, 2 * eg * EXPERT_FF)
    return wu.astype(BF16), w_down.reshape(N_GROUPS, eg * EXPERT_FF, d).astype(BF16)


def _expand_matrix():
    e = jnp.arange(N_EXPERTS)[None, :, None]
    g = jnp.arange(N_GROUPS)[:, None, None]
    c = jnp.arange(EXPERTS_PER_GROUP * EXPERT_FF)[None, None, :]
    return (e == g * EXPERTS_PER_GROUP + c // EXPERT_FF).astype(BF16)


def kernel(x, c, ctx, c_ctx, ada_w, ada_b, norm_g, router_w, router_b, moe_w_up, moe_w_down, shared_w_up,
           shared_w_down, diff_w_in, diff_lam, diff_norm_g, diff_w_out, gla_w_in, gla_gate_w, gla_gate_b,
           gla_norm_g, gla_w_out, mla_w_in, mla_q_norm_g, mla_w_uq, mla_kv_norm_g, mla_w_ukv, mla_w_out,
           final_norm_g):
    b, s, d = x.shape
    cl = ctx.shape[1]
    assert b + 1 <= COND_ROWS and d == D_MODEL
    depth = ada_w.shape[0]
    ctx_row = b

    cond = jnp.zeros((COND_ROWS, d), F32).at[:b].set(c).at[ctx_row].set(c_ctx)
    mods = _ada_params(cond, ada_w, ada_b).reshape(depth, COND_ROWS, N_MOD, d)

    tables = _rope_tables(s)
    rw_hi, rw_lo = _split_bf16(router_w.T)
    rb = router_b.reshape(N_EXPERTS, 1)
    expand = _expand_matrix()
    final_g = final_norm_g.reshape(1, d)

    tm_lat = 512
    n_ctx = b * cl
    tm_ctx = 512 if n_ctx % 512 == 0 else cl
    x_lat = x
    x_ctx = ctx.reshape(1, n_ctx, d)

    for i in range(depth):
        need_ctx = i < depth - 1
        mod = mods[i]
        g1 = norm_g[i, 0].reshape(1, d)
        g2 = norm_g[i, 1].reshape(1, d)
        kind, j = i % N_MIXERS, i // N_MIXERS
        y_ctx = None
        if kind == 0:
            lam_init = 0.8 - 0.6 * math.exp(-0.3 * i)
            w = _diff_weights(diff_w_in[j])
            w_out = diff_w_out[j].astype(BF16)
            extras = (diff_lam[j], diff_norm_g[j].reshape(1, 2 * DIFF_HEAD_DIM))
            q_l, k_l, v_l = _diff_proj(x_lat, mod, None, g1, w, tables, tm_lat)
            q_c, k_c, v_c = (t.reshape(b, cl, -1) for t in _diff_proj(x_ctx, mod, ctx_row, g1, w, None, tm_ctx))
            o_lat = _flash(q_l, k_l, v_l, (k_c, v_c), extras, heads=DIFF_HEADS, dq=2 * DIFF_HEAD_DIM,
                           tq=512, tk=1024, lam_init=lam_init)
            x_lat = _out_proj(o_lat, w_out, x_lat, mod, None, tm_lat)
            if need_ctx:
                o_ctx = _flash(q_c, k_c, v_c, None, extras, heads=DIFF_HEADS, dq=2 * DIFF_HEAD_DIM,
                               tq=cl, tk=cl, lam_init=lam_init)
                x_ctx = _out_proj(o_ctx.reshape(1, n_ctx, -1), w_out, x_ctx, mod, ctx_row, tm_ctx)
        elif kind == 1:
            w, w_r, w_g, b_g = _gla_weights(gla_w_in[j], gla_gate_w[j], gla_gate_b[j])
            w_out = gla_w_out[j].astype(BF16)
            ng = gla_norm_g[j].reshape(1, GLA_DV)
            q_l, k_l, v_l, og_l, gt_l = _gla_proj(x_lat, mod, None, g1, w, w_r, w_g, b_g, tm_lat)
            q_c, k_c, v_c, og_c, gt_c = (t.reshape(b, cl, -1)
                                         for t in _gla_proj(x_ctx, mod, ctx_row, g1, w, w_r, w_g, b_g, tm_ctx))
            s0 = jnp.zeros((b, GLA_HEADS, 2, GLA_DV, GLA_DK), F32)
            o_c, s_c = _gla_scan(q_c, k_c, v_c, gt_c, s0, cl)
            o_l, _ = _gla_scan(q_l, k_l, v_l, gt_l, s_c, 256)
            x_lat = _gla_out(o_l, og_l, ng, w_out, x_lat, mod, None, tm_lat)
            if need_ctx:
                x_ctx = _gla_out(o_c.reshape(2, 1, n_ctx, -1), og_c.reshape(1, n_ctx, -1), ng, w_out, x_ctx, mod,
                                 ctx_row, tm_ctx)
        else:
            w, wq, wkn, wv = _mla_weights(mla_w_in[j], mla_w_uq[j], mla_w_ukv[j])
            w_out = mla_w_out[j].astype(BF16)
            qg = mla_q_norm_g[j].reshape(1, MLA_Q_LORA)
            kvg = mla_kv_norm_g[j].reshape(1, MLA_KV_LORA)
            q_l, k_l, v_l = _mla_proj(x_lat, mod, None, g1, w, qg, wq, kvg, wkn, wv, tables, tm_lat)
            q_c, k_c, v_c = (t.reshape(b, cl, -1)
                             for t in _mla_proj(x_ctx, mod, ctx_row, g1, w, qg, wq, kvg, wkn, wv, None, tm_ctx))
            o_lat = _flash(q_l, k_l, v_l, (k_c, v_c), None, heads=MLA_HEADS, dq=MLA_QK_PAD, tq=512, tk=1024)
            x_lat = _out_proj(o_lat, w_out, x_lat, mod, None, tm_lat)
            if need_ctx:
                o_ctx = _flash(q_c, k_c, v_c, None, None, heads=MLA_HEADS, dq=MLA_QK_PAD, tq=cl, tk=cl)
                x_ctx = _out_proj(o_ctx.reshape(1, n_ctx, -1), w_out, x_ctx, mod, ctx_row, tm_ctx)

        w_up, w_dn = _moe_weights(moe_w_up[i], moe_w_down[i])
        sh_up = shared_w_up[i].astype(BF16)
        sh_dn = shared_w_down[i].astype(BF16)
        last = i == depth - 1
        x_lat = _moe(x_lat, mod, None, g2, rw_hi, rw_lo, rb, sh_up, sh_dn, expand, w_up, w_dn, final_g, last, tm_lat)
        if need_ctx:
            x_ctx = _moe(x_ctx, mod, ctx_row, g2, rw_hi, rw_lo, rb, sh_up, sh_dn, expand, w_up, w_dn, final_g,
                         False, tm_ctx)
    return x_lat
```

```python
import functools
import math

import jax
import jax.numpy as jnp
from jax import lax
from jax.experimental import pallas as pl
from jax.experimental.pallas import tpu as pltpu

F32 = jnp.float32
BF16 = jnp.bfloat16

D_MODEL = 1024
DEPTH = 4
GRID_W = 64
N_MIXERS = 3
NORM_EPS = 1e-6
ROPE_BASE = 10000.0
N_MOD = 6

DIFF_HEADS = 8
DIFF_HEAD_DIM = 64
DIFF_QK = 2 * DIFF_HEADS * DIFF_HEAD_DIM
DIFF_V = 2 * DIFF_HEADS * DIFF_HEAD_DIM

GLA_HEADS = 4
GLA_KEY = D_MODEL // 2
GLA_VAL = D_MODEL
GLA_DK = GLA_KEY // GLA_HEADS
GLA_DV = GLA_VAL // GLA_HEADS
GLA_GATE_RANK = 16
GLA_GATE_NORM = 16.0
GLA_CHUNK = 64

MLA_HEADS = 8
MLA_Q_LORA = 384
MLA_KV_LORA = 256
MLA_NOPE = 128
MLA_ROPE = 64
MLA_V = 128
MLA_SCALE = (MLA_NOPE + MLA_ROPE) ** -0.5
MLA_QK_PAD = 256

N_EXPERTS = 16
N_GROUPS = 4
EXPERTS_PER_GROUP = N_EXPERTS // N_GROUPS
EXPERT_FF = 256
SHARED_FF = 256

LANES = 128
COND_ROWS = 16
NEG_BIG = -1e30
LOG2E = math.log2(math.e)
VMEM_LIMIT = 56 * 1024 * 1024

NT_DIMS = (((1,), (1,)), ((), ()))
TN_DIMS = (((0,), (0,)), ((), ()))


def _dot(a, b):
    return jnp.dot(a, b, preferred_element_type=F32)


def _dot_nt(a, b):
    return lax.dot_general(a, b, NT_DIMS, preferred_element_type=F32)


def _dot_tn(a, b):
    return lax.dot_general(a, b, TN_DIMS, preferred_element_type=F32)


def _split_bf16(a):
    hi = a.astype(BF16)
    lo = (a - hi.astype(F32)).astype(BF16)
    return hi, lo


def _silu(a):
    return a * jax.nn.sigmoid(a)


def _params(*sem):
    return pltpu.CompilerParams(dimension_semantics=sem, vmem_limit_bytes=VMEM_LIMIT)


def _const_spec(shape):
    nd = len(shape)
    return pl.BlockSpec(shape, lambda *_: (0,) * nd)


def _ada_kernel(cond_ref, w_ref, b_ref, o_ref):
    a_hi, a_lo = _split_bf16(_silu(cond_ref[...]))
    w_hi, w_lo = _split_bf16(w_ref[0])
    o_ref[0] = _dot(a_hi, w_hi) + _dot(a_lo, w_hi) + _dot(a_hi, w_lo) + b_ref[0]


def _ada_params(cond, ada_w, ada_b):
    depth, d, n = ada_w.shape
    tn = 1024
    return pl.pallas_call(
        _ada_kernel,
        grid=(depth, n // tn),
        in_specs=[
            pl.BlockSpec((COND_ROWS, d), lambda i, j: (0, 0)),
            pl.BlockSpec((1, d, tn), lambda i, j: (i, 0, j)),
            pl.BlockSpec((1, 1, tn), lambda i, j: (i, 0, j)),
        ],
        out_specs=pl.BlockSpec((1, COND_ROWS, tn), lambda i, j: (i, 0, j)),
        out_shape=jax.ShapeDtypeStruct((depth, COND_ROWS, n), F32),
        compiler_params=_params("parallel", "parallel"),
        name="ada_params",
    )(cond, ada_w, ada_b.reshape(depth, 1, n))


def _norm_mod(x, g, shift, scale):
    ms = jnp.mean(x * x, axis=-1, keepdims=True)
    return (x * lax.rsqrt(ms + NORM_EPS) * g) * (1.0 + scale) + shift


def _rms(x, g):
    ms = jnp.mean(x * x, axis=-1, keepdims=True)
    return x * lax.rsqrt(ms + NORM_EPS) * g


def _rope_chunk(c, cos, sa, sb):
    return c * cos + pltpu.roll(c, LANES - 16, 1) * sa + pltpu.roll(c, 16, 1) * sb


def _rope_tables(seq):
    t = jnp.arange(seq)
    pos_row = (t // GRID_W).astype(F32)
    pos_col = (t % GRID_W).astype(F32)
    inv = ROPE_BASE ** (-jnp.arange(0, 32, 2, dtype=F32) / 32)
    lane = jnp.arange(LANES)
    d = lane % 64
    r = d % 32
    first = (r < 16)[None, :]
    pos = jnp.where((d // 32)[None, :] == 0, pos_row[:, None], pos_col[:, None])
    ang = pos * inv[r % 16][None, :]
    cos, sin = jnp.cos(ang), jnp.sin(ang)
    return cos, jnp.where(first, -sin, 0.0), jnp.where(first, 0.0, sin)


def _token_specs(tm, d, mod_row):
    x_spec = pl.BlockSpec((1, tm, d), lambda b, t: (b, t, 0))
    if mod_row is None:
        mod_spec = pl.BlockSpec((1, N_MOD, d), lambda b, t: (b, 0, 0))
    else:
        mod_spec = pl.BlockSpec((1, N_MOD, d), lambda b, t: (mod_row, 0, 0))
    return x_spec, mod_spec


def _rope_specs(tm):
    return [pl.BlockSpec((tm, LANES), lambda b, t: (t, 0))] * 3


def _out_spec(tm, n):
    return pl.BlockSpec((1, tm, n), lambda b, t: (b, t, 0))


def _diff_proj_kernel(*refs, rope):
    if rope:
        x_ref, mod_ref, g_ref, w_ref, cos_ref, sa_ref, sb_ref, q_ref, k_ref, v_ref = refs
        cos, sa, sb = cos_ref[...], sa_ref[...], sb_ref[...]
    else:
        x_ref, mod_ref, g_ref, w_ref, q_ref, k_ref, v_ref = refs
    h = _norm_mod(x_ref[0], g_ref[...], mod_ref[0, 0:1, :], mod_ref[0, 1:2, :]).astype(BF16)
    for idx, o_ref in enumerate((q_ref, k_ref)):
        for c in range(DIFF_QK // 256):
            y = _dot(h, w_ref[:, idx * DIFF_QK + c * 256: idx * DIFF_QK + (c + 1) * 256])
            for s in range(2):
                ys = y[:, s * LANES:(s + 1) * LANES]
                if rope:
                    ys = _rope_chunk(ys, cos, sa, sb)
                if idx == 0:
                    ys = ys * (DIFF_HEAD_DIM ** -0.5 * LOG2E)
                o_ref[0, :, c * 256 + s * LANES: c * 256 + (s + 1) * LANES] = ys.astype(BF16)
    for c in range(DIFF_V // 256):
        y = _dot(h, w_ref[:, 2 * DIFF_QK + c * 256: 2 * DIFF_QK + (c + 1) * 256])
        v_ref[0, :, c * 256:(c + 1) * 256] = y.astype(BF16)


def _diff_proj(x, mod, mod_row, norm_g, w, tables, tm):
    bx, s, d = x.shape
    rope = tables is not None
    x_spec, mod_spec = _token_specs(tm, d, mod_row)
    in_specs = [x_spec, mod_spec, _const_spec((1, d)), _const_spec(w.shape)]
    args = [x, mod, norm_g, w]
    if rope:
        in_specs += _rope_specs(tm)
        args += list(tables)
    out = jax.ShapeDtypeStruct((bx, s, DIFF_QK), BF16)
    return pl.pallas_call(
        functools.partial(_diff_proj_kernel, rope=rope),
        grid=(bx, s // tm),
        in_specs=in_specs,
        out_specs=[_out_spec(tm, DIFF_QK)] * 3,
        out_shape=[out, out, out],
        compiler_params=_params("parallel", "parallel"),
        name="diff_proj",
    )(*args)


def _flash_kernel(*refs, n_maps, has_prefix, lam_init, hb, dq):
    refs = list(refs)
    q_ref, k_ref, v_ref = refs[:3]
    pos = 3
    if has_prefix:
        kc_ref, vc_ref = refs[pos:pos + 2]
        pos += 2
    if n_maps == 2:
        lam_ref, ng_ref = refs[pos:pos + 2]
        pos += 2
    o_ref = refs[pos]
    scratch = refs[pos + 1:]
    if n_maps == 2:
        qm_ref, m_ref, acc_ref = scratch
    else:
        m_ref, acc_ref = scratch
    kv = pl.program_id(3)

    def step(hd, mi, k_src, v_src):
        ci = hd * n_maps + mi
        q = qm_ref[ci] if n_maps == 2 else q_ref[0, :, hd * dq:(hd + 1) * dq]
        k = k_src[0, :, hd * dq:(hd + 1) * dq]
        v = v_src[0, :, hd * LANES:(hd + 1) * LANES]
        s = _dot_nt(q, k)
        m_prev = m_ref[ci]
        m_new = jnp.maximum(m_prev, jnp.max(s, axis=1, keepdims=True))
        alpha = jnp.exp2(m_prev - m_new)
        p = jnp.exp2(s - jnp.tile(m_new, (1, s.shape[1] // LANES)))
        v1 = jnp.concatenate([v, jnp.ones_like(v)], axis=1)
        acc_ref[ci] = acc_ref[ci] * jnp.tile(alpha, (1, 2)) + _dot(p.astype(BF16), v1)
        m_ref[ci] = m_new

    def all_steps(k_src, v_src):
        for hd in range(hb):
            for mi in range(n_maps):
                step(hd, mi, k_src, v_src)

    @pl.when(kv == 0)
    def _init():
        m_ref[...] = jnp.full(m_ref.shape, NEG_BIG, F32)
        acc_ref[...] = jnp.zeros(acc_ref.shape, F32)
        if n_maps == 2:
            for hd in range(hb):
                q = q_ref[0, :, hd * dq:(hd + 1) * dq]
                lane = lax.broadcasted_iota(jnp.int32, q.shape, 1)
                qm_ref[2 * hd] = jnp.where(lane < DIFF_HEAD_DIM, q, jnp.zeros_like(q))
                qm_ref[2 * hd + 1] = jnp.where(lane >= DIFF_HEAD_DIM, q, jnp.zeros_like(q))
        if has_prefix:
            all_steps(kc_ref, vc_ref)

    all_steps(k_ref, v_ref)

    @pl.when(kv == pl.num_programs(3) - 1)
    def _finish():
        if n_maps == 2:
            lf = lam_ref[...]
            l1 = jnp.sum(lf[0:1] * lf[1:2], axis=1, keepdims=True)
            l2 = jnp.sum(lf[2:3] * lf[3:4], axis=1, keepdims=True)
            lam = jnp.exp(l1) - jnp.exp(l2) + lam_init
        for hd in range(hb):
            a0 = acc_ref[hd * n_maps]
            o = a0[:, :LANES] / a0[:, LANES:]
            if n_maps == 2:
                a1 = acc_ref[hd * n_maps + 1]
                o = o - lam * (a1[:, :LANES] / a1[:, LANES:])
                o = _rms(o, ng_ref[...]) * (1.0 - lam_init)
            o_ref[0, :, hd * LANES:(hd + 1) * LANES] = o.astype(o_ref.dtype)


def _flash(q, k, v, prefix, extras, *, heads, dq, tq, tk, lam_init=0.0, hb=1):
    b, s, _ = q.shape
    n_maps = 2 if extras is not None else 1
    in_specs = [
        pl.BlockSpec((1, tq, hb * dq), lambda bi, h, qi, kv: (bi, qi, h)),
        pl.BlockSpec((1, tk, hb * dq), lambda bi, h, qi, kv: (bi, kv, h)),
        pl.BlockSpec((1, tk, hb * LANES), lambda bi, h, qi, kv: (bi, kv, h)),
    ]
    args = [q, k, v]
    if prefix is not None:
        cl = prefix[0].shape[1]
        in_specs += [
            pl.BlockSpec((1, cl, hb * dq), lambda bi, h, qi, kv: (bi, 0, h)),
            pl.BlockSpec((1, cl, hb * LANES), lambda bi, h, qi, kv: (bi, 0, h)),
        ]
        args += list(prefix)
    scratch = []
    if n_maps == 2:
        in_specs += [_const_spec(extras[0].shape), _const_spec(extras[1].shape)]
        args += list(extras)
        scratch.append(pltpu.VMEM((2 * hb, tq, dq), BF16))
    scratch += [pltpu.VMEM((hb * n_maps, tq, LANES), F32), pltpu.VMEM((hb * n_maps, tq, 2 * LANES), F32)]
    return pl.pallas_call(
        functools.partial(_flash_kernel, n_maps=n_maps, has_prefix=prefix is not None, lam_init=lam_init,
                          hb=hb, dq=dq),
        grid=(b, heads // hb, s // tq, k.shape[1] // tk),
        in_specs=in_specs,
        out_specs=pl.BlockSpec((1, tq, hb * LANES), lambda bi, h, qi, kv: (bi, qi, h)),
        out_shape=jax.ShapeDtypeStruct((b, s, heads * LANES), BF16),
        scratch_shapes=scratch,
        compiler_params=_params("parallel", "parallel", "parallel", "arbitrary"),
        name="flash_diff" if n_maps == 2 else "flash_mla",
    )(*args)


def _out_proj_kernel(o_ref, w_ref, x_ref, mod_ref, out_ref):
    y = _dot(o_ref[0], w_ref[...])
    out_ref[0] = x_ref[0] + mod_ref[0, 2:3, :] * y


def _out_proj(o, w, x, mod, mod_row, tm):
    bx, s, d = x.shape
    x_spec, mod_spec = _token_specs(tm, d, mod_row)
    return pl.pallas_call(
        _out_proj_kernel,
        grid=(bx, s // tm),
        in_specs=[_out_spec(tm, o.shape[2]), _const_spec(w.shape), x_spec, mod_spec],
        out_specs=x_spec,
        out_shape=jax.ShapeDtypeStruct(x.shape, F32),
        compiler_params=_params("parallel", "parallel"),
        name="out_proj",
    )(o, w, x, mod)


def _gla_proj_kernel(x_ref, mod_ref, g_ref, w_ref, wr_ref, wg_ref, bg_ref, q_ref, k_ref, v_ref, og_ref, gate_ref):
    h = _norm_mod(x_ref[0], g_ref[...], mod_ref[0, 0:1, :], mod_ref[0, 1:2, :]).astype(BF16)
    col = 0
    for o_ref, width, scale in ((q_ref, GLA_KEY, GLA_DK ** -0.5), (k_ref, GLA_KEY, None),
                                (v_ref, GLA_VAL, None), (og_ref, GLA_VAL, None)):
        for c in range(width // 256):
            y = _dot(h, w_ref[:, col + c * 256: col + (c + 1) * 256])
            if scale is not None:
                y = y * scale
            o_ref[0, :, c * 256:(c + 1) * 256] = y.astype(BF16)
        col += width
    r_hi, r_lo = _split_bf16(_dot(h, wr_ref[...]))
    for c in range(2 * GLA_KEY // 256):
        wg = wg_ref[:, c * 256:(c + 1) * 256]
        z = _dot(r_hi, wg) + _dot(r_lo, wg) + bg_ref[:, c * 256:(c + 1) * 256]
        log_sig = jnp.minimum(z, 0.0) - jnp.log(1.0 + jnp.exp(-jnp.abs(z)))
        gate_ref[0, :, c * 256:(c + 1) * 256] = log_sig / GLA_GATE_NORM


def _gla_proj(x, mod, mod_row, norm_g, w, wr, wg, bg, tm):
    bx, s, d = x.shape
    x_spec, mod_spec = _token_specs(tm, d, mod_row)
    return pl.pallas_call(
        _gla_proj_kernel,
        grid=(bx, s // tm),
        in_specs=[x_spec, mod_spec, _const_spec((1, d)), _const_spec(w.shape), _const_spec(wr.shape),
                  _const_spec(wg.shape), _const_spec(bg.shape)],
        out_specs=[_out_spec(tm, GLA_KEY), _out_spec(tm, GLA_KEY), _out_spec(tm, GLA_VAL),
                   _out_spec(tm, GLA_VAL), _out_spec(tm, 2 * GLA_KEY)],
        out_shape=[jax.ShapeDtypeStruct((bx, s, GLA_KEY), BF16), jax.ShapeDtypeStruct((bx, s, GLA_KEY), BF16),
                   jax.ShapeDtypeStruct((bx, s, GLA_VAL), BF16), jax.ShapeDtypeStruct((bx, s, GLA_VAL), BF16),
                   jax.ShapeDtypeStruct((bx, s, 2 * GLA_KEY), F32)],
        compiler_params=_params("parallel", "parallel"),
        name="gla_proj",
    )(x, mod, norm_g, w, wr, wg, bg)


def _gla_scan_kernel(q_ref, k_ref, v_ref, g_ref, s0_ref, o_ref, sfin_ref, st_ref):
    direction = pl.program_id(1)
    i = pl.program_id(2)
    n_chunks = q_ref.shape[1] // GLA_CHUNK

    @pl.when(i == 0)
    def _load_state():
        st_ref[...] = s0_ref[0, 0]

    def run(backward):
        row = lax.broadcasted_iota(jnp.int32, (GLA_CHUNK, GLA_CHUNK), 0)
        colm = lax.broadcasted_iota(jnp.int32, (GLA_CHUNK, GLA_CHUNK), 1)
        allowed = (colm >= row) if backward else (colm <= row)
        cmat = jnp.where(allowed, 1.0, 0.0).astype(BF16)
        order = range(n_chunks - 1, -1, -1) if backward else range(n_chunks)
        for c in order:
            sl = slice(c * GLA_CHUNK, (c + 1) * GLA_CHUNK)
            for h in range(GLA_HEADS):
                ksl = slice(h * GLA_DK, (h + 1) * GLA_DK)
                vsl = slice(h * GLA_DV, (h + 1) * GLA_DV)
                q = q_ref[0, sl, ksl].astype(F32)
                k = k_ref[0, sl, ksl].astype(F32)
                v = v_ref[0, sl, vsl]
                g_hi, g_lo = _split_bf16(g_ref[0, sl, ksl])
                cum = _dot(cmat, g_hi) + _dot(cmat, g_lo)
                tot = cum[0:1] if backward else cum[GLA_CHUNK - 1:GLA_CHUNK]
                q_dec = (q * jnp.exp(cum)).astype(BF16)
                k_inv = (k * jnp.exp(-cum)).astype(BF16)
                k_end = (k * jnp.exp(tot - cum)).astype(BF16)
                a = jnp.where(allowed, _dot_nt(q_dec, k_inv), 0.0)
                st = st_ref[h]
                o_ref[0, 0, sl, vsl] = _dot(a.astype(BF16), v) + _dot_nt(q_dec, st.astype(BF16))
                st_ref[h] = st * jnp.exp(tot) + _dot_tn(v, k_end)

    pl.when(direction == 0)(lambda: run(False))
    pl.when(direction == 1)(lambda: run(True))

    @pl.when(i == pl.num_programs(2) - 1)
    def _store_state():
        sfin_ref[0, 0] = st_ref[...]


def _gla_scan(q, k, v, g, s0, tb):
    b, s, _ = q.shape
    nb = s // tb

    def blk(d, i):
        return i + d * (nb - 1 - 2 * i)

    st_spec = pl.BlockSpec((1, 1, GLA_HEADS, GLA_DV, GLA_DK), lambda bi, d, i: (bi, d, 0, 0, 0))
    return pl.pallas_call(
        _gla_scan_kernel,
        grid=(b, 2, nb),
        in_specs=[
            pl.BlockSpec((1, tb, GLA_KEY), lambda bi, d, i: (bi, blk(d, i), 0)),
            pl.BlockSpec((1, tb, GLA_KEY), lambda bi, d, i: (bi, blk(d, i), 0)),
            pl.BlockSpec((1, tb, GLA_VAL), lambda bi, d, i: (bi, blk(d, i), 0)),
            pl.BlockSpec((1, tb, GLA_KEY), lambda bi, d, i: (bi, blk(d, i), d)),
            st_spec,
        ],
        out_specs=[pl.BlockSpec((1, 1, tb, GLA_VAL), lambda bi, d, i: (d, bi, blk(d, i), 0)), st_spec],
        out_shape=[jax.ShapeDtypeStruct((2, b, s, GLA_VAL), F32),
                   jax.ShapeDtypeStruct((b, 2, GLA_HEADS, GLA_DV, GLA_DK), F32)],
        scratch_shapes=[pltpu.VMEM((GLA_HEADS, GLA_DV, GLA_DK), F32)],
        compiler_params=_params("parallel", "parallel", "arbitrary"),
        name="gla_scan",
    )(q, k, v, g, s0)


def _gla_out_kernel(of_ref, ob_ref, og_ref, ng_ref, w_ref, x_ref, mod_ref, out_ref):
    o = of_ref[0, 0] + ob_ref[0, 0]
    parts = []
    for h in range(GLA_HEADS):
        sl = slice(h * GLA_DV, (h + 1) * GLA_DV)
        parts.append((_rms(o[:, sl], ng_ref[...]) * _silu(og_ref[0, :, sl].astype(F32))).astype(BF16))
    y = _dot(jnp.concatenate(parts, axis=1), w_ref[...])
    out_ref[0] = x_ref[0] + mod_ref[0, 2:3, :] * y


def _gla_out(o2, og, norm_g, w, x, mod, mod_row, tm):
    bx, s, d = x.shape
    x_spec, mod_spec = _token_specs(tm, d, mod_row)
    o2 = o2.reshape(2, bx, s, GLA_VAL)
    return pl.pallas_call(
        _gla_out_kernel,
        grid=(bx, s // tm),
        in_specs=[pl.BlockSpec((1, 1, tm, GLA_VAL), lambda b, t: (0, b, t, 0)),
                  pl.BlockSpec((1, 1, tm, GLA_VAL), lambda b, t: (1, b, t, 0)),
                  _out_spec(tm, GLA_VAL), _const_spec(norm_g.shape), _const_spec(w.shape), x_spec, mod_spec],
        out_specs=x_spec,
        out_shape=jax.ShapeDtypeStruct(x.shape, F32),
        compiler_params=_params("parallel", "parallel"),
        name="gla_out",
    )(o2, o2, og, norm_g, w, x, mod)


def _mla_proj_kernel(*refs, rope):
    if rope:
        (x_ref, mod_ref, g_ref, w_ref, qg_ref, wq_ref, kvg_ref, wkn_ref, wv_ref,
         cos_ref, sa_ref, sb_ref, q_ref, k_ref, v_ref) = refs
        cos, sa, sb = cos_ref[...], sa_ref[...], sb_ref[...]
    else:
        x_ref, mod_ref, g_ref, w_ref, qg_ref, wq_ref, kvg_ref, wkn_ref, wv_ref, q_ref, k_ref, v_ref = refs
    h = _norm_mod(x_ref[0], g_ref[...], mod_ref[0, 0:1, :], mod_ref[0, 1:2, :]).astype(BF16)
    y = _dot(h, w_ref[...])
    c_q = _rms(y[:, :MLA_Q_LORA], qg_ref[...]).astype(BF16)
    c_kv = _rms(y[:, MLA_Q_LORA:MLA_Q_LORA + MLA_KV_LORA], kvg_ref[...]).astype(BF16)
    k_rope = y[:, MLA_Q_LORA + MLA_KV_LORA:]
    if rope:
        k_rope = _rope_chunk(k_rope, cos, sa, sb)
    k_rope = k_rope.astype(BF16)
    for hd in range(MLA_HEADS):
        base = hd * MLA_QK_PAD
        qh = _dot(c_q, wq_ref[:, base: base + MLA_QK_PAD])
        q_rope = qh[:, LANES:]
        if rope:
            q_rope = _rope_chunk(q_rope, cos, sa, sb)
        q_ref[0, :, base: base + LANES] = (qh[:, :LANES] * (MLA_SCALE * LOG2E)).astype(BF16)
        q_ref[0, :, base + LANES: base + MLA_QK_PAD] = (q_rope * (MLA_SCALE * LOG2E)).astype(BF16)
        k_ref[0, :, base + LANES: base + MLA_QK_PAD] = k_rope
    for c in range(MLA_HEADS * MLA_NOPE // 256):
        kn = _dot(c_kv, wkn_ref[:, c * 256:(c + 1) * 256]).astype(BF16)
        for s in range(2):
            hd = 2 * c + s
            k_ref[0, :, hd * MLA_QK_PAD: hd * MLA_QK_PAD + LANES] = kn[:, s * LANES:(s + 1) * LANES]
        v_ref[0, :, c * 256:(c + 1) * 256] = _dot(c_kv, wv_ref[:, c * 256:(c + 1) * 256]).astype(BF16)


def _mla_proj(x, mod, mod_row, norm_g, w, qg, wq, kvg, wkn, wv, tables, tm):
    bx, s, d = x.shape
    rope = tables is not None
    x_spec, mod_spec = _token_specs(tm, d, mod_row)
    consts = [norm_g, w, qg, wq, kvg, wkn, wv]
    in_specs = [x_spec, mod_spec] + [_const_spec(a.shape) for a in consts]
    args = [x, mod] + consts
    if rope:
        in_specs += _rope_specs(tm)
        args += list(tables)
    qk_w = MLA_HEADS * MLA_QK_PAD
    return pl.pallas_call(
        functools.partial(_mla_proj_kernel, rope=rope),
        grid=(bx, s // tm),
        in_specs=in_specs,
        out_specs=[_out_spec(tm, qk_w), _out_spec(tm, qk_w), _out_spec(tm, MLA_HEADS * MLA_V)],
        out_shape=[jax.ShapeDtypeStruct((bx, s, qk_w), BF16), jax.ShapeDtypeStruct((bx, s, qk_w), BF16),
                   jax.ShapeDtypeStruct((bx, s, MLA_HEADS * MLA_V), BF16)],
        compiler_params=_params("parallel", "parallel"),
        name="mla_proj",
    )(*args)


def _top2_sum(a, b, c, d):
    hi1, lo1 = jnp.maximum(a, b), jnp.minimum(a, b)
    hi2, lo2 = jnp.maximum(c, d), jnp.minimum(c, d)
    return jnp.maximum(hi1, hi2) + jnp.maximum(jnp.minimum(hi1, hi2), jnp.maximum(lo1, lo2))


def _route_rows(scores, sel):
    eg = EXPERTS_PER_GROUP
    gs = [_top2_sum(*sel[g * eg:(g + 1) * eg]) for g in range(N_GROUPS)]
    best = jnp.maximum(jnp.maximum(gs[0], gs[1]), jnp.maximum(gs[2], gs[3]))
    gidx = jnp.where(gs[0] >= best, 0, jnp.where(gs[1] >= best, 1, jnp.where(gs[2] >= best, 2, 3)))

    def pick(rows, j):
        out = rows[(N_GROUPS - 1) * eg + j]
        for g in range(N_GROUPS - 2, -1, -1):
            out = jnp.where(gidx == g, rows[g * eg + j], out)
        return out

    loc_sel = [pick(sel, j) for j in range(eg)]
    loc_sc = [pick(scores, j) for j in range(eg)]
    weights = []
    for j in range(eg):
        rank = jnp.zeros_like(loc_sel[j])
        for i in range(eg):
            if i == j:
                continue
            beats = (loc_sel[i] >= loc_sel[j]) if i < j else (loc_sel[i] > loc_sel[j])
            rank = rank + jnp.where(beats, 1.0, 0.0)
        weights.append(jnp.where(rank < 1.5, loc_sc[j], 0.0))
    den = weights[0] + weights[1] + weights[2] + weights[3]
    weights = [w / den for w in weights]
    return [jnp.where(gidx == e // eg, weights[e % eg], 0.0) for e in range(N_EXPERTS)]


def _moe_kernel(x_ref, mod_ref, g_ref, rwh_ref, rwl_ref, rb_ref, shu_ref, shd_ref, exp_ref, wup_ref, wdn_ref,
                fg_ref, out_ref, h_ref, comb_ref, acc_ref, *, final_norm):
    grp = pl.program_id(2)

    @pl.when(grp == 0)
    def _prepare():
        hf = _norm_mod(x_ref[0], g_ref[...], mod_ref[0, 3:4, :], mod_ref[0, 4:5, :])
        h_hi, h_lo = _split_bf16(hf)
        h_ref[...] = h_hi
        logits = _dot_nt(rwh_ref[...], h_hi) + _dot_nt(rwh_ref[...], h_lo) + _dot_nt(rwl_ref[...], h_hi)
        scores = jax.nn.sigmoid(logits)
        sel = scores + rb_ref[...]
        rows = _route_rows([scores[e:e + 1, :] for e in range(N_EXPERTS)],
                           [sel[e:e + 1, :] for e in range(N_EXPERTS)])
        comb_ref[...] = jnp.concatenate(rows, axis=0)
        a = _dot(h_hi, shu_ref[:, :SHARED_FF])
        u = _dot(h_hi, shu_ref[:, SHARED_FF:])
        acc_ref[...] = _dot((_silu(a) * u).astype(BF16), shd_ref[...])

    h = h_ref[...]
    c_hi, c_lo = _split_bf16(comb_ref[...])
    acts = []
    for e in range(EXPERTS_PER_GROUP):
        a = _dot(h, wup_ref[0, :, e * EXPERT_FF:(e + 1) * EXPERT_FF])
        u = _dot(h, wup_ref[0, :, (EXPERTS_PER_GROUP + e) * EXPERT_FF:(EXPERTS_PER_GROUP + e + 1) * EXPERT_FF])
        ex = exp_ref[0, :, e * EXPERT_FF:(e + 1) * EXPERT_FF]
        scale = _dot_tn(c_hi, ex) + _dot_tn(c_lo, ex)
        acts.append((_silu(a) * u * scale).astype(BF16))
    acc_ref[...] += _dot(jnp.concatenate(acts, axis=1), wdn_ref[0])

    @pl.when(grp == pl.num_programs(2) - 1)
    def _finish():
        out = x_ref[0] + mod_ref[0, 5:6, :] * acc_ref[...]
        if final_norm:
            out = _rms(out, fg_ref[...])
        out_ref[0] = out


def _moe(x, mod, mod_row, norm_g, rw_hi, rw_lo, rb, sh_up, sh_dn, expand, w_up, w_dn, final_g, final_norm, tm):
    bx, s, d = x.shape
    if mod_row is None:
        mod_map = lambda b, t, g: (b, 0, 0)
    else:
        mod_map = lambda b, t, g: (mod_row, 0, 0)
    x_spec = pl.BlockSpec((1, tm, d), lambda b, t, g: (b, t, 0))
    consts = [norm_g, rw_hi, rw_lo, rb, sh_up, sh_dn]
    in_specs = ([x_spec, pl.BlockSpec((1, N_MOD, d), mod_map)] + [_const_spec(a.shape) for a in consts] + [
        pl.BlockSpec((1,) + expand.shape[1:], lambda b, t, g: (g, 0, 0)),
        pl.BlockSpec((1,) + w_up.shape[1:], lambda b, t, g: (g, 0, 0)),
        pl.BlockSpec((1,) + w_dn.shape[1:], lambda b, t, g: (g, 0, 0)),
        _const_spec(final_g.shape)])
    return pl.pallas_call(
        functools.partial(_moe_kernel, final_norm=final_norm),
        grid=(bx, s // tm, N_GROUPS),
        in_specs=in_specs,
        out_specs=x_spec,
        out_shape=jax.ShapeDtypeStruct(x.shape, F32),
        scratch_shapes=[pltpu.VMEM((tm, d), BF16), pltpu.VMEM((N_EXPERTS, tm), F32), pltpu.VMEM((tm, d), F32)],
        compiler_params=_params("parallel", "parallel", "arbitrary"),
        name="moe",
    )(x, mod, *consts, expand, w_up, w_dn, final_g)


def _diff_weights(w_in):
    d = w_in.shape[0]

    def regroup(w):
        return w.reshape(d, 2, DIFF_HEADS, DIFF_HEAD_DIM).transpose(0, 2, 1, 3).reshape(d, DIFF_QK)

    return jnp.concatenate([regroup(w_in[:, :DIFF_QK]), regroup(w_in[:, DIFF_QK:2 * DIFF_QK]),
                            w_in[:, 2 * DIFF_QK:]], axis=1).astype(BF16)


def _gla_weights(w_in, gate_w, gate_b):
    d = w_in.shape[0]
    main = 2 * GLA_KEY + 2 * GLA_VAL
    w_r = jnp.zeros((d, LANES), F32).at[:, :2 * GLA_GATE_RANK].set(w_in[:, main:]).astype(BF16)
    w_g = jnp.zeros((LANES, 2 * GLA_KEY), F32)
    w_g = w_g.at[:GLA_GATE_RANK, :GLA_KEY].set(gate_w[0])
    w_g = w_g.at[GLA_GATE_RANK:2 * GLA_GATE_RANK, GLA_KEY:].set(gate_w[1]).astype(BF16)
    return w_in[:, :main].astype(BF16), w_r, w_g, gate_b.reshape(1, 2 * GLA_KEY)


def _mla_weights(w_in, w_uq, w_ukv):
    d = w_in.shape[0]
    w = jnp.zeros((d, MLA_Q_LORA + MLA_KV_LORA + LANES), F32).at[:, :w_in.shape[1]].set(w_in).astype(BF16)
    wq = w_uq.reshape(MLA_Q_LORA, MLA_HEADS, MLA_NOPE + MLA_ROPE)
    wq = jnp.pad(wq, ((0, 0), (0, 0), (0, MLA_QK_PAD - MLA_NOPE - MLA_ROPE)))
    wq = wq.reshape(MLA_Q_LORA, MLA_HEADS * MLA_QK_PAD).astype(BF16)
    wkv = w_ukv.reshape(MLA_KV_LORA, MLA_HEADS, MLA_NOPE + MLA_V)
    wkn = wkv[:, :, :MLA_NOPE].reshape(MLA_KV_LORA, MLA_HEADS * MLA_NOPE).astype(BF16)
    wv = wkv[:, :, MLA_NOPE:].reshape(MLA_KV_LORA, MLA_HEADS * MLA_V).astype(BF16)
    return w, wq, wkn, wv


def _moe_weights(w_up, w_down):
    e, d, _ = w_up.shape
    eg = EXPERTS_PER_GROUP
    wu = w_up.reshape(N_GROUPS, eg, d, 2, EXPERT_FF).transpose(0, 2, 3, 1, 4).reshape(N_GROUPS, d, 2 * eg * EXPERT_FF)
    return wu.astype(BF16), w_down.reshape(N_GROUPS, eg * EXPERT_FF, d).astype(BF16)


def _expand_matrix():
    e = jnp.arange(N_EXPERTS)[None, :, None]
    g = jnp.arange(N_GROUPS)[:, None, None]
    c = jnp.arange(EXPERTS_PER_GROUP * EXPERT_FF)[None, None, :]
    return (e == g * EXPERTS_PER_GROUP + c // EXPERT_FF).astype(BF16)


def kernel(x, c, ctx, c_ctx, ada_w, ada_b, norm_g, router_w, router_b, moe_w_up, moe_w_down, shared_w_up,
           shared_w_down, diff_w_in, diff_lam, diff_norm_g, diff_w_out, gla_w_in, gla_gate_w, gla_gate_b,
           gla_norm_g, gla_w_out, mla_w_in, mla_q_norm_g, mla_w_uq, mla_kv_norm_g, mla_w_ukv, mla_w_out,
           final_norm_g):
    b, s, d = x.shape
    cl = ctx.shape[1]
    assert b + 1 <= COND_ROWS and d == D_MODEL
    depth = ada_w.shape[0]
    ctx_row = b

    cond = jnp.zeros((COND_ROWS, d), F32).at[:b].set(c).at[ctx_row].set(c_ctx)
    mods = _ada_params(cond, ada_w, ada_b).reshape(depth, COND_ROWS, N_MOD, d)

    tables = _rope_tables(s)
    rw_hi, rw_lo = _split_bf16(router_w.T)
    rb = router_b.reshape(N_EXPERTS, 1)
    expand = _expand_matrix()
    final_g = final_norm_g.reshape(1, d)

    tm_lat = 512
    n_ctx = b * cl
    tm_ctx = 512 if n_ctx % 512 == 0 else cl
    x_lat = x
    x_ctx = ctx.reshape(1, n_ctx, d)

    for i in range(depth):
        need_ctx = i < depth - 1
        mod = mods[i]
        g1 = norm_g[i, 0].reshape(1, d)
        g2 = norm_g[i, 1].reshape(1, d)
        kind, j = i % N_MIXERS, i // N_MIXERS
        y_ctx = None
        if kind == 0:
            lam_init = 0.8 - 0.6 * math.exp(-0.3 * i)
            w = _diff_weights(diff_w_in[j])
            w_out = diff_w_out[j].astype(BF16)
            extras = (diff_lam[j], diff_norm_g[j].reshape(1, 2 * DIFF_HEAD_DIM))
            q_l, k_l, v_l = _diff_proj(x_lat, mod, None, g1, w, tables, tm_lat)
            q_c, k_c, v_c = (t.reshape(b, cl, -1) for t in _diff_proj(x_ctx, mod, ctx_row, g1, w, None, tm_ctx))
            o_lat = _flash(q_l, k_l, v_l, (k_c, v_c), extras, heads=DIFF_HEADS, dq=2 * DIFF_HEAD_DIM,
                           tq=512, tk=1024, lam_init=lam_init, hb=DIFF_HEADS)
            x_lat = _out_proj(o_lat, w_out, x_lat, mod, None, tm_lat)
            if need_ctx:
                o_ctx = _flash(q_c, k_c, v_c, None, extras, heads=DIFF_HEADS, dq=2 * DIFF_HEAD_DIM,
                               tq=cl, tk=cl, lam_init=lam_init, hb=DIFF_HEADS)
                x_ctx = _out_proj(o_ctx.reshape(1, n_ctx, -1), w_out, x_ctx, mod, ctx_row, tm_ctx)
        elif kind == 1:
            w, w_r, w_g, b_g = _gla_weights(gla_w_in[j], gla_gate_w[j], gla_gate_b[j])
            w_out = gla_w_out[j].astype(BF16)
            ng = gla_norm_g[j].reshape(1, GLA_DV)
            q_l, k_l, v_l, og_l, gt_l = _gla_proj(x_lat, mod, None, g1, w, w_r, w_g, b_g, tm_lat)
            q_c, k_c, v_c, og_c, gt_c = (t.reshape(b, cl, -1)
                                         for t in _gla_proj(x_ctx, mod, ctx_row, g1, w, w_r, w_g, b_g, tm_ctx))
            s0 = jnp.zeros((b, 2, GLA_HEADS, GLA_DV, GLA_DK), F32)
            o_c, s_c = _gla_scan(q_c, k_c, v_c, gt_c, s0, cl)
            o_l, _ = _gla_scan(q_l, k_l, v_l, gt_l, s_c, 256)
            x_lat = _gla_out(o_l, og_l, ng, w_out, x_lat, mod, None, tm_lat)
            if need_ctx:
                x_ctx = _gla_out(o_c.reshape(2, 1, n_ctx, -1), og_c.reshape(1, n_ctx, -1), ng, w_out, x_ctx, mod,
                                 ctx_row, tm_ctx)
        else:
            w, wq, wkn, wv = _mla_weights(mla_w_in[j], mla_w_uq[j], mla_w_ukv[j])
            w_out = mla_w_out[j].astype(BF16)
            qg = mla_q_norm_g[j].reshape(1, MLA_Q_LORA)
            kvg = mla_kv_norm_g[j].reshape(1, MLA_KV_LORA)
            q_l, k_l, v_l = _mla_proj(x_lat, mod, None, g1, w, qg, wq, kvg, wkn, wv, tables, tm_lat)
            q_c, k_c, v_c = (t.reshape(b, cl, -1)
                             for t in _mla_proj(x_ctx, mod, ctx_row, g1, w, qg, wq, kvg, wkn, wv, None, tm_ctx))
            o_lat = _flash(q_l, k_l, v_l, (k_c, v_c), None, heads=MLA_HEADS, dq=MLA_QK_PAD, tq=512, tk=1024,
                           hb=MLA_HEADS)
            x_lat = _out_proj(o_lat, w_out, x_lat, mod, None, tm_lat)
            if need_ctx:
                o_ctx = _flash(q_c, k_c, v_c, None, None, heads=MLA_HEADS, dq=MLA_QK_PAD, tq=cl, tk=cl,
                               hb=MLA_HEADS)
                x_ctx = _out_proj(o_ctx.reshape(1, n_ctx, -1), w_out, x_ctx, mod, ctx_row, tm_ctx)

        w_up, w_dn = _moe_weights(moe_w_up[i], moe_w_down[i])
        sh_up = shared_w_up[i].astype(BF16)
        sh_dn = shared_w_down[i].astype(BF16)
        last = i == depth - 1
        x_lat = _moe(x_lat, mod, None, g2, rw_hi, rw_lo, rb, sh_up, sh_dn, expand, w_up, w_dn, final_g, last, tm_lat)
        if need_ctx:
            x_ctx = _moe(x_ctx, mod, ctx_row, g2, rw_hi, rw_lo, rb, sh_up, sh_dn, expand, w_up, w_dn, final_g,
                         False, tm_ctx)
    return x_lat
```

```python
import functools
import math

import jax
import jax.numpy as jnp
from jax import lax
from jax.experimental import pallas as pl
from jax.experimental.pallas import tpu as pltpu

F32 = jnp.float32
BF16 = jnp.bfloat16

D_MODEL = 1024
DEPTH = 4
GRID_W = 64
N_MIXERS = 3
NORM_EPS = 1e-6
ROPE_BASE = 10000.0
N_MOD = 6

DIFF_HEADS = 8
DIFF_HEAD_DIM = 64
DIFF_QK = 2 * DIFF_HEADS * DIFF_HEAD_DIM
DIFF_V = 2 * DIFF_HEADS * DIFF_HEAD_DIM

GLA_HEADS = 4
GLA_KEY = D_MODEL // 2
GLA_VAL = D_MODEL
GLA_DK = GLA_KEY // GLA_HEADS
GLA_DV = GLA_VAL // GLA_HEADS
GLA_GATE_RANK = 16
GLA_GATE_NORM = 16.0
GLA_CHUNK = 64

MLA_HEADS = 8
MLA_Q_LORA = 384
MLA_KV_LORA = 256
MLA_NOPE = 128
MLA_ROPE = 64
MLA_V = 128
MLA_SCALE = (MLA_NOPE + MLA_ROPE) ** -0.5
MLA_QK_PAD = 256

N_EXPERTS = 16
N_GROUPS = 4
EXPERTS_PER_GROUP = N_EXPERTS // N_GROUPS
EXPERT_FF = 256
SHARED_FF = 256

LANES = 128
COND_ROWS = 16
NEG_BIG = -1e30
LOG2E = math.log2(math.e)
VMEM_LIMIT = 56 * 1024 * 1024

NT_DIMS = (((1,), (1,)), ((), ()))
TN_DIMS = (((0,), (0,)), ((), ()))


def _dot(a, b):
    return jnp.dot(a, b, preferred_element_type=F32)


def _dot_nt(a, b):
    return lax.dot_general(a, b, NT_DIMS, preferred_element_type=F32)


def _dot_tn(a, b):
    return lax.dot_general(a, b, TN_DIMS, preferred_element_type=F32)


def _split_bf16(a):
    hi = a.astype(BF16)
    lo = (a - hi.astype(F32)).astype(BF16)
    return hi, lo


def _silu(a):
    return a * jax.nn.sigmoid(a)


def _params(*sem):
    return pltpu.CompilerParams(dimension_semantics=sem, vmem_limit_bytes=VMEM_LIMIT)


def _const_spec(shape):
    nd = len(shape)
    return pl.BlockSpec(shape, lambda *_: (0,) * nd)


def _ada_kernel(cond_ref, w_ref, b_ref, o_ref):
    a_hi, a_lo = _split_bf16(_silu(cond_ref[...]))
    w_hi, w_lo = _split_bf16(w_ref[0])
    o_ref[0] = _dot(a_hi, w_hi) + _dot(a_lo, w_hi) + _dot(a_hi, w_lo) + b_ref[0]


def _ada_params(cond, ada_w, ada_b):
    depth, d, n = ada_w.shape
    tn = 1024
    return pl.pallas_call(
        _ada_kernel,
        grid=(depth, n // tn),
        in_specs=[
            pl.BlockSpec((COND_ROWS, d), lambda i, j: (0, 0)),
            pl.BlockSpec((1, d, tn), lambda i, j: (i, 0, j)),
            pl.BlockSpec((1, 1, tn), lambda i, j: (i, 0, j)),
        ],
        out_specs=pl.BlockSpec((1, COND_ROWS, tn), lambda i, j: (i, 0, j)),
        out_shape=jax.ShapeDtypeStruct((depth, COND_ROWS, n), F32),
        compiler_params=_params("parallel", "parallel"),
        name="ada_params",
    )(cond, ada_w, ada_b.reshape(depth, 1, n))


def _norm_mod(x, g, shift, scale):
    ms = jnp.mean(x * x, axis=-1, keepdims=True)
    return (x * lax.rsqrt(ms + NORM_EPS) * g) * (1.0 + scale) + shift


def _rms(x, g):
    ms = jnp.mean(x * x, axis=-1, keepdims=True)
    return x * lax.rsqrt(ms + NORM_EPS) * g


def _rope_chunk(c, cos, sa, sb):
    return c * cos + pltpu.roll(c, LANES - 16, 1) * sa + pltpu.roll(c, 16, 1) * sb


def _rope_tables(seq):
    t = jnp.arange(seq)
    pos_row = (t // GRID_W).astype(F32)
    pos_col = (t % GRID_W).astype(F32)
    inv = ROPE_BASE ** (-jnp.arange(0, 32, 2, dtype=F32) / 32)
    lane = jnp.arange(LANES)
    d = lane % 64
    r = d % 32
    first = (r < 16)[None, :]
    pos = jnp.where((d // 32)[None, :] == 0, pos_row[:, None], pos_col[:, None])
    ang = pos * inv[r % 16][None, :]
    cos, sin = jnp.cos(ang), jnp.sin(ang)
    return cos, jnp.where(first, -sin, 0.0), jnp.where(first, 0.0, sin)


def _token_specs(tm, d, mod_row):
    x_spec = pl.BlockSpec((1, tm, d), lambda b, t: (b, t, 0))
    if mod_row is None:
        mod_spec = pl.BlockSpec((1, N_MOD, d), lambda b, t: (b, 0, 0))
    else:
        mod_spec = pl.BlockSpec((1, N_MOD, d), lambda b, t: (mod_row, 0, 0))
    return x_spec, mod_spec


def _rope_specs(tm):
    return [pl.BlockSpec((tm, LANES), lambda b, t: (t, 0))] * 3


def _out_spec(tm, n):
    return pl.BlockSpec((1, tm, n), lambda b, t: (b, t, 0))


def _diff_proj_kernel(*refs, rope):
    if rope:
        x_ref, mod_ref, g_ref, w_ref, cos_ref, sa_ref, sb_ref, q_ref, k_ref, v_ref = refs
        cos, sa, sb = cos_ref[...], sa_ref[...], sb_ref[...]
    else:
        x_ref, mod_ref, g_ref, w_ref, q_ref, k_ref, v_ref = refs
    h = _norm_mod(x_ref[0], g_ref[...], mod_ref[0, 0:1, :], mod_ref[0, 1:2, :]).astype(BF16)
    for idx, o_ref in enumerate((q_ref, k_ref)):
        for c in range(DIFF_QK // 256):
            y = _dot(h, w_ref[:, idx * DIFF_QK + c * 256: idx * DIFF_QK + (c + 1) * 256])
            for s in range(2):
                ys = y[:, s * LANES:(s + 1) * LANES]
                if rope:
                    ys = _rope_chunk(ys, cos, sa, sb)
                if idx == 0:
                    ys = ys * (DIFF_HEAD_DIM ** -0.5 * LOG2E)
                o_ref[0, :, c * 256 + s * LANES: c * 256 + (s + 1) * LANES] = ys.astype(BF16)
    for c in range(DIFF_V // 256):
        y = _dot(h, w_ref[:, 2 * DIFF_QK + c * 256: 2 * DIFF_QK + (c + 1) * 256])
        v_ref[0, :, c * 256:(c + 1) * 256] = y.astype(BF16)


def _diff_proj(x, mod, mod_row, norm_g, w, tables, tm):
    bx, s, d = x.shape
    rope = tables is not None
    x_spec, mod_spec = _token_specs(tm, d, mod_row)
    in_specs = [x_spec, mod_spec, _const_spec((1, d)), _const_spec(w.shape)]
    args = [x, mod, norm_g, w]
    if rope:
        in_specs += _rope_specs(tm)
        args += list(tables)
    out = jax.ShapeDtypeStruct((bx, s, DIFF_QK), BF16)
    return pl.pallas_call(
        functools.partial(_diff_proj_kernel, rope=rope),
        grid=(bx, s // tm),
        in_specs=in_specs,
        out_specs=[_out_spec(tm, DIFF_QK)] * 3,
        out_shape=[out, out, out],
        compiler_params=_params("parallel", "parallel"),
        name="diff_proj",
    )(*args)


def _flash_kernel(*refs, n_maps, has_prefix, lam_init, hb, dq):
    refs = list(refs)
    q_ref, k_ref, v_ref = refs[:3]
    pos = 3
    if has_prefix:
        kc_ref, vc_ref = refs[pos:pos + 2]
        pos += 2
    if n_maps == 2:
        lam_ref, ng_ref = refs[pos:pos + 2]
        pos += 2
    o_ref = refs[pos]
    scratch = refs[pos + 1:]
    if n_maps == 2:
        qm_ref, m_ref, acc_ref = scratch
    else:
        m_ref, acc_ref = scratch
    kv = pl.program_id(3)

    def step(hd, mi, k_src, v_src):
        ci = hd * n_maps + mi
        q = qm_ref[ci] if n_maps == 2 else q_ref[0, :, hd * dq:(hd + 1) * dq]
        k = k_src[0, :, hd * dq:(hd + 1) * dq]
        v = v_src[0, :, hd * LANES:(hd + 1) * LANES]
        s = _dot_nt(q, k)
        m_prev = m_ref[ci]
        m_new = jnp.maximum(m_prev, jnp.max(s, axis=1, keepdims=True))
        alpha = jnp.exp2(m_prev - m_new)
        p = jnp.exp2(s - jnp.tile(m_new, (1, s.shape[1] // LANES)))
        v1 = jnp.concatenate([v, jnp.ones_like(v)], axis=1)
        acc_ref[ci] = acc_ref[ci] * jnp.tile(alpha, (1, 2)) + _dot(p.astype(BF16), v1)
        m_ref[ci] = m_new

    def all_steps(k_src, v_src):
        for hd in range(hb):
            for mi in range(n_maps):
                step(hd, mi, k_src, v_src)

    @pl.when(kv == 0)
    def _init():
        m_ref[...] = jnp.full(m_ref.shape, NEG_BIG, F32)
        acc_ref[...] = jnp.zeros(acc_ref.shape, F32)
        if n_maps == 2:
            for hd in range(hb):
                q = q_ref[0, :, hd * dq:(hd + 1) * dq]
                lane = lax.broadcasted_iota(jnp.int32, q.shape, 1)
                qm_ref[2 * hd] = jnp.where(lane < DIFF_HEAD_DIM, q, jnp.zeros_like(q))
                qm_ref[2 * hd + 1] = jnp.where(lane >= DIFF_HEAD_DIM, q, jnp.zeros_like(q))
        if has_prefix:
            all_steps(kc_ref, vc_ref)

    all_steps(k_ref, v_ref)

    @pl.when(kv == pl.num_programs(3) - 1)
    def _finish():
        if n_maps == 2:
            lf = lam_ref[...]
            l1 = jnp.sum(lf[0:1] * lf[1:2], axis=1, keepdims=True)
            l2 = jnp.sum(lf[2:3] * lf[3:4], axis=1, keepdims=True)
            lam = jnp.exp(l1) - jnp.exp(l2) + lam_init
        for hd in range(hb):
            a0 = acc_ref[hd * n_maps]
            o = a0[:, :LANES] / a0[:, LANES:]
            if n_maps == 2:
                a1 = acc_ref[hd * n_maps + 1]
                o = o - lam * (a1[:, :LANES] / a1[:, LANES:])
                o = _rms(o, ng_ref[...]) * (1.0 - lam_init)
            o_ref[0, :, hd * LANES:(hd + 1) * LANES] = o.astype(o_ref.dtype)


def _flash(q, k, v, prefix, extras, *, heads, dq, tq, tk, lam_init=0.0, hb=1):
    b, s, _ = q.shape
    n_maps = 2 if extras is not None else 1
    in_specs = [
        pl.BlockSpec((1, tq, hb * dq), lambda bi, h, qi, kv: (bi, qi, h)),
        pl.BlockSpec((1, tk, hb * dq), lambda bi, h, qi, kv: (bi, kv, h)),
        pl.BlockSpec((1, tk, hb * LANES), lambda bi, h, qi, kv: (bi, kv, h)),
    ]
    args = [q, k, v]
    if prefix is not None:
        cl = prefix[0].shape[1]
        in_specs += [
            pl.BlockSpec((1, cl, hb * dq), lambda bi, h, qi, kv: (bi, 0, h)),
            pl.BlockSpec((1, cl, hb * LANES), lambda bi, h, qi, kv: (bi, 0, h)),
        ]
        args += list(prefix)
    scratch = []
    if n_maps == 2:
        in_specs += [_const_spec(extras[0].shape), _const_spec(extras[1].shape)]
        args += list(extras)
        scratch.append(pltpu.VMEM((2 * hb, tq, dq), BF16))
    scratch += [pltpu.VMEM((hb * n_maps, tq, LANES), F32), pltpu.VMEM((hb * n_maps, tq, 2 * LANES), F32)]
    return pl.pallas_call(
        functools.partial(_flash_kernel, n_maps=n_maps, has_prefix=prefix is not None, lam_init=lam_init,
                          hb=hb, dq=dq),
        grid=(b, heads // hb, s // tq, k.shape[1] // tk),
        in_specs=in_specs,
        out_specs=pl.BlockSpec((1, tq, hb * LANES), lambda bi, h, qi, kv: (bi, qi, h)),
        out_shape=jax.ShapeDtypeStruct((b, s, heads * LANES), BF16),
        scratch_shapes=scratch,
        compiler_params=_params("parallel", "parallel", "parallel", "arbitrary"),
        name="flash_diff" if n_maps == 2 else "flash_mla",
    )(*args)


def _gla_proj_kernel(x_ref, mod_ref, g_ref, w_ref, wr_ref, wg_ref, bg_ref, q_ref, k_ref, v_ref, og_ref, gate_ref):
    h = _norm_mod(x_ref[0], g_ref[...], mod_ref[0, 0:1, :], mod_ref[0, 1:2, :]).astype(BF16)
    col = 0
    for o_ref, width, scale in ((q_ref, GLA_KEY, GLA_DK ** -0.5), (k_ref, GLA_KEY, None),
                                (v_ref, GLA_VAL, None), (og_ref, GLA_VAL, None)):
        for c in range(width // 256):
            y = _dot(h, w_ref[:, col + c * 256: col + (c + 1) * 256])
            if scale is not None:
                y = y * scale
            o_ref[0, :, c * 256:(c + 1) * 256] = y.astype(BF16)
        col += width
    r_hi, r_lo = _split_bf16(_dot(h, wr_ref[...]))
    for c in range(2 * GLA_KEY // 256):
        wg = wg_ref[:, c * 256:(c + 1) * 256]
        z = _dot(r_hi, wg) + _dot(r_lo, wg) + bg_ref[:, c * 256:(c + 1) * 256]
        log_sig = jnp.minimum(z, 0.0) - jnp.log(1.0 + jnp.exp(-jnp.abs(z)))
        gate_ref[0, :, c * 256:(c + 1) * 256] = log_sig / GLA_GATE_NORM


def _gla_proj(x, mod, mod_row, norm_g, w, wr, wg, bg, tm):
    bx, s, d = x.shape
    x_spec, mod_spec = _token_specs(tm, d, mod_row)
    return pl.pallas_call(
        _gla_proj_kernel,
        grid=(bx, s // tm),
        in_specs=[x_spec, mod_spec, _const_spec((1, d)), _const_spec(w.shape), _const_spec(wr.shape),
                  _const_spec(wg.shape), _const_spec(bg.shape)],
        out_specs=[_out_spec(tm, GLA_KEY), _out_spec(tm, GLA_KEY), _out_spec(tm, GLA_VAL),
                   _out_spec(tm, GLA_VAL), _out_spec(tm, 2 * GLA_KEY)],
        out_shape=[jax.ShapeDtypeStruct((bx, s, GLA_KEY), BF16), jax.ShapeDtypeStruct((bx, s, GLA_KEY), BF16),
                   jax.ShapeDtypeStruct((bx, s, GLA_VAL), BF16), jax.ShapeDtypeStruct((bx, s, GLA_VAL), BF16),
                   jax.ShapeDtypeStruct((bx, s, 2 * GLA_KEY), F32)],
        compiler_params=_params("parallel", "parallel"),
        name="gla_proj",
    )(x, mod, norm_g, w, wr, wg, bg)


def _gla_scan_kernel(q_ref, k_ref, v_ref, g_ref, s0_ref, o_ref, sfin_ref, st_ref):
    direction = pl.program_id(1)
    i = pl.program_id(2)
    n_chunks = q_ref.shape[1] // GLA_CHUNK

    @pl.when(i == 0)
    def _load_state():
        st_ref[...] = s0_ref[0, 0]

    def run(backward):
        row = lax.broadcasted_iota(jnp.int32, (GLA_CHUNK, GLA_CHUNK), 0)
        colm = lax.broadcasted_iota(jnp.int32, (GLA_CHUNK, GLA_CHUNK), 1)
        allowed = (colm >= row) if backward else (colm <= row)
        cmat = jnp.where(allowed, 1.0, 0.0).astype(BF16)
        order = range(n_chunks - 1, -1, -1) if backward else range(n_chunks)
        for c in order:
            sl = slice(c * GLA_CHUNK, (c + 1) * GLA_CHUNK)
            for h in range(GLA_HEADS):
                ksl = slice(h * GLA_DK, (h + 1) * GLA_DK)
                vsl = slice(h * GLA_DV, (h + 1) * GLA_DV)
                q = q_ref[0, sl, ksl].astype(F32)
                k = k_ref[0, sl, ksl].astype(F32)
                v = v_ref[0, sl, vsl]
                g_hi, g_lo = _split_bf16(g_ref[0, sl, ksl])
                cum = _dot(cmat, g_hi) + _dot(cmat, g_lo)
                tot = cum[0:1] if backward else cum[GLA_CHUNK - 1:GLA_CHUNK]
                q_dec = (q * jnp.exp(cum)).astype(BF16)
                k_inv = (k * jnp.exp(-cum)).astype(BF16)
                k_end = (k * jnp.exp(tot - cum)).astype(BF16)
                a = jnp.where(allowed, _dot_nt(q_dec, k_inv), 0.0)
                st = st_ref[h]
                o_ref[0, 0, sl, vsl] = _dot(a.astype(BF16), v) + _dot_nt(q_dec, st.astype(BF16))
                st_ref[h] = st * jnp.exp(tot) + _dot_tn(v, k_end)

    pl.when(direction == 0)(lambda: run(False))
    pl.when(direction == 1)(lambda: run(True))

    @pl.when(i == pl.num_programs(2) - 1)
    def _store_state():
        sfin_ref[0, 0] = st_ref[...]


def _gla_scan(q, k, v, g, s0, tb):
    b, s, _ = q.shape
    nb = s // tb

    def blk(d, i):
        return i + d * (nb - 1 - 2 * i)

    st_spec = pl.BlockSpec((1, 1, GLA_HEADS, GLA_DV, GLA_DK), lambda bi, d, i: (bi, d, 0, 0, 0))
    return pl.pallas_call(
        _gla_scan_kernel,
        grid=(b, 2, nb),
        in_specs=[
            pl.BlockSpec((1, tb, GLA_KEY), lambda bi, d, i: (bi, blk(d, i), 0)),
            pl.BlockSpec((1, tb, GLA_KEY), lambda bi, d, i: (bi, blk(d, i), 0)),
            pl.BlockSpec((1, tb, GLA_VAL), lambda bi, d, i: (bi, blk(d, i), 0)),
            pl.BlockSpec((1, tb, GLA_KEY), lambda bi, d, i: (bi, blk(d, i), d)),
            st_spec,
        ],
        out_specs=[pl.BlockSpec((1, 1, tb, GLA_VAL), lambda bi, d, i: (d, bi, blk(d, i), 0)), st_spec],
        out_shape=[jax.ShapeDtypeStruct((2, b, s, GLA_VAL), F32),
                   jax.ShapeDtypeStruct((b, 2, GLA_HEADS, GLA_DV, GLA_DK), F32)],
        scratch_shapes=[pltpu.VMEM((GLA_HEADS, GLA_DV, GLA_DK), F32)],
        compiler_params=_params("parallel", "parallel", "arbitrary"),
        name="gla_scan",
    )(q, k, v, g, s0)


def _mla_proj_kernel(*refs, rope):
    if rope:
        (x_ref, mod_ref, g_ref, w_ref, qg_ref, wq_ref, kvg_ref, wkn_ref, wv_ref,
         cos_ref, sa_ref, sb_ref, q_ref, k_ref, v_ref) = refs
        cos, sa, sb = cos_ref[...], sa_ref[...], sb_ref[...]
    else:
        x_ref, mod_ref, g_ref, w_ref, qg_ref, wq_ref, kvg_ref, wkn_ref, wv_ref, q_ref, k_ref, v_ref = refs
    h = _norm_mod(x_ref[0], g_ref[...], mod_ref[0, 0:1, :], mod_ref[0, 1:2, :]).astype(BF16)
    y = _dot(h, w_ref[...])
    c_q = _rms(y[:, :MLA_Q_LORA], qg_ref[...]).astype(BF16)
    c_kv = _rms(y[:, MLA_Q_LORA:MLA_Q_LORA + MLA_KV_LORA], kvg_ref[...]).astype(BF16)
    k_rope = y[:, MLA_Q_LORA + MLA_KV_LORA:]
    if rope:
        k_rope = _rope_chunk(k_rope, cos, sa, sb)
    k_rope = k_rope.astype(BF16)
    for hd in range(MLA_HEADS):
        base = hd * MLA_QK_PAD
        qh = _dot(c_q, wq_ref[:, base: base + MLA_QK_PAD])
        q_rope = qh[:, LANES:]
        if rope:
            q_rope = _rope_chunk(q_rope, cos, sa, sb)
        q_ref[0, :, base: base + LANES] = (qh[:, :LANES] * (MLA_SCALE * LOG2E)).astype(BF16)
        q_ref[0, :, base + LANES: base + MLA_QK_PAD] = (q_rope * (MLA_SCALE * LOG2E)).astype(BF16)
        k_ref[0, :, base + LANES: base + MLA_QK_PAD] = k_rope
    for c in range(MLA_HEADS * MLA_NOPE // 256):
        kn = _dot(c_kv, wkn_ref[:, c * 256:(c + 1) * 256]).astype(BF16)
        for s in range(2):
            hd = 2 * c + s
            k_ref[0, :, hd * MLA_QK_PAD: hd * MLA_QK_PAD + LANES] = kn[:, s * LANES:(s + 1) * LANES]
        v_ref[0, :, c * 256:(c + 1) * 256] = _dot(c_kv, wv_ref[:, c * 256:(c + 1) * 256]).astype(BF16)


def _mla_proj(x, mod, mod_row, norm_g, w, qg, wq, kvg, wkn, wv, tables, tm):
    bx, s, d = x.shape
    rope = tables is not None
    x_spec, mod_spec = _token_specs(tm, d, mod_row)
    consts = [norm_g, w, qg, wq, kvg, wkn, wv]
    in_specs = [x_spec, mod_spec] + [_const_spec(a.shape) for a in consts]
    args = [x, mod] + consts
    if rope:
        in_specs += _rope_specs(tm)
        args += list(tables)
    qk_w = MLA_HEADS * MLA_QK_PAD
    return pl.pallas_call(
        functools.partial(_mla_proj_kernel, rope=rope),
        grid=(bx, s // tm),
        in_specs=in_specs,
        out_specs=[_out_spec(tm, qk_w), _out_spec(tm, qk_w), _out_spec(tm, MLA_HEADS * MLA_V)],
        out_shape=[jax.ShapeDtypeStruct((bx, s, qk_w), BF16), jax.ShapeDtypeStruct((bx, s, qk_w), BF16),
                   jax.ShapeDtypeStruct((bx, s, MLA_HEADS * MLA_V), BF16)],
        compiler_params=_params("parallel", "parallel"),
        name="mla_proj",
    )(*args)


def _top2_sum(a, b, c, d):
    hi1, lo1 = jnp.maximum(a, b), jnp.minimum(a, b)
    hi2, lo2 = jnp.maximum(c, d), jnp.minimum(c, d)
    return jnp.maximum(hi1, hi2) + jnp.maximum(jnp.minimum(hi1, hi2), jnp.maximum(lo1, lo2))


def _route_rows(scores, sel):
    eg = EXPERTS_PER_GROUP
    gs = [_top2_sum(*sel[g * eg:(g + 1) * eg]) for g in range(N_GROUPS)]
    best = jnp.maximum(jnp.maximum(gs[0], gs[1]), jnp.maximum(gs[2], gs[3]))
    gidx = jnp.where(gs[0] >= best, 0, jnp.where(gs[1] >= best, 1, jnp.where(gs[2] >= best, 2, 3)))

    def pick(rows, j):
        out = rows[(N_GROUPS - 1) * eg + j]
        for g in range(N_GROUPS - 2, -1, -1):
            out = jnp.where(gidx == g, rows[g * eg + j], out)
        return out

    loc_sel = [pick(sel, j) for j in range(eg)]
    loc_sc = [pick(scores, j) for j in range(eg)]
    weights = []
    for j in range(eg):
        rank = jnp.zeros_like(loc_sel[j])
        for i in range(eg):
            if i == j:
                continue
            beats = (loc_sel[i] >= loc_sel[j]) if i < j else (loc_sel[i] > loc_sel[j])
            rank = rank + jnp.where(beats, 1.0, 0.0)
        weights.append(jnp.where(rank < 1.5, loc_sc[j], 0.0))
    den = weights[0] + weights[1] + weights[2] + weights[3]
    weights = [w / den for w in weights]
    return [jnp.where(gidx == e // eg, weights[e % eg], 0.0) for e in range(N_EXPERTS)]


def _mixer_output(pre, refs):
    if pre == "gla":
        of_ref, ob_ref, og_ref, ng_ref, w_ref = refs
        o = of_ref[0, 0] + ob_ref[0, 0]
        parts = []
        for h in range(GLA_HEADS):
            sl = slice(h * GLA_DV, (h + 1) * GLA_DV)
            parts.append((_rms(o[:, sl], ng_ref[...]) * _silu(og_ref[0, :, sl].astype(F32))).astype(BF16))
        return _dot(jnp.concatenate(parts, axis=1), w_ref[...])
    o_ref, w_ref = refs
    return _dot(o_ref[0], w_ref[...])


def _moe_kernel(*refs, pre, final_norm):
    n_pre = 5 if pre == "gla" else 2
    (x_ref, mod_ref, g_ref, rwh_ref, rwl_ref, rb_ref, shu_ref, shd_ref, wup_ref, wdn_ref, fg_ref,
     out_ref) = refs[n_pre:]
    x = x_ref[0] + mod_ref[0, 2:3, :] * _mixer_output(pre, refs[:n_pre])
    tm = x.shape[0]
    h, h_lo = _split_bf16(_norm_mod(x, g_ref[...], mod_ref[0, 3:4, :], mod_ref[0, 4:5, :]))
    logits = _dot_nt(rwh_ref[...], h) + _dot_nt(rwh_ref[...], h_lo) + _dot_nt(rwl_ref[...], h)
    scores = jax.nn.sigmoid(logits)
    sel = scores + rb_ref[...]
    rows = _route_rows([scores[e:e + 1, :] for e in range(N_EXPERTS)],
                       [sel[e:e + 1, :] for e in range(N_EXPERTS)])
    comb = jnp.concatenate(rows + [jnp.zeros((LANES - N_EXPERTS, tm), F32)], axis=0).T
    a = _dot(h, shu_ref[:, :SHARED_FF])
    u = _dot(h, shu_ref[:, SHARED_FF:])
    acc = _dot((_silu(a) * u).astype(BF16), shd_ref[...])
    for grp in range(N_GROUPS):
        acts = []
        for j in range(EXPERTS_PER_GROUP):
            e = grp * EXPERTS_PER_GROUP + j
            a = _dot(h, wup_ref[0, e, :, :EXPERT_FF])
            u = _dot(h, wup_ref[0, e, :, EXPERT_FF:])
            scale = jnp.broadcast_to(comb[:, e:e + 1], a.shape)
            acts.append((_silu(a) * u * scale).astype(BF16))
        w_dn = wdn_ref[0, grp * EXPERTS_PER_GROUP:(grp + 1) * EXPERTS_PER_GROUP]
        acc = acc + _dot(jnp.concatenate(acts, axis=1), w_dn.reshape(EXPERTS_PER_GROUP * EXPERT_FF, -1))
    out = x + mod_ref[0, 5:6, :] * acc
    if final_norm:
        out = _rms(out, fg_ref[...])
    out_ref[0] = out


def _moe(mixer, x, mod, mod_row, norm_g, rw_hi, rw_lo, rb, sh_up, sh_dn, w_up, w_dn, layer, final_g, final_norm, tm):
    bx, s, d = x.shape
    x_spec, mod_spec = _token_specs(tm, d, mod_row)
    pre = mixer[0]
    if pre == "gla":
        o2, og, ng, w_out = mixer[1:]
        o2 = o2.reshape(2, bx, s, GLA_VAL)
        pre_specs = [pl.BlockSpec((1, 1, tm, GLA_VAL), lambda b, t: (0, b, t, 0)),
                     pl.BlockSpec((1, 1, tm, GLA_VAL), lambda b, t: (1, b, t, 0)),
                     _out_spec(tm, GLA_VAL), _const_spec(ng.shape), _const_spec(w_out.shape)]
        pre_args = [o2, o2, og.reshape(bx, s, GLA_VAL), ng, w_out]
    else:
        o, w_out = mixer[1:]
        o = o.reshape(bx, s, -1)
        pre_specs = [_out_spec(tm, o.shape[2]), _const_spec(w_out.shape)]
        pre_args = [o, w_out]
    consts = [norm_g, rw_hi, rw_lo, rb, sh_up, sh_dn]
    resident = pl.Buffered(1)
    in_specs = (pre_specs + [x_spec, mod_spec] + [_const_spec(a.shape) for a in consts] + [
        pl.BlockSpec((1,) + w_up.shape[1:], lambda b, t: (layer, 0, 0, 0), pipeline_mode=resident),
        pl.BlockSpec((1,) + w_dn.shape[1:], lambda b, t: (layer, 0, 0, 0), pipeline_mode=resident),
        _const_spec(final_g.shape)])
    return pl.pallas_call(
        functools.partial(_moe_kernel, pre=pre, final_norm=final_norm),
        grid=(bx, s // tm),
        in_specs=in_specs,
        out_specs=x_spec,
        out_shape=jax.ShapeDtypeStruct(x.shape, F32),
        compiler_params=_params("parallel", "parallel"),
        name="moe",
    )(*pre_args, x, mod, *consts, w_up, w_dn, final_g)


def _diff_weights(w_in):
    d = w_in.shape[0]

    def regroup(w):
        return w.reshape(d, 2, DIFF_HEADS, DIFF_HEAD_DIM).transpose(0, 2, 1, 3).reshape(d, DIFF_QK)

    return jnp.concatenate([regroup(w_in[:, :DIFF_QK]), regroup(w_in[:, DIFF_QK:2 * DIFF_QK]),
                            w_in[:, 2 * DIFF_QK:]], axis=1).astype(BF16)


def _gla_weights(w_in, gate_w, gate_b):
    d = w_in.shape[0]
    main = 2 * GLA_KEY + 2 * GLA_VAL
    w_r = jnp.zeros((d, LANES), F32).at[:, :2 * GLA_GATE_RANK].set(w_in[:, main:]).astype(BF16)
    w_g = jnp.zeros((LANES, 2 * GLA_KEY), F32)
    w_g = w_g.at[:GLA_GATE_RANK, :GLA_KEY].set(gate_w[0])
    w_g = w_g.at[GLA_GATE_RANK:2 * GLA_GATE_RANK, GLA_KEY:].set(gate_w[1]).astype(BF16)
    return w_in[:, :main].astype(BF16), w_r, w_g, gate_b.reshape(1, 2 * GLA_KEY)


def _mla_weights(w_in, w_uq, w_ukv):
    d = w_in.shape[0]
    w = jnp.zeros((d, MLA_Q_LORA + MLA_KV_LORA + LANES), F32).at[:, :w_in.shape[1]].set(w_in).astype(BF16)
    wq = w_uq.reshape(MLA_Q_LORA, MLA_HEADS, MLA_NOPE + MLA_ROPE)
    wq = jnp.pad(wq, ((0, 0), (0, 0), (0, MLA_QK_PAD - MLA_NOPE - MLA_ROPE)))
    wq = wq.reshape(MLA_Q_LORA, MLA_HEADS * MLA_QK_PAD).astype(BF16)
    wkv = w_ukv.reshape(MLA_KV_LORA, MLA_HEADS, MLA_NOPE + MLA_V)
    wkn = wkv[:, :, :MLA_NOPE].reshape(MLA_KV_LORA, MLA_HEADS * MLA_NOPE).astype(BF16)
    wv = wkv[:, :, MLA_NOPE:].reshape(MLA_KV_LORA, MLA_HEADS * MLA_V).astype(BF16)
    return w, wq, wkn, wv


def kernel(x, c, ctx, c_ctx, ada_w, ada_b, norm_g, router_w, router_b, moe_w_up, moe_w_down, shared_w_up,
           shared_w_down, diff_w_in, diff_lam, diff_norm_g, diff_w_out, gla_w_in, gla_gate_w, gla_gate_b,
           gla_norm_g, gla_w_out, mla_w_in, mla_q_norm_g, mla_w_uq, mla_kv_norm_g, mla_w_ukv, mla_w_out,
           final_norm_g):
    b, s, d = x.shape
    cl = ctx.shape[1]
    assert b + 1 <= COND_ROWS and d == D_MODEL
    depth = ada_w.shape[0]
    ctx_row = b

    cond = jnp.zeros((COND_ROWS, d), F32).at[:b].set(c).at[ctx_row].set(c_ctx)
    mods = _ada_params(cond, ada_w, ada_b).reshape(depth, COND_ROWS, N_MOD, d)

    tables = _rope_tables(s)
    rw_hi, rw_lo = _split_bf16(router_w.T)
    rb = router_b.reshape(N_EXPERTS, 1)
    w_up_all = moe_w_up.astype(BF16)
    w_dn_all = moe_w_down.astype(BF16)
    final_g = final_norm_g.reshape(1, d)

    tm_lat = 512
    n_ctx = b * cl
    tm_ctx = 512 if n_ctx % 512 == 0 else cl
    x_lat = x
    x_ctx = ctx.reshape(1, n_ctx, d)

    for i in range(depth):
        need_ctx = i < depth - 1
        mod = mods[i]
        g1 = norm_g[i, 0].reshape(1, d)
        g2 = norm_g[i, 1].reshape(1, d)
        kind, j = i % N_MIXERS, i // N_MIXERS
        if kind == 0:
            lam_init = 0.8 - 0.6 * math.exp(-0.3 * i)
            w = _diff_weights(diff_w_in[j])
            w_out = diff_w_out[j].astype(BF16)
            extras = (diff_lam[j], diff_norm_g[j].reshape(1, 2 * DIFF_HEAD_DIM))
            q_l, k_l, v_l = _diff_proj(x_lat, mod, None, g1, w, tables, tm_lat)
            q_c, k_c, v_c = (t.reshape(b, cl, -1) for t in _diff_proj(x_ctx, mod, ctx_row, g1, w, None, tm_ctx))
            o_lat = _flash(q_l, k_l, v_l, (k_c, v_c), extras, heads=DIFF_HEADS, dq=2 * DIFF_HEAD_DIM,
                           tq=512, tk=1024, lam_init=lam_init, hb=DIFF_HEADS)
            mix_lat = ("plain", o_lat, w_out)
            if need_ctx:
                o_ctx = _flash(q_c, k_c, v_c, None, extras, heads=DIFF_HEADS, dq=2 * DIFF_HEAD_DIM,
                               tq=cl, tk=cl, lam_init=lam_init, hb=DIFF_HEADS)
                mix_ctx = ("plain", o_ctx, w_out)
        elif kind == 1:
            w, w_r, w_g, b_g = _gla_weights(gla_w_in[j], gla_gate_w[j], gla_gate_b[j])
            w_out = gla_w_out[j].astype(BF16)
            ng = gla_norm_g[j].reshape(1, GLA_DV)
            q_l, k_l, v_l, og_l, gt_l = _gla_proj(x_lat, mod, None, g1, w, w_r, w_g, b_g, tm_lat)
            q_c, k_c, v_c, og_c, gt_c = (t.reshape(b, cl, -1)
                                         for t in _gla_proj(x_ctx, mod, ctx_row, g1, w, w_r, w_g, b_g, tm_ctx))
            s0 = jnp.zeros((b, 2, GLA_HEADS, GLA_DV, GLA_DK), F32)
            o_c, s_c = _gla_scan(q_c, k_c, v_c, gt_c, s0, cl)
            o_l, _ = _gla_scan(q_l, k_l, v_l, gt_l, s_c, 256)
            mix_lat = ("gla", o_l, og_l, ng, w_out)
            if need_ctx:
                mix_ctx = ("gla", o_c, og_c, ng, w_out)
        else:
            w, wq, wkn, wv = _mla_weights(mla_w_in[j], mla_w_uq[j], mla_w_ukv[j])
            w_out = mla_w_out[j].astype(BF16)
            qg = mla_q_norm_g[j].reshape(1, MLA_Q_LORA)
            kvg = mla_kv_norm_g[j].reshape(1, MLA_KV_LORA)
            q_l, k_l, v_l = _mla_proj(x_lat, mod, None, g1, w, qg, wq, kvg, wkn, wv, tables, tm_lat)
            q_c, k_c, v_c = (t.reshape(b, cl, -1)
                             for t in _mla_proj(x_ctx, mod, ctx_row, g1, w, qg, wq, kvg, wkn, wv, None, tm_ctx))
            o_lat = _flash(q_l, k_l, v_l, (k_c, v_c), None, heads=MLA_HEADS, dq=MLA_QK_PAD, tq=512, tk=1024,
                           hb=MLA_HEADS)
            mix_lat = ("plain", o_lat, w_out)
            if need_ctx:
                o_ctx = _flash(q_c, k_c, v_c, None, None, heads=MLA_HEADS, dq=MLA_QK_PAD, tq=cl, tk=cl,
                               hb=MLA_HEADS)
                mix_ctx = ("plain", o_ctx, w_out)

        sh_up = shared_w_up[i].astype(BF16)
        sh_dn = shared_w_down[i].astype(BF16)
        last = i == depth - 1
        x_lat = _moe(mix_lat, x_lat, mod, None, g2, rw_hi, rw_lo, rb, sh_up, sh_dn, w_up_all, w_dn_all, i, final_g, last,
                     tm_lat)
        if need_ctx:
            x_ctx = _moe(mix_ctx, x_ctx, mod, ctx_row, g2, rw_hi, rw_lo, rb, sh_up, sh_dn, w_up_all, w_dn_all, i, final_g,
                         False, tm_ctx)
    return x_lat
```

```python
import functools
import math

import jax
import jax.numpy as jnp
from jax import lax
from jax.experimental import pallas as pl
from jax.experimental.pallas import tpu as pltpu

F32 = jnp.float32
BF16 = jnp.bfloat16

D_MODEL = 1024
DEPTH = 4
GRID_W = 64
N_MIXERS = 3
NORM_EPS = 1e-6
ROPE_BASE = 10000.0
N_MOD = 6

DIFF_HEADS = 8
DIFF_HEAD_DIM = 64
DIFF_QK = 2 * DIFF_HEADS * DIFF_HEAD_DIM
DIFF_V = 2 * DIFF_HEADS * DIFF_HEAD_DIM

GLA_HEADS = 4
GLA_KEY = D_MODEL // 2
GLA_VAL = D_MODEL
GLA_DK = GLA_KEY // GLA_HEADS
GLA_DV = GLA_VAL // GLA_HEADS
GLA_GATE_RANK = 16
GLA_GATE_NORM = 16.0
GLA_CHUNK = 64

MLA_HEADS = 8
MLA_Q_LORA = 384
MLA_KV_LORA = 256
MLA_NOPE = 128
MLA_ROPE = 64
MLA_V = 128
MLA_SCALE = (MLA_NOPE + MLA_ROPE) ** -0.5
MLA_QK_PAD = 256

N_EXPERTS = 16
N_GROUPS = 4
EXPERTS_PER_GROUP = N_EXPERTS // N_GROUPS
EXPERT_FF = 256
SHARED_FF = 256

LANES = 128
COND_ROWS = 16
NEG_BIG = -1e30
LOG2E = math.log2(math.e)
VMEM_LIMIT = 56 * 1024 * 1024

NT_DIMS = (((1,), (1,)), ((), ()))
TN_DIMS = (((0,), (0,)), ((), ()))


def _dot(a, b):
    return jnp.dot(a, b, preferred_element_type=F32)


def _dot_nt(a, b):
    return lax.dot_general(a, b, NT_DIMS, preferred_element_type=F32)


def _dot_tn(a, b):
    return lax.dot_general(a, b, TN_DIMS, preferred_element_type=F32)


def _split_bf16(a):
    hi = a.astype(BF16)
    lo = (a - hi.astype(F32)).astype(BF16)
    return hi, lo


def _silu(a):
    return a * jax.nn.sigmoid(a)


def _params(*sem):
    return pltpu.CompilerParams(dimension_semantics=sem, vmem_limit_bytes=VMEM_LIMIT)


def _const_spec(shape):
    nd = len(shape)
    return pl.BlockSpec(shape, lambda *_: (0,) * nd)


def _ada_kernel(cond_ref, w_ref, b_ref, o_ref):
    a_hi, a_lo = _split_bf16(_silu(cond_ref[...]))
    w_hi, w_lo = _split_bf16(w_ref[0])
    o_ref[0] = _dot(a_hi, w_hi) + _dot(a_lo, w_hi) + _dot(a_hi, w_lo) + b_ref[0]


def _ada_params(cond, ada_w, ada_b):
    depth, d, n = ada_w.shape
    tn = 1024
    return pl.pallas_call(
        _ada_kernel,
        grid=(depth, n // tn),
        in_specs=[
            pl.BlockSpec((COND_ROWS, d), lambda i, j: (0, 0)),
            pl.BlockSpec((1, d, tn), lambda i, j: (i, 0, j)),
            pl.BlockSpec((1, 1, tn), lambda i, j: (i, 0, j)),
        ],
        out_specs=pl.BlockSpec((1, COND_ROWS, tn), lambda i, j: (i, 0, j)),
        out_shape=jax.ShapeDtypeStruct((depth, COND_ROWS, n), F32),
        compiler_params=_params("parallel", "parallel"),
        name="ada_params",
    )(cond, ada_w, ada_b.reshape(depth, 1, n))


def _norm_mod(x, g, shift, scale):
    ms = jnp.mean(x * x, axis=-1, keepdims=True)
    return (x * lax.rsqrt(ms + NORM_EPS) * g) * (1.0 + scale) + shift


def _rms(x, g):
    ms = jnp.mean(x * x, axis=-1, keepdims=True)
    return x * lax.rsqrt(ms + NORM_EPS) * g


def _rope_chunk(c, cos, sa, sb):
    return c * cos + pltpu.roll(c, LANES - 16, 1) * sa + pltpu.roll(c, 16, 1) * sb


def _rope_tables(seq):
    t = jnp.arange(seq)
    pos_row = (t // GRID_W).astype(F32)
    pos_col = (t % GRID_W).astype(F32)
    inv = ROPE_BASE ** (-jnp.arange(0, 32, 2, dtype=F32) / 32)
    lane = jnp.arange(LANES)
    d = lane % 64
    r = d % 32
    first = (r < 16)[None, :]
    pos = jnp.where((d // 32)[None, :] == 0, pos_row[:, None], pos_col[:, None])
    ang = pos * inv[r % 16][None, :]
    cos, sin = jnp.cos(ang), jnp.sin(ang)
    return cos, jnp.where(first, -sin, 0.0), jnp.where(first, 0.0, sin)


def _token_specs(tm, d, mod_row):
    x_spec = pl.BlockSpec((1, tm, d), lambda b, t: (b, t, 0))
    if mod_row is None:
        mod_spec = pl.BlockSpec((1, N_MOD, d), lambda b, t: (b, 0, 0))
    else:
        mod_spec = pl.BlockSpec((1, N_MOD, d), lambda b, t: (mod_row, 0, 0))
    return x_spec, mod_spec


def _rope_specs(tm):
    return [pl.BlockSpec((tm, LANES), lambda b, t: (t, 0))] * 3


def _out_spec(tm, n):
    return pl.BlockSpec((1, tm, n), lambda b, t: (b, t, 0))


def _diff_proj_kernel(*refs, rope):
    if rope:
        x_ref, mod_ref, g_ref, w_ref, cos_ref, sa_ref, sb_ref, q_ref, k_ref, v_ref = refs
        cos, sa, sb = cos_ref[...], sa_ref[...], sb_ref[...]
    else:
        x_ref, mod_ref, g_ref, w_ref, q_ref, k_ref, v_ref = refs
    h = _norm_mod(x_ref[0], g_ref[...], mod_ref[0, 0:1, :], mod_ref[0, 1:2, :]).astype(BF16)
    for idx, o_ref in enumerate((q_ref, k_ref)):
        for c in range(DIFF_QK // 256):
            y = _dot(h, w_ref[:, idx * DIFF_QK + c * 256: idx * DIFF_QK + (c + 1) * 256])
            for s in range(2):
                ys = y[:, s * LANES:(s + 1) * LANES]
                if rope:
                    ys = _rope_chunk(ys, cos, sa, sb)
                if idx == 0:
                    ys = ys * (DIFF_HEAD_DIM ** -0.5 * LOG2E)
                o_ref[0, :, c * 256 + s * LANES: c * 256 + (s + 1) * LANES] = ys.astype(BF16)
    for c in range(DIFF_V // 256):
        y = _dot(h, w_ref[:, 2 * DIFF_QK + c * 256: 2 * DIFF_QK + (c + 1) * 256])
        v_ref[0, :, c * 256:(c + 1) * 256] = y.astype(BF16)


def _diff_proj(x, mod, mod_row, norm_g, w, tables, tm):
    bx, s, d = x.shape
    rope = tables is not None
    x_spec, mod_spec = _token_specs(tm, d, mod_row)
    in_specs = [x_spec, mod_spec, _const_spec((1, d)), _const_spec(w.shape)]
    args = [x, mod, norm_g, w]
    if rope:
        in_specs += _rope_specs(tm)
        args += list(tables)
    out = jax.ShapeDtypeStruct((bx, s, DIFF_QK), BF16)
    return pl.pallas_call(
        functools.partial(_diff_proj_kernel, rope=rope),
        grid=(bx, s // tm),
        in_specs=in_specs,
        out_specs=[_out_spec(tm, DIFF_QK)] * 3,
        out_shape=[out, out, out],
        compiler_params=_params("parallel", "parallel"),
        name="diff_proj",
    )(*args)


def _flash_kernel(*refs, n_maps, has_prefix, lam_init, hb, dq):
    refs = list(refs)
    q_ref, k_ref, v_ref = refs[:3]
    pos = 3
    if has_prefix:
        kc_ref, vc_ref = refs[pos:pos + 2]
        pos += 2
    if n_maps == 2:
        lam_ref, ng_ref = refs[pos:pos + 2]
        pos += 2
    o_ref = refs[pos]
    scratch = refs[pos + 1:]
    if n_maps == 2:
        qm_ref, m_ref, acc_ref = scratch
    else:
        m_ref, acc_ref = scratch
    kv = pl.program_id(3)

    def step(hd, mi, k_src, v_src):
        ci = hd * n_maps + mi
        q = qm_ref[ci] if n_maps == 2 else q_ref[0, :, hd * dq:(hd + 1) * dq]
        k = k_src[0, :, hd * dq:(hd + 1) * dq]
        v = v_src[0, :, hd * LANES:(hd + 1) * LANES]
        s = _dot_nt(q, k)
        m_prev = m_ref[ci]
        m_new = jnp.maximum(m_prev, jnp.max(s, axis=1, keepdims=True))
        alpha = jnp.exp2(m_prev - m_new)
        p = jnp.exp2(s - jnp.tile(m_new, (1, s.shape[1] // LANES)))
        v1 = jnp.concatenate([v, jnp.ones_like(v)], axis=1)
        acc_ref[ci] = acc_ref[ci] * jnp.tile(alpha, (1, 2)) + _dot(p.astype(BF16), v1)
        m_ref[ci] = m_new

    def all_steps(k_src, v_src):
        for hd in range(hb):
            for mi in range(n_maps):
                step(hd, mi, k_src, v_src)

    @pl.when(kv == 0)
    def _init():
        m_ref[...] = jnp.full(m_ref.shape, NEG_BIG, F32)
        acc_ref[...] = jnp.zeros(acc_ref.shape, F32)
        if n_maps == 2:
            for hd in range(hb):
                q = q_ref[0, :, hd * dq:(hd + 1) * dq]
                lane = lax.broadcasted_iota(jnp.int32, q.shape, 1)
                qm_ref[2 * hd] = jnp.where(lane < DIFF_HEAD_DIM, q, jnp.zeros_like(q))
                qm_ref[2 * hd + 1] = jnp.where(lane >= DIFF_HEAD_DIM, q, jnp.zeros_like(q))
        if has_prefix:
            all_steps(kc_ref, vc_ref)

    all_steps(k_ref, v_ref)

    @pl.when(kv == pl.num_programs(3) - 1)
    def _finish():
        if n_maps == 2:
            lf = lam_ref[...]
            l1 = jnp.sum(lf[0:1] * lf[1:2], axis=1, keepdims=True)
            l2 = jnp.sum(lf[2:3] * lf[3:4], axis=1, keepdims=True)
            lam = jnp.exp(l1) - jnp.exp(l2) + lam_init
        for hd in range(hb):
            a0 = acc_ref[hd * n_maps]
            o = a0[:, :LANES] / a0[:, LANES:]
            if n_maps == 2:
                a1 = acc_ref[hd * n_maps + 1]
                o = o - lam * (a1[:, :LANES] / a1[:, LANES:])
                o = _rms(o, ng_ref[...]) * (1.0 - lam_init)
            o_ref[0, :, hd * LANES:(hd + 1) * LANES] = o.astype(o_ref.dtype)


def _flash(q, k, v, prefix, extras, *, heads, dq, tq, tk, lam_init=0.0, hb=1):
    b, s, _ = q.shape
    n_maps = 2 if extras is not None else 1
    in_specs = [
        pl.BlockSpec((1, tq, hb * dq), lambda bi, h, qi, kv: (bi, qi, h)),
        pl.BlockSpec((1, tk, hb * dq), lambda bi, h, qi, kv: (bi, kv, h)),
        pl.BlockSpec((1, tk, hb * LANES), lambda bi, h, qi, kv: (bi, kv, h)),
    ]
    args = [q, k, v]
    if prefix is not None:
        cl = prefix[0].shape[1]
        in_specs += [
            pl.BlockSpec((1, cl, hb * dq), lambda bi, h, qi, kv: (bi, 0, h)),
            pl.BlockSpec((1, cl, hb * LANES), lambda bi, h, qi, kv: (bi, 0, h)),
        ]
        args += list(prefix)
    scratch = []
    if n_maps == 2:
        in_specs += [_const_spec(extras[0].shape), _const_spec(extras[1].shape)]
        args += list(extras)
        scratch.append(pltpu.VMEM((2 * hb, tq, dq), BF16))
    scratch += [pltpu.VMEM((hb * n_maps, tq, LANES), F32), pltpu.VMEM((hb * n_maps, tq, 2 * LANES), F32)]
    return pl.pallas_call(
        functools.partial(_flash_kernel, n_maps=n_maps, has_prefix=prefix is not None, lam_init=lam_init,
                          hb=hb, dq=dq),
        grid=(b, heads // hb, s // tq, k.shape[1] // tk),
        in_specs=in_specs,
        out_specs=pl.BlockSpec((1, tq, hb * LANES), lambda bi, h, qi, kv: (bi, qi, h)),
        out_shape=jax.ShapeDtypeStruct((b, s, heads * LANES), BF16),
        scratch_shapes=scratch,
        compiler_params=_params("parallel", "parallel", "parallel", "arbitrary"),
        name="flash_diff" if n_maps == 2 else "flash_mla",
    )(*args)


def _gla_proj_kernel(x_ref, mod_ref, g_ref, w_ref, wr_ref, wg_ref, bg_ref, q_ref, k_ref, v_ref, og_ref, gate_ref):
    h = _norm_mod(x_ref[0], g_ref[...], mod_ref[0, 0:1, :], mod_ref[0, 1:2, :]).astype(BF16)
    col = 0
    for o_ref, width, scale in ((q_ref, GLA_KEY, GLA_DK ** -0.5), (k_ref, GLA_KEY, None),
                                (v_ref, GLA_VAL, None), (og_ref, GLA_VAL, None)):
        for c in range(width // 256):
            y = _dot(h, w_ref[:, col + c * 256: col + (c + 1) * 256])
            if scale is not None:
                y = y * scale
            o_ref[0, :, c * 256:(c + 1) * 256] = y.astype(BF16)
        col += width
    r_hi, r_lo = _split_bf16(_dot(h, wr_ref[...]))
    for c in range(2 * GLA_KEY // 256):
        wg = wg_ref[:, c * 256:(c + 1) * 256]
        z = _dot(r_hi, wg) + _dot(r_lo, wg) + bg_ref[:, c * 256:(c + 1) * 256]
        log_sig = jnp.minimum(z, 0.0) - jnp.log(1.0 + jnp.exp(-jnp.abs(z)))
        gate_ref[0, :, c * 256:(c + 1) * 256] = log_sig / GLA_GATE_NORM


def _gla_proj(x, mod, mod_row, norm_g, w, wr, wg, bg, tm):
    bx, s, d = x.shape
    x_spec, mod_spec = _token_specs(tm, d, mod_row)
    return pl.pallas_call(
        _gla_proj_kernel,
        grid=(bx, s // tm),
        in_specs=[x_spec, mod_spec, _const_spec((1, d)), _const_spec(w.shape), _const_spec(wr.shape),
                  _const_spec(wg.shape), _const_spec(bg.shape)],
        out_specs=[_out_spec(tm, GLA_KEY), _out_spec(tm, GLA_KEY), _out_spec(tm, GLA_VAL),
                   _out_spec(tm, GLA_VAL), _out_spec(tm, 2 * GLA_KEY)],
        out_shape=[jax.ShapeDtypeStruct((bx, s, GLA_KEY), BF16), jax.ShapeDtypeStruct((bx, s, GLA_KEY), BF16),
                   jax.ShapeDtypeStruct((bx, s, GLA_VAL), BF16), jax.ShapeDtypeStruct((bx, s, GLA_VAL), BF16),
                   jax.ShapeDtypeStruct((bx, s, 2 * GLA_KEY), F32)],
        compiler_params=_params("parallel", "parallel"),
        name="gla_proj",
    )(x, mod, norm_g, w, wr, wg, bg)


def _gla_scan_kernel(q_ref, k_ref, v_ref, g_ref, s0_ref, o_ref, sfin_ref, st_ref):
    direction = pl.program_id(1)
    i = pl.program_id(2)
    n_chunks = q_ref.shape[1] // GLA_CHUNK

    @pl.when(i == 0)
    def _load_state():
        st_ref[...] = s0_ref[0, 0]

    def run(backward):
        tb = q_ref.shape[1]
        row = lax.broadcasted_iota(jnp.int32, (tb, tb), 0)
        colm = lax.broadcasted_iota(jnp.int32, (tb, tb), 1)
        same = (row // GLA_CHUNK) == (colm // GLA_CHUNK)
        seen = (colm >= row) if backward else (colm <= row)
        allowed = seen[:GLA_CHUNK, :GLA_CHUNK]
        seen_m = jnp.where(same & seen, 1.0, 0.0).astype(BF16)
        rest_m = jnp.where(same & jnp.logical_not(seen), 1.0, 0.0).astype(BF16)
        g_hi, g_lo = _split_bf16(g_ref[0])
        cum = _dot(seen_m, g_hi) + _dot(seen_m, g_lo)
        rest = _dot(rest_m, g_hi) + _dot(rest_m, g_lo)
        q = q_ref[0].astype(F32)
        k = k_ref[0].astype(F32)
        q_dec = (q * jnp.exp(cum)).astype(BF16)
        k_inv = (k * jnp.exp(-cum)).astype(BF16)
        k_end = (k * jnp.exp(rest)).astype(BF16)
        decay = jnp.exp(cum + rest)
        order = range(n_chunks - 1, -1, -1) if backward else range(n_chunks)
        for c in order:
            sl = slice(c * GLA_CHUNK, (c + 1) * GLA_CHUNK)
            for h in range(GLA_HEADS):
                ksl = slice(h * GLA_DK, (h + 1) * GLA_DK)
                vsl = slice(h * GLA_DV, (h + 1) * GLA_DV)
                v = v_ref[0, sl, vsl]
                qd = q_dec[sl, ksl]
                a = jnp.where(allowed, _dot_nt(qd, k_inv[sl, ksl]), 0.0)
                st = st_ref[h]
                o_ref[0, 0, sl, vsl] = _dot(a.astype(BF16), v) + _dot_nt(qd, st.astype(BF16))
                st_ref[h] = st * decay[c * GLA_CHUNK:c * GLA_CHUNK + 1, ksl] + _dot_tn(v, k_end[sl, ksl])

    pl.when(direction == 0)(lambda: run(False))
    pl.when(direction == 1)(lambda: run(True))

    @pl.when(i == pl.num_programs(2) - 1)
    def _store_state():
        sfin_ref[0, 0] = st_ref[...]


def _gla_scan(q, k, v, g, s0, tb):
    b, s, _ = q.shape
    nb = s // tb

    def blk(d, i):
        return i + d * (nb - 1 - 2 * i)

    st_spec = pl.BlockSpec((1, 1, GLA_HEADS, GLA_DV, GLA_DK), lambda bi, d, i: (bi, d, 0, 0, 0))
    return pl.pallas_call(
        _gla_scan_kernel,
        grid=(b, 2, nb),
        in_specs=[
            pl.BlockSpec((1, tb, GLA_KEY), lambda bi, d, i: (bi, blk(d, i), 0)),
            pl.BlockSpec((1, tb, GLA_KEY), lambda bi, d, i: (bi, blk(d, i), 0)),
            pl.BlockSpec((1, tb, GLA_VAL), lambda bi, d, i: (bi, blk(d, i), 0)),
            pl.BlockSpec((1, tb, GLA_KEY), lambda bi, d, i: (bi, blk(d, i), d)),
            st_spec,
        ],
        out_specs=[pl.BlockSpec((1, 1, tb, GLA_VAL), lambda bi, d, i: (d, bi, blk(d, i), 0)), st_spec],
        out_shape=[jax.ShapeDtypeStruct((2, b, s, GLA_VAL), F32),
                   jax.ShapeDtypeStruct((b, 2, GLA_HEADS, GLA_DV, GLA_DK), F32)],
        scratch_shapes=[pltpu.VMEM((GLA_HEADS, GLA_DV, GLA_DK), F32)],
        compiler_params=_params("parallel", "parallel", "arbitrary"),
        name="gla_scan",
    )(q, k, v, g, s0)


def _mla_proj_kernel(*refs, rope):
    if rope:
        (x_ref, mod_ref, g_ref, w_ref, qg_ref, wq_ref, kvg_ref, wkn_ref, wv_ref,
         cos_ref, sa_ref, sb_ref, q_ref, k_ref, v_ref) = refs
        cos, sa, sb = cos_ref[...], sa_ref[...], sb_ref[...]
    else:
        x_ref, mod_ref, g_ref, w_ref, qg_ref, wq_ref, kvg_ref, wkn_ref, wv_ref, q_ref, k_ref, v_ref = refs
    h = _norm_mod(x_ref[0], g_ref[...], mod_ref[0, 0:1, :], mod_ref[0, 1:2, :]).astype(BF16)
    y = _dot(h, w_ref[...])
    c_q = _rms(y[:, :MLA_Q_LORA], qg_ref[...]).astype(BF16)
    c_kv = _rms(y[:, MLA_Q_LORA:MLA_Q_LORA + MLA_KV_LORA], kvg_ref[...]).astype(BF16)
    k_rope = y[:, MLA_Q_LORA + MLA_KV_LORA:]
    if rope:
        k_rope = _rope_chunk(k_rope, cos, sa, sb)
    k_rope = k_rope.astype(BF16)
    for hd in range(MLA_HEADS):
        base = hd * MLA_QK_PAD
        qh = _dot(c_q, wq_ref[:, base: base + MLA_QK_PAD])
        q_rope = qh[:, LANES:]
        if rope:
            q_rope = _rope_chunk(q_rope, cos, sa, sb)
        q_ref[0, :, base: base + LANES] = (qh[:, :LANES] * (MLA_SCALE * LOG2E)).astype(BF16)
        q_ref[0, :, base + LANES: base + MLA_QK_PAD] = (q_rope * (MLA_SCALE * LOG2E)).astype(BF16)
        k_ref[0, :, base + LANES: base + MLA_QK_PAD] = k_rope
    for c in range(MLA_HEADS * MLA_NOPE // 256):
        kn = _dot(c_kv, wkn_ref[:, c * 256:(c + 1) * 256]).astype(BF16)
        for s in range(2):
            hd = 2 * c + s
            k_ref[0, :, hd * MLA_QK_PAD: hd * MLA_QK_PAD + LANES] = kn[:, s * LANES:(s + 1) * LANES]
        v_ref[0, :, c * 256:(c + 1) * 256] = _dot(c_kv, wv_ref[:, c * 256:(c + 1) * 256]).astype(BF16)


def _mla_proj(x, mod, mod_row, norm_g, w, qg, wq, kvg, wkn, wv, tables, tm):
    bx, s, d = x.shape
    rope = tables is not None
    x_spec, mod_spec = _token_specs(tm, d, mod_row)
    consts = [norm_g, w, qg, wq, kvg, wkn, wv]
    in_specs = [x_spec, mod_spec] + [_const_spec(a.shape) for a in consts]
    args = [x, mod] + consts
    if rope:
        in_specs += _rope_specs(tm)
        args += list(tables)
    qk_w = MLA_HEADS * MLA_QK_PAD
    return pl.pallas_call(
        functools.partial(_mla_proj_kernel, rope=rope),
        grid=(bx, s // tm),
        in_specs=in_specs,
        out_specs=[_out_spec(tm, qk_w), _out_spec(tm, qk_w), _out_spec(tm, MLA_HEADS * MLA_V)],
        out_shape=[jax.ShapeDtypeStruct((bx, s, qk_w), BF16), jax.ShapeDtypeStruct((bx, s, qk_w), BF16),
                   jax.ShapeDtypeStruct((bx, s, MLA_HEADS * MLA_V), BF16)],
        compiler_params=_params("parallel", "parallel"),
        name="mla_proj",
    )(*args)


def _top2_sum(a, b, c, d):
    hi1, lo1 = jnp.maximum(a, b), jnp.minimum(a, b)
    hi2, lo2 = jnp.maximum(c, d), jnp.minimum(c, d)
    return jnp.maximum(hi1, hi2) + jnp.maximum(jnp.minimum(hi1, hi2), jnp.maximum(lo1, lo2))


def _route_rows(scores, sel):
    eg = EXPERTS_PER_GROUP
    gs = [_top2_sum(*sel[g * eg:(g + 1) * eg]) for g in range(N_GROUPS)]
    best = jnp.maximum(jnp.maximum(gs[0], gs[1]), jnp.maximum(gs[2], gs[3]))
    gidx = jnp.where(gs[0] >= best, 0, jnp.where(gs[1] >= best, 1, jnp.where(gs[2] >= best, 2, 3)))

    def pick(rows, j):
        out = rows[(N_GROUPS - 1) * eg + j]
        for g in range(N_GROUPS - 2, -1, -1):
            out = jnp.where(gidx == g, rows[g * eg + j], out)
        return out

    loc_sel = [pick(sel, j) for j in range(eg)]
    loc_sc = [pick(scores, j) for j in range(eg)]
    weights = []
    for j in range(eg):
        rank = jnp.zeros_like(loc_sel[j])
        for i in range(eg):
            if i == j:
                continue
            beats = (loc_sel[i] >= loc_sel[j]) if i < j else (loc_sel[i] > loc_sel[j])
            rank = rank + jnp.where(beats, 1.0, 0.0)
        weights.append(jnp.where(rank < 1.5, loc_sc[j], 0.0))
    den = weights[0] + weights[1] + weights[2] + weights[3]
    weights = [w / den for w in weights]
    return [jnp.where(gidx == e // eg, weights[e % eg], 0.0) for e in range(N_EXPERTS)]


def _mixer_output(pre, refs):
    if pre == "gla":
        of_ref, ob_ref, og_ref, ng_ref, w_ref = refs
        o = of_ref[0, 0] + ob_ref[0, 0]
        parts = []
        for h in range(GLA_HEADS):
            sl = slice(h * GLA_DV, (h + 1) * GLA_DV)
            parts.append((_rms(o[:, sl], ng_ref[...]) * _silu(og_ref[0, :, sl].astype(F32))).astype(BF16))
        return _dot(jnp.concatenate(parts, axis=1), w_ref[...])
    o_ref, w_ref = refs
    return _dot(o_ref[0], w_ref[...])


def _moe_kernel(*refs, pre, final_norm):
    n_pre = 5 if pre == "gla" else 2
    (x_ref, mod_ref, g_ref, rwh_ref, rwl_ref, rb_ref, shu_ref, shd_ref, wup_ref, wdn_ref, fg_ref,
     out_ref) = refs[n_pre:]
    x = x_ref[0] + mod_ref[0, 2:3, :] * _mixer_output(pre, refs[:n_pre])
    tm = x.shape[0]
    h, h_lo = _split_bf16(_norm_mod(x, g_ref[...], mod_ref[0, 3:4, :], mod_ref[0, 4:5, :]))
    logits = _dot_nt(rwh_ref[...], h) + _dot_nt(rwh_ref[...], h_lo) + _dot_nt(rwl_ref[...], h)
    scores = jax.nn.sigmoid(logits)
    sel = scores + rb_ref[...]
    rows = _route_rows([scores[e:e + 1, :] for e in range(N_EXPERTS)],
                       [sel[e:e + 1, :] for e in range(N_EXPERTS)])
    comb = jnp.concatenate(rows + [jnp.zeros((LANES - N_EXPERTS, tm), F32)], axis=0).T
    a = _dot(h, shu_ref[:, :SHARED_FF])
    u = _dot(h, shu_ref[:, SHARED_FF:])
    acc = _dot((_silu(a) * u).astype(BF16), shd_ref[...])
    for grp in range(N_GROUPS):
        acts = []
        for j in range(EXPERTS_PER_GROUP):
            e = grp * EXPERTS_PER_GROUP + j
            a = _dot(h, wup_ref[0, e, :, :EXPERT_FF])
            u = _dot(h, wup_ref[0, e, :, EXPERT_FF:])
            scale = jnp.broadcast_to(comb[:, e:e + 1], a.shape)
            acts.append((_silu(a) * u * scale).astype(BF16))
        w_dn = wdn_ref[0, grp * EXPERTS_PER_GROUP:(grp + 1) * EXPERTS_PER_GROUP]
        acc = acc + _dot(jnp.concatenate(acts, axis=1), w_dn.reshape(EXPERTS_PER_GROUP * EXPERT_FF, -1))
    out = x + mod_ref[0, 5:6, :] * acc
    if final_norm:
        out = _rms(out, fg_ref[...])
    out_ref[0] = out


def _moe(mixer, x, mod, mod_row, norm_g, rw_hi, rw_lo, rb, sh_up, sh_dn, w_up, w_dn, layer, final_g, final_norm, tm):
    bx, s, d = x.shape
    x_spec, mod_spec = _token_specs(tm, d, mod_row)
    pre = mixer[0]
    if pre == "gla":
        o2, og, ng, w_out = mixer[1:]
        o2 = o2.reshape(2, bx, s, GLA_VAL)
        pre_specs = [pl.BlockSpec((1, 1, tm, GLA_VAL), lambda b, t: (0, b, t, 0)),
                     pl.BlockSpec((1, 1, tm, GLA_VAL), lambda b, t: (1, b, t, 0)),
                     _out_spec(tm, GLA_VAL), _const_spec(ng.shape), _const_spec(w_out.shape)]
        pre_args = [o2, o2, og.reshape(bx, s, GLA_VAL), ng, w_out]
    else:
        o, w_out = mixer[1:]
        o = o.reshape(bx, s, -1)
        pre_specs = [_out_spec(tm, o.shape[2]), _const_spec(w_out.shape)]
        pre_args = [o, w_out]
    consts = [norm_g, rw_hi, rw_lo, rb, sh_up, sh_dn]
    resident = pl.Buffered(1)
    in_specs = (pre_specs + [x_spec, mod_spec] + [_const_spec(a.shape) for a in consts] + [
        pl.BlockSpec((1,) + w_up.shape[1:], lambda b, t: (layer, 0, 0, 0), pipeline_mode=resident),
        pl.BlockSpec((1,) + w_dn.shape[1:], lambda b, t: (layer, 0, 0, 0), pipeline_mode=resident),
        _const_spec(final_g.shape)])
    return pl.pallas_call(
        functools.partial(_moe_kernel, pre=pre, final_norm=final_norm),
        grid=(bx, s // tm),
        in_specs=in_specs,
        out_specs=x_spec,
        out_shape=jax.ShapeDtypeStruct(x.shape, F32),
        compiler_params=_params("parallel", "parallel"),
        name="moe",
    )(*pre_args, x, mod, *consts, w_up, w_dn, final_g)


def _diff_weights(w_in):
    d = w_in.shape[0]

    def regroup(w):
        return w.reshape(d, 2, DIFF_HEADS, DIFF_HEAD_DIM).transpose(0, 2, 1, 3).reshape(d, DIFF_QK)

    return jnp.concatenate([regroup(w_in[:, :DIFF_QK]), regroup(w_in[:, DIFF_QK:2 * DIFF_QK]),
                            w_in[:, 2 * DIFF_QK:]], axis=1).astype(BF16)


def _gla_weights(w_in, gate_w, gate_b):
    d = w_in.shape[0]
    main = 2 * GLA_KEY + 2 * GLA_VAL
    w_r = jnp.zeros((d, LANES), F32).at[:, :2 * GLA_GATE_RANK].set(w_in[:, main:]).astype(BF16)
    w_g = jnp.zeros((LANES, 2 * GLA_KEY), F32)
    w_g = w_g.at[:GLA_GATE_RANK, :GLA_KEY].set(gate_w[0])
    w_g = w_g.at[GLA_GATE_RANK:2 * GLA_GATE_RANK, GLA_KEY:].set(gate_w[1]).astype(BF16)
    return w_in[:, :main].astype(BF16), w_r, w_g, gate_b.reshape(1, 2 * GLA_KEY)


def _mla_weights(w_in, w_uq, w_ukv):
    d = w_in.shape[0]
    w = jnp.zeros((d, MLA_Q_LORA + MLA_KV_LORA + LANES), F32).at[:, :w_in.shape[1]].set(w_in).astype(BF16)
    wq = w_uq.reshape(MLA_Q_LORA, MLA_HEADS, MLA_NOPE + MLA_ROPE)
    wq = jnp.pad(wq, ((0, 0), (0, 0), (0, MLA_QK_PAD - MLA_NOPE - MLA_ROPE)))
    wq = wq.reshape(MLA_Q_LORA, MLA_HEADS * MLA_QK_PAD).astype(BF16)
    wkv = w_ukv.reshape(MLA_KV_LORA, MLA_HEADS, MLA_NOPE + MLA_V)
    wkn = wkv[:, :, :MLA_NOPE].reshape(MLA_KV_LORA, MLA_HEADS * MLA_NOPE).astype(BF16)
    wv = wkv[:, :, MLA_NOPE:].reshape(MLA_KV_LORA, MLA_HEADS * MLA_V).astype(BF16)
    return w, wq, wkn, wv


def kernel(x, c, ctx, c_ctx, ada_w, ada_b, norm_g, router_w, router_b, moe_w_up, moe_w_down, shared_w_up,
           shared_w_down, diff_w_in, diff_lam, diff_norm_g, diff_w_out, gla_w_in, gla_gate_w, gla_gate_b,
           gla_norm_g, gla_w_out, mla_w_in, mla_q_norm_g, mla_w_uq, mla_kv_norm_g, mla_w_ukv, mla_w_out,
           final_norm_g):
    b, s, d = x.shape
    cl = ctx.shape[1]
    assert b + 1 <= COND_ROWS and d == D_MODEL
    depth = ada_w.shape[0]
    ctx_row = b

    cond = jnp.zeros((COND_ROWS, d), F32).at[:b].set(c).at[ctx_row].set(c_ctx)
    mods = _ada_params(cond, ada_w, ada_b).reshape(depth, COND_ROWS, N_MOD, d)

    tables = _rope_tables(s)
    rw_hi, rw_lo = _split_bf16(router_w.T)
    rb = router_b.reshape(N_EXPERTS, 1)
    w_up_all = moe_w_up.astype(BF16)
    w_dn_all = moe_w_down.astype(BF16)
    final_g = final_norm_g.reshape(1, d)

    tm_lat = 512
    n_ctx = b * cl
    tm_ctx = 512 if n_ctx % 512 == 0 else cl
    x_lat = x
    x_ctx = ctx.reshape(1, n_ctx, d)

    for i in range(depth):
        need_ctx = i < depth - 1
        mod = mods[i]
        g1 = norm_g[i, 0].reshape(1, d)
        g2 = norm_g[i, 1].reshape(1, d)
        kind, j = i % N_MIXERS, i // N_MIXERS
        if kind == 0:
            lam_init = 0.8 - 0.6 * math.exp(-0.3 * i)
            w = _diff_weights(diff_w_in[j])
            w_out = diff_w_out[j].astype(BF16)
            extras = (diff_lam[j], diff_norm_g[j].reshape(1, 2 * DIFF_HEAD_DIM))
            q_l, k_l, v_l = _diff_proj(x_lat, mod, None, g1, w, tables, tm_lat)
            q_c, k_c, v_c = (t.reshape(b, cl, -1) for t in _diff_proj(x_ctx, mod, ctx_row, g1, w, None, tm_ctx))
            o_lat = _flash(q_l, k_l, v_l, (k_c, v_c), extras, heads=DIFF_HEADS, dq=2 * DIFF_HEAD_DIM,
                           tq=512, tk=1024, lam_init=lam_init, hb=DIFF_HEADS)
            mix_lat = ("plain", o_lat, w_out)
            if need_ctx:
                o_ctx = _flash(q_c, k_c, v_c, None, extras, heads=DIFF_HEADS, dq=2 * DIFF_HEAD_DIM,
                               tq=cl, tk=cl, lam_init=lam_init, hb=DIFF_HEADS)
                mix_ctx = ("plain", o_ctx, w_out)
        elif kind == 1:
            w, w_r, w_g, b_g = _gla_weights(gla_w_in[j], gla_gate_w[j], gla_gate_b[j])
            w_out = gla_w_out[j].astype(BF16)
            ng = gla_norm_g[j].reshape(1, GLA_DV)
            q_l, k_l, v_l, og_l, gt_l = _gla_proj(x_lat, mod, None, g1, w, w_r, w_g, b_g, tm_lat)
            q_c, k_c, v_c, og_c, gt_c = (t.reshape(b, cl, -1)
                                         for t in _gla_proj(x_ctx, mod, ctx_row, g1, w, w_r, w_g, b_g, tm_ctx))
            s0 = jnp.zeros((b, 2, GLA_HEADS, GLA_DV, GLA_DK), F32)
            o_c, s_c = _gla_scan(q_c, k_c, v_c, gt_c, s0, cl)
            o_l, _ = _gla_scan(q_l, k_l, v_l, gt_l, s_c, 256)
            mix_lat = ("gla", o_l, og_l, ng, w_out)
            if need_ctx:
                mix_ctx = ("gla", o_c, og_c, ng, w_out)
        else:
            w, wq, wkn, wv = _mla_weights(mla_w_in[j], mla_w_uq[j], mla_w_ukv[j])
            w_out = mla_w_out[j].astype(BF16)
            qg = mla_q_norm_g[j].reshape(1, MLA_Q_LORA)
            kvg = mla_kv_norm_g[j].reshape(1, MLA_KV_LORA)
            q_l, k_l, v_l = _mla_proj(x_lat, mod, None, g1, w, qg, wq, kvg, wkn, wv, tables, tm_lat)
            q_c, k_c, v_c = (t.reshape(b, cl, -1)
                             for t in _mla_proj(x_ctx, mod, ctx_row, g1, w, qg, wq, kvg, wkn, wv, None, tm_ctx))
            o_lat = _flash(q_l, k_l, v_l, (k_c, v_c), None, heads=MLA_HEADS, dq=MLA_QK_PAD, tq=512, tk=1024,
                           hb=MLA_HEADS)
            mix_lat = ("plain", o_lat, w_out)
            if need_ctx:
                o_ctx = _flash(q_c, k_c, v_c, None, None, heads=MLA_HEADS, dq=MLA_QK_PAD, tq=cl, tk=cl,
                               hb=MLA_HEADS)
                mix_ctx = ("plain", o_ctx, w_out)

        sh_up = shared_w_up[i].astype(BF16)
        sh_dn = shared_w_down[i].astype(BF16)
        last = i == depth - 1
        x_lat = _moe(mix_lat, x_lat, mod, None, g2, rw_hi, rw_lo, rb, sh_up, sh_dn, w_up_all, w_dn_all, i, final_g, last,
                     tm_lat)
        if need_ctx:
            x_ctx = _moe(mix_ctx, x_ctx, mod, ctx_row, g2, rw_hi, rw_lo, rb, sh_up, sh_dn, w_up_all, w_dn_all, i, final_g,
                         False, tm_ctx)
    return x_lat
```

```python
import functools
import math

import jax
import jax.numpy as jnp
from jax import lax
from jax.experimental import pallas as pl
from jax.experimental.pallas import tpu as pltpu

F32 = jnp.float32
BF16 = jnp.bfloat16

D_MODEL = 1024
DEPTH = 4
GRID_W = 64
N_MIXERS = 3
NORM_EPS = 1e-6
ROPE_BASE = 10000.0
N_MOD = 6

DIFF_HEADS = 8
DIFF_HEAD_DIM = 64
DIFF_QK = 2 * DIFF_HEADS * DIFF_HEAD_DIM
DIFF_V = 2 * DIFF_HEADS * DIFF_HEAD_DIM

GLA_HEADS = 4
GLA_KEY = D_MODEL // 2
GLA_VAL = D_MODEL
GLA_DK = GLA_KEY // GLA_HEADS
GLA_DV = GLA_VAL // GLA_HEADS
GLA_GATE_RANK = 16
GLA_GATE_NORM = 16.0
GLA_CHUNK = 64

MLA_HEADS = 8
MLA_Q_LORA = 384
MLA_KV_LORA = 256
MLA_NOPE = 128
MLA_ROPE = 64
MLA_V = 128
MLA_SCALE = (MLA_NOPE + MLA_ROPE) ** -0.5
MLA_QK_PAD = 256

N_EXPERTS = 16
N_GROUPS = 4
EXPERTS_PER_GROUP = N_EXPERTS // N_GROUPS
EXPERT_FF = 256
SHARED_FF = 256

LANES = 128
MOE_CHUNK_SMALL = 128
MOE_CHUNK_BIG = 208
MOE_ALIGN = 16
COND_ROWS = 16
NEG_BIG = -1e30
LOG2E = math.log2(math.e)
VMEM_LIMIT = 56 * 1024 * 1024

NT_DIMS = (((1,), (1,)), ((), ()))
TN_DIMS = (((0,), (0,)), ((), ()))


def _dot(a, b):
    return jnp.dot(a, b, preferred_element_type=F32)


def _dot_nt(a, b):
    return lax.dot_general(a, b, NT_DIMS, preferred_element_type=F32)


def _dot_tn(a, b):
    return lax.dot_general(a, b, TN_DIMS, preferred_element_type=F32)


def _split_bf16(a):
    hi = a.astype(BF16)
    lo = (a - hi.astype(F32)).astype(BF16)
    return hi, lo


def _silu(a):
    return a * jax.nn.sigmoid(a)


def _params(*sem):
    return pltpu.CompilerParams(dimension_semantics=sem, vmem_limit_bytes=VMEM_LIMIT)


def _const_spec(shape, single=False):
    nd = len(shape)
    if single:
        return pl.BlockSpec(shape, lambda *_: (0,) * nd, pipeline_mode=pl.Buffered(1))
    return pl.BlockSpec(shape, lambda *_: (0,) * nd)


def _ada_kernel(cond_ref, w_ref, b_ref, o_ref):
    a_hi, a_lo = _split_bf16(_silu(cond_ref[...]))
    w_hi, w_lo = _split_bf16(w_ref[0])
    o_ref[0] = _dot(a_hi, w_hi) + _dot(a_lo, w_hi) + _dot(a_hi, w_lo) + b_ref[0]


def _ada_params(cond, ada_w, ada_b):
    depth, d, n = ada_w.shape
    tn = 1024
    return pl.pallas_call(
        _ada_kernel,
        grid=(depth, n // tn),
        in_specs=[
            pl.BlockSpec((COND_ROWS, d), lambda i, j: (0, 0)),
            pl.BlockSpec((1, d, tn), lambda i, j: (i, 0, j)),
            pl.BlockSpec((1, 1, tn), lambda i, j: (i, 0, j)),
        ],
        out_specs=pl.BlockSpec((1, COND_ROWS, tn), lambda i, j: (i, 0, j)),
        out_shape=jax.ShapeDtypeStruct((depth, COND_ROWS, n), F32),
        compiler_params=_params("parallel", "parallel"),
        name="ada_params",
    )(cond, ada_w, ada_b.reshape(depth, 1, n))


def _norm_mod(x, g, shift, scale):
    ms = jnp.mean(x * x, axis=-1, keepdims=True)
    return (x * lax.rsqrt(ms + NORM_EPS) * g) * (1.0 + scale) + shift


def _rms(x, g):
    ms = jnp.mean(x * x, axis=-1, keepdims=True)
    return x * lax.rsqrt(ms + NORM_EPS) * g


def _rope_chunk(c, cos, sa, sb):
    return c * cos + pltpu.roll(c, LANES - 16, 1) * sa + pltpu.roll(c, 16, 1) * sb


def _rope_tables(seq):
    t = jnp.arange(seq)
    pos_row = (t // GRID_W).astype(F32)
    pos_col = (t % GRID_W).astype(F32)
    inv = ROPE_BASE ** (-jnp.arange(0, 32, 2, dtype=F32) / 32)
    lane = jnp.arange(LANES)
    d = lane % 64
    r = d % 32
    first = (r < 16)[None, :]
    pos = jnp.where((d // 32)[None, :] == 0, pos_row[:, None], pos_col[:, None])
    ang = pos * inv[r % 16][None, :]
    cos, sin = jnp.cos(ang), jnp.sin(ang)
    return cos, jnp.where(first, -sin, 0.0), jnp.where(first, 0.0, sin)


def _token_specs(tm, d, mod_row):
    x_spec = pl.BlockSpec((1, tm, d), lambda b, t: (b, t, 0))
    if mod_row is None:
        mod_spec = pl.BlockSpec((1, N_MOD, d), lambda b, t: (b, 0, 0))
    else:
        mod_spec = pl.BlockSpec((1, N_MOD, d), lambda b, t: (mod_row, 0, 0))
    return x_spec, mod_spec


def _rope_specs(tm):
    return [pl.BlockSpec((tm, LANES), lambda b, t: (t, 0))] * 3


def _out_spec(tm, n):
    return pl.BlockSpec((1, tm, n), lambda b, t: (b, t, 0))


def _diff_proj_kernel(*refs, rope):
    if rope:
        x_ref, mod_ref, g_ref, w_ref, cos_ref, sa_ref, sb_ref, q_ref, k_ref, v_ref = refs
        cos, sa, sb = cos_ref[...], sa_ref[...], sb_ref[...]
    else:
        x_ref, mod_ref, g_ref, w_ref, q_ref, k_ref, v_ref = refs
    h = _norm_mod(x_ref[0], g_ref[...], mod_ref[0, 0:1, :], mod_ref[0, 1:2, :]).astype(BF16)
    for idx, o_ref in enumerate((q_ref, k_ref)):
        for c in range(DIFF_QK // 256):
            y = _dot(h, w_ref[:, idx * DIFF_QK + c * 256: idx * DIFF_QK + (c + 1) * 256])
            for s in range(2):
                ys = y[:, s * LANES:(s + 1) * LANES]
                if rope:
                    ys = _rope_chunk(ys, cos, sa, sb)
                if idx == 0:
                    ys = ys * (DIFF_HEAD_DIM ** -0.5 * LOG2E)
                o_ref[0, :, c * 256 + s * LANES: c * 256 + (s + 1) * LANES] = ys.astype(BF16)
    for c in range(DIFF_V // 256):
        y = _dot(h, w_ref[:, 2 * DIFF_QK + c * 256: 2 * DIFF_QK + (c + 1) * 256])
        v_ref[0, :, c * 256:(c + 1) * 256] = y.astype(BF16)


def _diff_proj(x, mod, mod_row, norm_g, w, tables, tm):
    bx, s, d = x.shape
    rope = tables is not None
    x_spec, mod_spec = _token_specs(tm, d, mod_row)
    in_specs = [x_spec, mod_spec, _const_spec((1, d)), _const_spec(w.shape)]
    args = [x, mod, norm_g, w]
    if rope:
        in_specs += _rope_specs(tm)
        args += list(tables)
    out = jax.ShapeDtypeStruct((bx, s, DIFF_QK), BF16)
    return pl.pallas_call(
        functools.partial(_diff_proj_kernel, rope=rope),
        grid=(bx, s // tm),
        in_specs=in_specs,
        out_specs=[_out_spec(tm, DIFF_QK)] * 3,
        out_shape=[out, out, out],
        compiler_params=_params("parallel", "parallel"),
        name="diff_proj",
    )(*args)


def _flash_kernel(*refs, n_maps, has_prefix, lam_init, hb, dq):
    refs = list(refs)
    q_ref, k_ref, v_ref = refs[:3]
    pos = 3
    if has_prefix:
        kc_ref, vc_ref = refs[pos:pos + 2]
        pos += 2
    if n_maps == 2:
        lam_ref, ng_ref = refs[pos:pos + 2]
        pos += 2
    o_ref = refs[pos]
    scratch = refs[pos + 1:]
    if n_maps == 2:
        qm_ref, m_ref, acc_ref = scratch
    else:
        m_ref, acc_ref = scratch
    kv = pl.program_id(3)

    def step(hd, mi, k_src, v_src):
        ci = hd * n_maps + mi
        q = qm_ref[ci] if n_maps == 2 else q_ref[0, :, hd * dq:(hd + 1) * dq]
        k = k_src[0, :, hd * dq:(hd + 1) * dq]
        v = v_src[0, :, hd * LANES:(hd + 1) * LANES]
        s = _dot_nt(q, k)
        m_prev = m_ref[ci]
        m_new = jnp.maximum(m_prev, jnp.max(s, axis=1, keepdims=True))
        alpha = jnp.exp2(m_prev - m_new)
        p = jnp.exp2(s - jnp.tile(m_new, (1, s.shape[1] // LANES)))
        v1 = jnp.concatenate([v, jnp.ones_like(v)], axis=1)
        acc_ref[ci] = acc_ref[ci] * jnp.tile(alpha, (1, 2)) + _dot(p.astype(BF16), v1)
        m_ref[ci] = m_new

    def all_steps(k_src, v_src):
        for hd in range(hb):
            for mi in range(n_maps):
                step(hd, mi, k_src, v_src)

    @pl.when(kv == 0)
    def _init():
        m_ref[...] = jnp.full(m_ref.shape, NEG_BIG, F32)
        acc_ref[...] = jnp.zeros(acc_ref.shape, F32)
        if n_maps == 2:
            for hd in range(hb):
                q = q_ref[0, :, hd * dq:(hd + 1) * dq]
                lane = lax.broadcasted_iota(jnp.int32, q.shape, 1)
                qm_ref[2 * hd] = jnp.where(lane < DIFF_HEAD_DIM, q, jnp.zeros_like(q))
                qm_ref[2 * hd + 1] = jnp.where(lane >= DIFF_HEAD_DIM, q, jnp.zeros_like(q))
        if has_prefix:
            all_steps(kc_ref, vc_ref)

    all_steps(k_ref, v_ref)

    @pl.when(kv == pl.num_programs(3) - 1)
    def _finish():
        if n_maps == 2:
            lf = lam_ref[...]
            l1 = jnp.sum(lf[0:1] * lf[1:2], axis=1, keepdims=True)
            l2 = jnp.sum(lf[2:3] * lf[3:4], axis=1, keepdims=True)
            lam = jnp.exp(l1) - jnp.exp(l2) + lam_init
        for hd in range(hb):
            a0 = acc_ref[hd * n_maps]
            o = a0[:, :LANES] / a0[:, LANES:]
            if n_maps == 2:
                a1 = acc_ref[hd * n_maps + 1]
                o = o - lam * (a1[:, :LANES] / a1[:, LANES:])
                o = _rms(o, ng_ref[...]) * (1.0 - lam_init)
            o_ref[0, :, hd * LANES:(hd + 1) * LANES] = o.astype(o_ref.dtype)


def _flash(q, k, v, prefix, extras, *, heads, dq, tq, tk, lam_init=0.0, hb=1):
    b, s, _ = q.shape
    n_maps = 2 if extras is not None else 1
    in_specs = [
        pl.BlockSpec((1, tq, hb * dq), lambda bi, h, qi, kv: (bi, qi, h)),
        pl.BlockSpec((1, tk, hb * dq), lambda bi, h, qi, kv: (bi, kv, h)),
        pl.BlockSpec((1, tk, hb * LANES), lambda bi, h, qi, kv: (bi, kv, h)),
    ]
    args = [q, k, v]
    if prefix is not None:
        cl = prefix[0].shape[1]
        in_specs += [
            pl.BlockSpec((1, cl, hb * dq), lambda bi, h, qi, kv: (bi, 0, h)),
            pl.BlockSpec((1, cl, hb * LANES), lambda bi, h, qi, kv: (bi, 0, h)),
        ]
        args += list(prefix)
    scratch = []
    if n_maps == 2:
        in_specs += [_const_spec(extras[0].shape), _const_spec(extras[1].shape)]
        args += list(extras)
        scratch.append(pltpu.VMEM((2 * hb, tq, dq), BF16))
    scratch += [pltpu.VMEM((hb * n_maps, tq, LANES), F32), pltpu.VMEM((hb * n_maps, tq, 2 * LANES), F32)]
    return pl.pallas_call(
        functools.partial(_flash_kernel, n_maps=n_maps, has_prefix=prefix is not None, lam_init=lam_init,
                          hb=hb, dq=dq),
        grid=(b, heads // hb, s // tq, k.shape[1] // tk),
        in_specs=in_specs,
        out_specs=pl.BlockSpec((1, tq, hb * LANES), lambda bi, h, qi, kv: (bi, qi, h)),
        out_shape=jax.ShapeDtypeStruct((b, s, heads * LANES), BF16),
        scratch_shapes=scratch,
        compiler_params=_params("parallel", "parallel", "parallel", "arbitrary"),
        name="flash_diff" if n_maps == 2 else "flash_mla",
    )(*args)


def _gla_proj_kernel(x_ref, mod_ref, g_ref, w_ref, wr_ref, wg_ref, bg_ref, q_ref, k_ref, v_ref, og_ref, gate_ref):
    h = _norm_mod(x_ref[0], g_ref[...], mod_ref[0, 0:1, :], mod_ref[0, 1:2, :]).astype(BF16)
    col = 0
    for o_ref, width, scale in ((q_ref, GLA_KEY, GLA_DK ** -0.5), (k_ref, GLA_KEY, None),
                                (v_ref, GLA_VAL, None), (og_ref, GLA_VAL, None)):
        for c in range(width // 256):
            y = _dot(h, w_ref[:, col + c * 256: col + (c + 1) * 256])
            if scale is not None:
                y = y * scale
            o_ref[0, :, c * 256:(c + 1) * 256] = y.astype(BF16)
        col += width
    r_hi, r_lo = _split_bf16(_dot(h, wr_ref[...]))
    for c in range(2 * GLA_KEY // 256):
        wg = wg_ref[:, c * 256:(c + 1) * 256]
        z = _dot(r_hi, wg) + _dot(r_lo, wg) + bg_ref[:, c * 256:(c + 1) * 256]
        log_sig = jnp.minimum(z, 0.0) - jnp.log(1.0 + jnp.exp(-jnp.abs(z)))
        gate_ref[0, :, c * 256:(c + 1) * 256] = log_sig / GLA_GATE_NORM


def _gla_proj(x, mod, mod_row, norm_g, w, wr, wg, bg, tm):
    bx, s, d = x.shape
    x_spec, mod_spec = _token_specs(tm, d, mod_row)
    return pl.pallas_call(
        _gla_proj_kernel,
        grid=(bx, s // tm),
        in_specs=[x_spec, mod_spec, _const_spec((1, d)), _const_spec(w.shape), _const_spec(wr.shape),
                  _const_spec(wg.shape), _const_spec(bg.shape)],
        out_specs=[_out_spec(tm, GLA_KEY), _out_spec(tm, GLA_KEY), _out_spec(tm, GLA_VAL),
                   _out_spec(tm, GLA_VAL), _out_spec(tm, 2 * GLA_KEY)],
        out_shape=[jax.ShapeDtypeStruct((bx, s, GLA_KEY), BF16), jax.ShapeDtypeStruct((bx, s, GLA_KEY), BF16),
                   jax.ShapeDtypeStruct((bx, s, GLA_VAL), BF16), jax.ShapeDtypeStruct((bx, s, GLA_VAL), BF16),
                   jax.ShapeDtypeStruct((bx, s, 2 * GLA_KEY), F32)],
        compiler_params=_params("parallel", "parallel"),
        name="gla_proj",
    )(x, mod, norm_g, w, wr, wg, bg)


def _gla_scan_kernel(q_ref, k_ref, v_ref, g_ref, s0_ref, o_ref, sfin_ref, st_ref):
    direction = pl.program_id(1)
    i = pl.program_id(2)
    n_chunks = q_ref.shape[1] // GLA_CHUNK

    @pl.when(i == 0)
    def _load_state():
        st_ref[...] = s0_ref[0, 0]

    def run(backward):
        tb = q_ref.shape[1]
        row = lax.broadcasted_iota(jnp.int32, (tb, tb), 0)
        colm = lax.broadcasted_iota(jnp.int32, (tb, tb), 1)
        same = (row // GLA_CHUNK) == (colm // GLA_CHUNK)
        seen = (colm >= row) if backward else (colm <= row)
        allowed = seen[:GLA_CHUNK, :GLA_CHUNK]
        seen_m = jnp.where(same & seen, 1.0, 0.0).astype(BF16)
        rest_m = jnp.where(same & jnp.logical_not(seen), 1.0, 0.0).astype(BF16)
        g_hi, g_lo = _split_bf16(g_ref[0])
        cum = _dot(seen_m, g_hi) + _dot(seen_m, g_lo)
        rest = _dot(rest_m, g_hi) + _dot(rest_m, g_lo)
        q = q_ref[0].astype(F32)
        k = k_ref[0].astype(F32)
        q_dec = (q * jnp.exp(cum)).astype(BF16)
        k_inv = (k * jnp.exp(-cum)).astype(BF16)
        k_end = (k * jnp.exp(rest)).astype(BF16)
        decay = jnp.exp(cum + rest)
        order = range(n_chunks - 1, -1, -1) if backward else range(n_chunks)
        for c in order:
            sl = slice(c * GLA_CHUNK, (c + 1) * GLA_CHUNK)
            for h in range(GLA_HEADS):
                ksl = slice(h * GLA_DK, (h + 1) * GLA_DK)
                vsl = slice(h * GLA_DV, (h + 1) * GLA_DV)
                v = v_ref[0, sl, vsl]
                qd = q_dec[sl, ksl]
                a = jnp.where(allowed, _dot_nt(qd, k_inv[sl, ksl]), 0.0)
                st = st_ref[h]
                o_ref[0, 0, sl, vsl] = (_dot(a.astype(BF16), v) + _dot_nt(qd, st.astype(BF16))).astype(o_ref.dtype)
                st_ref[h] = st * decay[c * GLA_CHUNK:c * GLA_CHUNK + 1, ksl] + _dot_tn(v, k_end[sl, ksl])

    pl.when(direction == 0)(lambda: run(False))
    pl.when(direction == 1)(lambda: run(True))

    @pl.when(i == pl.num_programs(2) - 1)
    def _store_state():
        sfin_ref[0, 0] = st_ref[...]


def _gla_scan(q, k, v, g, s0, tb):
    b, s, _ = q.shape
    nb = s // tb

    def blk(d, i):
        return i + d * (nb - 1 - 2 * i)

    st_spec = pl.BlockSpec((1, 1, GLA_HEADS, GLA_DV, GLA_DK), lambda bi, d, i: (bi, d, 0, 0, 0))
    return pl.pallas_call(
        _gla_scan_kernel,
        grid=(b, 2, nb),
        in_specs=[
            pl.BlockSpec((1, tb, GLA_KEY), lambda bi, d, i: (bi, blk(d, i), 0)),
            pl.BlockSpec((1, tb, GLA_KEY), lambda bi, d, i: (bi, blk(d, i), 0)),
            pl.BlockSpec((1, tb, GLA_VAL), lambda bi, d, i: (bi, blk(d, i), 0)),
            pl.BlockSpec((1, tb, GLA_KEY), lambda bi, d, i: (bi, blk(d, i), d)),
            st_spec,
        ],
        out_specs=[pl.BlockSpec((1, 1, tb, GLA_VAL), lambda bi, d, i: (d, bi, blk(d, i), 0)), st_spec],
        out_shape=[jax.ShapeDtypeStruct((2, b, s, GLA_VAL), BF16),
                   jax.ShapeDtypeStruct((b, 2, GLA_HEADS, GLA_DV, GLA_DK), F32)],
        scratch_shapes=[pltpu.VMEM((GLA_HEADS, GLA_DV, GLA_DK), F32)],
        compiler_params=_params("parallel", "parallel", "arbitrary"),
        name="gla_scan",
    )(q, k, v, g, s0)


def _mla_proj_kernel(*refs, rope):
    if rope:
        (x_ref, mod_ref, g_ref, w_ref, qg_ref, wq_ref, kvg_ref, wkn_ref, wv_ref,
         cos_ref, sa_ref, sb_ref, q_ref, k_ref, v_ref) = refs
        cos, sa, sb = cos_ref[...], sa_ref[...], sb_ref[...]
    else:
        x_ref, mod_ref, g_ref, w_ref, qg_ref, wq_ref, kvg_ref, wkn_ref, wv_ref, q_ref, k_ref, v_ref = refs
    h = _norm_mod(x_ref[0], g_ref[...], mod_ref[0, 0:1, :], mod_ref[0, 1:2, :]).astype(BF16)
    y = _dot(h, w_ref[...])
    c_q = _rms(y[:, :MLA_Q_LORA], qg_ref[...]).astype(BF16)
    c_kv = _rms(y[:, MLA_Q_LORA:MLA_Q_LORA + MLA_KV_LORA], kvg_ref[...]).astype(BF16)
    k_rope = y[:, MLA_Q_LORA + MLA_KV_LORA:]
    if rope:
        k_rope = _rope_chunk(k_rope, cos, sa, sb)
    k_rope = k_rope.astype(BF16)
    for hd in range(MLA_HEADS):
        base = hd * MLA_QK_PAD
        qh = _dot(c_q, wq_ref[:, base: base + MLA_QK_PAD])
        q_rope = qh[:, LANES:]
        if rope:
            q_rope = _rope_chunk(q_rope, cos, sa, sb)
        q_ref[0, :, base: base + LANES] = (qh[:, :LANES] * (MLA_SCALE * LOG2E)).astype(BF16)
        q_ref[0, :, base + LANES: base + MLA_QK_PAD] = (q_rope * (MLA_SCALE * LOG2E)).astype(BF16)
        k_ref[0, :, base + LANES: base + MLA_QK_PAD] = k_rope
    for c in range(MLA_HEADS * MLA_NOPE // 256):
        kn = _dot(c_kv, wkn_ref[:, c * 256:(c + 1) * 256]).astype(BF16)
        for s in range(2):
            hd = 2 * c + s
            k_ref[0, :, hd * MLA_QK_PAD: hd * MLA_QK_PAD + LANES] = kn[:, s * LANES:(s + 1) * LANES]
        v_ref[0, :, c * 256:(c + 1) * 256] = _dot(c_kv, wv_ref[:, c * 256:(c + 1) * 256]).astype(BF16)


def _mla_proj(x, mod, mod_row, norm_g, w, qg, wq, kvg, wkn, wv, tables, tm):
    bx, s, d = x.shape
    rope = tables is not None
    x_spec, mod_spec = _token_specs(tm, d, mod_row)
    consts = [norm_g, w, qg, wq, kvg, wkn, wv]
    in_specs = [x_spec, mod_spec] + [_const_spec(a.shape) for a in consts]
    args = [x, mod] + consts
    if rope:
        in_specs += _rope_specs(tm)
        args += list(tables)
    qk_w = MLA_HEADS * MLA_QK_PAD
    return pl.pallas_call(
        functools.partial(_mla_proj_kernel, rope=rope),
        grid=(bx, s // tm),
        in_specs=in_specs,
        out_specs=[_out_spec(tm, qk_w), _out_spec(tm, qk_w), _out_spec(tm, MLA_HEADS * MLA_V)],
        out_shape=[jax.ShapeDtypeStruct((bx, s, qk_w), BF16), jax.ShapeDtypeStruct((bx, s, qk_w), BF16),
                   jax.ShapeDtypeStruct((bx, s, MLA_HEADS * MLA_V), BF16)],
        compiler_params=_params("parallel", "parallel"),
        name="mla_proj",
    )(*args)


def _top2_sum(a, b, c, d):
    hi1, lo1 = jnp.maximum(a, b), jnp.minimum(a, b)
    hi2, lo2 = jnp.maximum(c, d), jnp.minimum(c, d)
    return jnp.maximum(hi1, hi2) + jnp.maximum(jnp.minimum(hi1, hi2), jnp.maximum(lo1, lo2))


def _route_rows(scores, sel):
    eg = EXPERTS_PER_GROUP
    gs = [_top2_sum(*sel[g * eg:(g + 1) * eg]) for g in range(N_GROUPS)]
    best = jnp.maximum(jnp.maximum(gs[0], gs[1]), jnp.maximum(gs[2], gs[3]))
    gidx = jnp.where(gs[0] >= best, 0, jnp.where(gs[1] >= best, 1, jnp.where(gs[2] >= best, 2, 3)))

    def pick(rows, j):
        out = rows[(N_GROUPS - 1) * eg + j]
        for g in range(N_GROUPS - 2, -1, -1):
            out = jnp.where(gidx == g, rows[g * eg + j], out)
        return out

    loc_sel = [pick(sel, j) for j in range(eg)]
    loc_sc = [pick(scores, j) for j in range(eg)]
    weights = []
    for j in range(eg):
        rank = jnp.zeros_like(loc_sel[j])
        for i in range(eg):
            if i == j:
                continue
            beats = (loc_sel[i] >= loc_sel[j]) if i < j else (loc_sel[i] > loc_sel[j])
            rank = rank + jnp.where(beats, 1.0, 0.0)
        weights.append(jnp.where(rank < 1.5, loc_sc[j], 0.0))
    den = weights[0] + weights[1] + weights[2] + weights[3]
    weights = [w / den for w in weights]
    return [jnp.where(gidx == e // eg, weights[e % eg], 0.0) for e in range(N_EXPERTS)], gidx


def _mixer_output(pre, refs):
    if pre == "gla":
        of_ref, ob_ref, og_ref, ng_ref, w_ref = refs
        o = of_ref[0, 0].astype(F32) + ob_ref[0, 0].astype(F32)
        parts = []
        for h in range(GLA_HEADS):
            sl = slice(h * GLA_DV, (h + 1) * GLA_DV)
            parts.append((_rms(o[:, sl], ng_ref[...]) * _silu(og_ref[0, :, sl].astype(F32))).astype(BF16))
        return _dot(jnp.concatenate(parts, axis=1), w_ref[...])
    o_ref, w_ref = refs
    return _dot(o_ref[0], w_ref[...])


def _moe_kernel(*refs, pre, final_norm):
    n_pre = 5 if pre == "gla" else 2
    (x_ref, mod_ref, g_ref, rw_ref, rb_ref, shu_ref, shd_ref, tri_ref, wup_ref, wdn_ref, fg_ref,
     out_ref, perm_ref, hs_ref, cs_ref, ys_ref) = refs[n_pre:]
    x = x_ref[0] + mod_ref[0, 2:3, :] * _mixer_output(pre, refs[:n_pre])
    tm = x.shape[0]
    n_rows = hs_ref.shape[0]
    h, h_lo = _split_bf16(_norm_mod(x, g_ref[...], mod_ref[0, 3:4, :], mod_ref[0, 4:5, :]))
    both = _dot_nt(rw_ref[...], h)
    logits = both[:N_EXPERTS] + both[N_EXPERTS:] + _dot_nt(rw_ref[:N_EXPERTS, :], h_lo)
    scores = jax.nn.sigmoid(logits)
    sel = scores + rb_ref[...]
    rows, gidx = _route_rows([scores[e:e + 1, :] for e in range(N_EXPERTS)],
                             [sel[e:e + 1, :] for e in range(N_EXPERTS)])
    comb = jnp.concatenate(rows + [jnp.zeros((LANES - N_EXPERTS, tm), F32)], axis=0).T

    member = [jnp.where(gidx == g, 1.0, 0.0) for g in range(N_GROUPS)]
    member_m = jnp.concatenate(member + [jnp.zeros((8 - N_GROUPS, tm), F32)], axis=0).astype(BF16)
    rank = _dot(member_m, tri_ref[...])
    dest = jnp.zeros((1, tm), F32)
    start = jnp.zeros((1, 1), F32)
    seg_start, seg_count = [], []
    for g in range(N_GROUPS):
        count = jnp.sum(member[g], axis=1, keepdims=True)
        dest = dest + member[g] * (start + rank[g:g + 1] - 1.0)
        seg_start.append(start[0, 0].astype(jnp.int32))
        seg_count.append(count[0, 0].astype(jnp.int32))
        start = jnp.floor((start + count + (MOE_ALIGN - 1)) * (1.0 / MOE_ALIGN)) * MOE_ALIGN
    row_id = lax.broadcasted_iota(jnp.int32, (n_rows, tm), 0)
    perm = jnp.where(row_id == dest.astype(jnp.int32), 1.0, 0.0).astype(BF16)
    perm_ref[...] = perm
    hs_ref[...] = _dot(perm, h).astype(BF16)
    c_hi, c_lo = _split_bf16(comb)
    cs_ref[...] = _dot(perm, c_hi) + _dot(perm, c_lo)
    ys_ref[...] = jnp.zeros(ys_ref.shape, F32)
    a = _dot(h, shu_ref[:, :SHARED_FF])
    u = _dot(h, shu_ref[:, SHARED_FF:])
    out_ref[0] = x + mod_ref[0, 5:6, :] * _dot((_silu(a) * u).astype(BF16), shd_ref[...])

    def run_group(grp, size, n_passes):
        def one_pass(i, carry):
            r0 = pl.multiple_of(seg_start[grp] + i * size, MOE_ALIGN)
            hc = hs_ref[pl.ds(r0, size), :]
            cc = cs_ref[pl.ds(r0, size), :]
            acts = []
            for j in range(EXPERTS_PER_GROUP):
                e = grp * EXPERTS_PER_GROUP + j
                a = _dot(hc, wup_ref[0, e, :, :EXPERT_FF])
                u = _dot(hc, wup_ref[0, e, :, EXPERT_FF:])
                scale = jnp.broadcast_to(cc[:, e:e + 1], a.shape)
                acts.append((_silu(a) * u * scale).astype(BF16))
            w_dn = wdn_ref[0, grp * EXPERTS_PER_GROUP:(grp + 1) * EXPERTS_PER_GROUP]
            y = _dot(jnp.concatenate(acts, axis=1), w_dn.reshape(EXPERTS_PER_GROUP * EXPERT_FF, -1))
            ys_ref[pl.ds(r0, size), :] = ys_ref[pl.ds(r0, size), :] + y
            return carry

        lax.fori_loop(0, n_passes, one_pass, 0)

    for grp in range(N_GROUPS):
        n = seg_count[grp]
        fits = n <= MOE_CHUNK_SMALL
        run_group(grp, MOE_CHUNK_SMALL, jnp.where(fits, jnp.minimum(n, 1), 0))
        run_group(grp, MOE_CHUNK_BIG, jnp.where(fits, 0, (n + (MOE_CHUNK_BIG - 1)) // MOE_CHUNK_BIG))

    out = out_ref[0] + mod_ref[0, 5:6, :] * _dot_tn(perm_ref[...], ys_ref[...].astype(BF16))
    if final_norm:
        out = _rms(out, fg_ref[...])
    out_ref[0] = out


def _moe(mixer, x, mod, mod_row, norm_g, rw, rb, sh_up, sh_dn, w_up, w_dn, layer, final_g, final_norm, tm):
    bx, s, d = x.shape
    x_spec, mod_spec = _token_specs(tm, d, mod_row)
    pre = mixer[0]
    if pre == "gla":
        o2, og, ng, w_out = mixer[1:]
        o2 = o2.reshape(2, bx, s, GLA_VAL)
        pre_specs = [pl.BlockSpec((1, 1, tm, GLA_VAL), lambda b, t: (0, b, t, 0)),
                     pl.BlockSpec((1, 1, tm, GLA_VAL), lambda b, t: (1, b, t, 0)),
                     _out_spec(tm, GLA_VAL), _const_spec(ng.shape, True), _const_spec(w_out.shape, True)]
        pre_args = [o2, o2, og.reshape(bx, s, GLA_VAL), ng, w_out]
    else:
        o, w_out = mixer[1:]
        o = o.reshape(bx, s, -1)
        pre_specs = [_out_spec(tm, o.shape[2]), _const_spec(w_out.shape, True)]
        pre_args = [o, w_out]
    tri = (jnp.arange(tm)[:, None] <= jnp.arange(tm)[None, :]).astype(BF16)
    consts = [norm_g, rw, rb, sh_up, sh_dn, tri]
    n_rows = -(-(tm + (N_GROUPS - 1) * (MOE_ALIGN - 1) + MOE_CHUNK_BIG) // 256) * 256
    resident = pl.Buffered(1)
    in_specs = (pre_specs + [x_spec, mod_spec] + [_const_spec(a.shape, True) for a in consts] + [
        pl.BlockSpec((1,) + w_up.shape[1:], lambda b, t: (layer, 0, 0, 0), pipeline_mode=resident),
        pl.BlockSpec((1,) + w_dn.shape[1:], lambda b, t: (layer, 0, 0, 0), pipeline_mode=resident),
        _const_spec(final_g.shape, True)])
    return pl.pallas_call(
        functools.partial(_moe_kernel, pre=pre, final_norm=final_norm),
        grid=(bx, s // tm),
        in_specs=in_specs,
        out_specs=x_spec,
        out_shape=jax.ShapeDtypeStruct(x.shape, F32),
        scratch_shapes=[pltpu.VMEM((n_rows, tm), BF16), pltpu.VMEM((n_rows, d), BF16),
                        pltpu.VMEM((n_rows, LANES), F32), pltpu.VMEM((n_rows, d), F32)],
        compiler_params=_params("parallel", "parallel"),
        name="moe",
    )(*pre_args, x, mod, *consts, w_up, w_dn, final_g)


def _diff_weights(w_in):
    d = w_in.shape[0]

    def regroup(w):
        return w.reshape(d, 2, DIFF_HEADS, DIFF_HEAD_DIM).transpose(0, 2, 1, 3).reshape(d, DIFF_QK)

    return jnp.concatenate([regroup(w_in[:, :DIFF_QK]), regroup(w_in[:, DIFF_QK:2 * DIFF_QK]),
                            w_in[:, 2 * DIFF_QK:]], axis=1).astype(BF16)


def _gla_weights(w_in, gate_w, gate_b):
    d = w_in.shape[0]
    main = 2 * GLA_KEY + 2 * GLA_VAL
    w_r = jnp.zeros((d, LANES), F32).at[:, :2 * GLA_GATE_RANK].set(w_in[:, main:]).astype(BF16)
    w_g = jnp.zeros((LANES, 2 * GLA_KEY), F32)
    w_g = w_g.at[:GLA_GATE_RANK, :GLA_KEY].set(gate_w[0])
    w_g = w_g.at[GLA_GATE_RANK:2 * GLA_GATE_RANK, GLA_KEY:].set(gate_w[1]).astype(BF16)
    return w_in[:, :main].astype(BF16), w_r, w_g, gate_b.reshape(1, 2 * GLA_KEY)


def _mla_weights(w_in, w_uq, w_ukv):
    d = w_in.shape[0]
    w = jnp.zeros((d, MLA_Q_LORA + MLA_KV_LORA + LANES), F32).at[:, :w_in.shape[1]].set(w_in).astype(BF16)
    wq = w_uq.reshape(MLA_Q_LORA, MLA_HEADS, MLA_NOPE + MLA_ROPE)
    wq = jnp.pad(wq, ((0, 0), (0, 0), (0, MLA_QK_PAD - MLA_NOPE - MLA_ROPE)))
    wq = wq.reshape(MLA_Q_LORA, MLA_HEADS * MLA_QK_PAD).astype(BF16)
    wkv = w_ukv.reshape(MLA_KV_LORA, MLA_HEADS, MLA_NOPE + MLA_V)
    wkn = wkv[:, :, :MLA_NOPE].reshape(MLA_KV_LORA, MLA_HEADS * MLA_NOPE).astype(BF16)
    wv = wkv[:, :, MLA_NOPE:].reshape(MLA_KV_LORA, MLA_HEADS * MLA_V).astype(BF16)
    return w, wq, wkn, wv


def kernel(x, c, ctx, c_ctx, ada_w, ada_b, norm_g, router_w, router_b, moe_w_up, moe_w_down, shared_w_up,
           shared_w_down, diff_w_in, diff_lam, diff_norm_g, diff_w_out, gla_w_in, gla_gate_w, gla_gate_b,
           gla_norm_g, gla_w_out, mla_w_in, mla_q_norm_g, mla_w_uq, mla_kv_norm_g, mla_w_ukv, mla_w_out,
           final_norm_g):
    b, s, d = x.shape
    cl = ctx.shape[1]
    assert b + 1 <= COND_ROWS and d == D_MODEL
    depth = ada_w.shape[0]
    ctx_row = b

    cond = jnp.zeros((COND_ROWS, d), F32).at[:b].set(c).at[ctx_row].set(c_ctx)
    mods = _ada_params(cond, ada_w, ada_b).reshape(depth, COND_ROWS, N_MOD, d)

    tables = _rope_tables(s)
    rw = jnp.concatenate(_split_bf16(router_w.T), axis=0)
    rb = router_b.reshape(N_EXPERTS, 1)
    w_up_all = moe_w_up.astype(BF16)
    w_dn_all = moe_w_down.astype(BF16)
    final_g = final_norm_g.reshape(1, d)

    tm_lat = 512
    n_ctx = b * cl
    tm_ctx = 512 if n_ctx % 512 == 0 else cl
    x_lat = x
    x_ctx = ctx.reshape(1, n_ctx, d)

    for i in range(depth):
        need_ctx = i < depth - 1
        mod = mods[i]
        g1 = norm_g[i, 0].reshape(1, d)
        g2 = norm_g[i, 1].reshape(1, d)
        kind, j = i % N_MIXERS, i // N_MIXERS
        if kind == 0:
            lam_init = 0.8 - 0.6 * math.exp(-0.3 * i)
            w = _diff_weights(diff_w_in[j])
            w_out = diff_w_out[j].astype(BF16)
            extras = (diff_lam[j], diff_norm_g[j].reshape(1, 2 * DIFF_HEAD_DIM))
            q_l, k_l, v_l = _diff_proj(x_lat, mod, None, g1, w, tables, tm_lat)
            q_c, k_c, v_c = (t.reshape(b, cl, -1) for t in _diff_proj(x_ctx, mod, ctx_row, g1, w, None, tm_ctx))
            o_lat = _flash(q_l, k_l, v_l, (k_c, v_c), extras, heads=DIFF_HEADS, dq=2 * DIFF_HEAD_DIM,
                           tq=512, tk=1024, lam_init=lam_init, hb=DIFF_HEADS)
            mix_lat = ("plain", o_lat, w_out)
            if need_ctx:
                o_ctx = _flash(q_c, k_c, v_c, None, extras, heads=DIFF_HEADS, dq=2 * DIFF_HEAD_DIM,
                               tq=cl, tk=cl, lam_init=lam_init, hb=DIFF_HEADS)
                mix_ctx = ("plain", o_ctx, w_out)
        elif kind == 1:
            w, w_r, w_g, b_g = _gla_weights(gla_w_in[j], gla_gate_w[j], gla_gate_b[j])
            w_out = gla_w_out[j].astype(BF16)
            ng = gla_norm_g[j].reshape(1, GLA_DV)
            q_l, k_l, v_l, og_l, gt_l = _gla_proj(x_lat, mod, None, g1, w, w_r, w_g, b_g, tm_lat)
            q_c, k_c, v_c, og_c, gt_c = (t.reshape(b, cl, -1)
                                         for t in _gla_proj(x_ctx, mod, ctx_row, g1, w, w_r, w_g, b_g, tm_ctx))
            s0 = jnp.zeros((b, 2, GLA_HEADS, GLA_DV, GLA_DK), F32)
            o_c, s_c = _gla_scan(q_c, k_c, v_c, gt_c, s0, cl)
            o_l, _ = _gla_scan(q_l, k_l, v_l, gt_l, s_c, 256)
            mix_lat = ("gla", o_l, og_l, ng, w_out)
            if need_ctx:
                mix_ctx = ("gla", o_c, og_c, ng, w_out)
        else:
            w, wq, wkn, wv = _mla_weights(mla_w_in[j], mla_w_uq[j], mla_w_ukv[j])
            w_out = mla_w_out[j].astype(BF16)
            qg = mla_q_norm_g[j].reshape(1, MLA_Q_LORA)
            kvg = mla_kv_norm_g[j].reshape(1, MLA_KV_LORA)
            q_l, k_l, v_l = _mla_proj(x_lat, mod, None, g1, w, qg, wq, kvg, wkn, wv, tables, tm_lat)
            q_c, k_c, v_c = (t.reshape(b, cl, -1)
                             for t in _mla_proj(x_ctx, mod, ctx_row, g1, w, qg, wq, kvg, wkn, wv, None, tm_ctx))
            o_lat = _flash(q_l, k_l, v_l, (k_c, v_c), None, heads=MLA_HEADS, dq=MLA_QK_PAD, tq=512, tk=1024,
                           hb=MLA_HEADS)
            mix_lat = ("plain", o_lat, w_out)
            if need_ctx:
                o_ctx = _flash(q_c, k_c, v_c, None, None, heads=MLA_HEADS, dq=MLA_QK_PAD, tq=cl, tk=cl,
                               hb=MLA_HEADS)
                mix_ctx = ("plain", o_ctx, w_out)

        sh_up = shared_w_up[i].astype(BF16)
        sh_dn = shared_w_down[i].astype(BF16)
        last = i == depth - 1
        x_lat = _moe(mix_lat, x_lat, mod, None, g2, rw, rb, sh_up, sh_dn, w_up_all, w_dn_all, i, final_g, last,
                     tm_lat)
        if need_ctx:
            x_ctx = _moe(mix_ctx, x_ctx, mod, ctx_row, g2, rw, rb, sh_up, sh_dn, w_up_all, w_dn_all, i, final_g,
                         False, tm_ctx)
    return x_lat
```

```python
import functools
import math

import jax
import jax.numpy as jnp
from jax import lax
from jax.experimental import pallas as pl
from jax.experimental.pallas import tpu as pltpu

F32 = jnp.float32
BF16 = jnp.bfloat16

D_MODEL = 1024
DEPTH = 4
GRID_W = 64
N_MIXERS = 3
NORM_EPS = 1e-6
ROPE_BASE = 10000.0
N_MOD = 6

DIFF_HEADS = 8
DIFF_HEAD_DIM = 64
DIFF_QK = 2 * DIFF_HEADS * DIFF_HEAD_DIM
DIFF_V = 2 * DIFF_HEADS * DIFF_HEAD_DIM

GLA_HEADS = 4
GLA_KEY = D_MODEL // 2
GLA_VAL = D_MODEL
GLA_DK = GLA_KEY // GLA_HEADS
GLA_DV = GLA_VAL // GLA_HEADS
GLA_GATE_RANK = 16
GLA_GATE_NORM = 16.0
GLA_CHUNK = 64

MLA_HEADS = 8
MLA_Q_LORA = 384
MLA_KV_LORA = 256
MLA_NOPE = 128
MLA_ROPE = 64
MLA_V = 128
MLA_SCALE = (MLA_NOPE + MLA_ROPE) ** -0.5
MLA_QK_PAD = 256

N_EXPERTS = 16
N_GROUPS = 4
EXPERTS_PER_GROUP = N_EXPERTS // N_GROUPS
EXPERT_FF = 256
SHARED_FF = 256

LANES = 128
MOE_CHUNK_SMALL = 128
MOE_CHUNK_BIG = 208
MOE_ALIGN = 16
LOOKAHEAD = 2
ONES_ROWS = 16
COND_ROWS = 16
NEG_BIG = -1e30
LOG2E = math.log2(math.e)
VMEM_LIMIT = 56 * 1024 * 1024

NT_DIMS = (((1,), (1,)), ((), ()))
TN_DIMS = (((0,), (0,)), ((), ()))


def _dot(a, b):
    return jnp.dot(a, b, preferred_element_type=F32)


def _dot_nt(a, b):
    return lax.dot_general(a, b, NT_DIMS, preferred_element_type=F32)


def _dot_tn(a, b):
    return lax.dot_general(a, b, TN_DIMS, preferred_element_type=F32)


def _split_bf16(a):
    hi = a.astype(BF16)
    lo = (a - hi.astype(F32)).astype(BF16)
    return hi, lo


def _silu(a):
    return a * jax.nn.sigmoid(a)


def _params(*sem):
    return pltpu.CompilerParams(dimension_semantics=sem, vmem_limit_bytes=VMEM_LIMIT)


def _const_spec(shape, single=False):
    nd = len(shape)
    if single:
        return pl.BlockSpec(shape, lambda *_: (0,) * nd, pipeline_mode=pl.Buffered(1))
    return pl.BlockSpec(shape, lambda *_: (0,) * nd)


def _ada_kernel(cond_ref, w_ref, b_ref, o_ref):
    a_hi, a_lo = _split_bf16(_silu(cond_ref[...]))
    w_hi, w_lo = _split_bf16(w_ref[0])
    o_ref[0] = _dot(a_hi, w_hi) + _dot(a_lo, w_hi) + _dot(a_hi, w_lo) + b_ref[0]


def _ada_params(cond, ada_w, ada_b):
    depth, d, n = ada_w.shape
    tn = 1024
    return pl.pallas_call(
        _ada_kernel,
        grid=(depth, n // tn),
        in_specs=[
            pl.BlockSpec((COND_ROWS, d), lambda i, j: (0, 0)),
            pl.BlockSpec((1, d, tn), lambda i, j: (i, 0, j)),
            pl.BlockSpec((1, 1, tn), lambda i, j: (i, 0, j)),
        ],
        out_specs=pl.BlockSpec((1, COND_ROWS, tn), lambda i, j: (i, 0, j)),
        out_shape=jax.ShapeDtypeStruct((depth, COND_ROWS, n), F32),
        compiler_params=_params("parallel", "parallel"),
        name="ada_params",
    )(cond, ada_w, ada_b.reshape(depth, 1, n))


def _norm_mod(x, g, shift, scale):
    ms = jnp.mean(x * x, axis=-1, keepdims=True)
    return (x * lax.rsqrt(ms + NORM_EPS) * g) * (1.0 + scale) + shift


def _rms(x, g):
    ms = jnp.mean(x * x, axis=-1, keepdims=True)
    return x * lax.rsqrt(ms + NORM_EPS) * g


def _rope_chunk(c, cos, sa, sb):
    return c * cos + pltpu.roll(c, LANES - 16, 1) * sa + pltpu.roll(c, 16, 1) * sb


def _rope_tables(seq):
    t = jnp.arange(seq)
    pos_row = (t // GRID_W).astype(F32)
    pos_col = (t % GRID_W).astype(F32)
    inv = ROPE_BASE ** (-jnp.arange(0, 32, 2, dtype=F32) / 32)
    lane = jnp.arange(LANES)
    d = lane % 64
    r = d % 32
    first = (r < 16)[None, :]
    pos = jnp.where((d // 32)[None, :] == 0, pos_row[:, None], pos_col[:, None])
    ang = pos * inv[r % 16][None, :]
    cos, sin = jnp.cos(ang), jnp.sin(ang)
    return cos, jnp.where(first, -sin, 0.0), jnp.where(first, 0.0, sin)


def _token_specs(tm, d, mod_row):
    x_spec = pl.BlockSpec((1, tm, d), lambda b, t: (b, t, 0))
    if mod_row is None:
        mod_spec = pl.BlockSpec((1, N_MOD, d), lambda b, t: (b, 0, 0))
    else:
        mod_spec = pl.BlockSpec((1, N_MOD, d), lambda b, t: (mod_row, 0, 0))
    return x_spec, mod_spec


def _rope_specs(tm):
    return [pl.BlockSpec((tm, LANES), lambda b, t: (t, 0))] * 3


def _out_spec(tm, n):
    return pl.BlockSpec((1, tm, n), lambda b, t: (b, t, 0))


def _diff_proj_kernel(*refs, rope):
    if rope:
        x_ref, mod_ref, g_ref, w_ref, wvt_ref, cos_ref, sa_ref, sb_ref, q_ref, k_ref, vt_ref = refs
        cos, sa, sb = cos_ref[...], sa_ref[...], sb_ref[...]
    else:
        x_ref, mod_ref, g_ref, w_ref, wvt_ref, q_ref, k_ref, vt_ref = refs
    h = _norm_mod(x_ref[0], g_ref[...], mod_ref[0, 0:1, :], mod_ref[0, 1:2, :]).astype(BF16)
    for idx, o_ref in enumerate((q_ref, k_ref)):
        for c in range(DIFF_QK // 256):
            y = _dot(h, w_ref[:, idx * DIFF_QK + c * 256: idx * DIFF_QK + (c + 1) * 256])
            for s in range(2):
                ys = y[:, s * LANES:(s + 1) * LANES]
                if rope:
                    ys = _rope_chunk(ys, cos, sa, sb)
                if idx == 0:
                    ys = ys * (DIFF_HEAD_DIM ** -0.5 * LOG2E)
                o_ref[0, :, c * 256 + s * LANES: c * 256 + (s + 1) * LANES] = ys.astype(BF16)
    vt_ref[0] = _dot_nt(wvt_ref[...], h).astype(BF16)


def _vt_spec(tm, n):
    return pl.BlockSpec((1, n, tm), lambda b, t: (b, 0, t))


def _diff_proj(x, mod, mod_row, norm_g, w, wvt, tables, tm):
    bx, s, d = x.shape
    rope = tables is not None
    x_spec, mod_spec = _token_specs(tm, d, mod_row)
    in_specs = [x_spec, mod_spec, _const_spec((1, d)), _const_spec(w.shape), _const_spec(wvt.shape)]
    args = [x, mod, norm_g, w, wvt]
    if rope:
        in_specs += _rope_specs(tm)
        args += list(tables)
    out = jax.ShapeDtypeStruct((bx, s, DIFF_QK), BF16)
    return pl.pallas_call(
        functools.partial(_diff_proj_kernel, rope=rope),
        grid=(bx, s // tm),
        in_specs=in_specs,
        out_specs=[_out_spec(tm, DIFF_QK), _out_spec(tm, DIFF_QK), _vt_spec(tm, DIFF_V)],
        out_shape=[out, out, jax.ShapeDtypeStruct((bx, DIFF_V, s), BF16)],
        compiler_params=_params("parallel", "parallel"),
        name="diff_proj",
    )(*args)


def _col_max(s, rows=64):
    parts = [jnp.max(s[i:i + rows], axis=0, keepdims=True) for i in range(0, s.shape[0], rows)]
    while len(parts) > 1:
        parts = [jnp.maximum(a, b) for a, b in zip(parts[::2], parts[1::2])]
    return parts[0]


def _flash_kernel(*refs, n_maps, has_prefix, lam_init, hb, dq):
    refs = list(refs)
    q_ref, k_ref, vt_ref = refs[:3]
    pos = 3
    if has_prefix:
        kc_ref, vct_ref = refs[pos:pos + 2]
        pos += 2
    if n_maps == 2:
        lam_ref, ng_ref = refs[pos:pos + 2]
        pos += 2
    o_ref = refs[pos]
    scratch = refs[pos + 1:]
    if n_maps == 2:
        qm_ref, m_ref, acc_ref = scratch
    else:
        m_ref, acc_ref = scratch
    kv = pl.program_id(3)

    def scores(ci, k_src):
        hd = ci // n_maps
        q = qm_ref[ci] if n_maps == 2 else q_ref[0, :, hd * dq:(hd + 1) * dq]
        return _dot_nt(k_src[0, :, hd * dq:(hd + 1) * dq], q)

    def update(ci, s, vt_src):
        hd = ci // n_maps
        vt = vt_src[0, hd * LANES:(hd + 1) * LANES, :]
        m_prev = m_ref[ci]
        m_new = jnp.maximum(m_prev, _col_max(s))
        alpha = jnp.exp2(m_prev - m_new)
        p = jnp.exp2(s - m_new).astype(BF16)
        v1 = jnp.concatenate([vt, jnp.ones((ONES_ROWS, vt.shape[1]), BF16)], axis=0)
        acc_ref[ci] = acc_ref[ci] * alpha + _dot(v1, p)
        m_ref[ci] = m_new

    def all_steps(k_src, vt_src):
        n_chains = hb * n_maps
        pending = [scores(ci, k_src) for ci in range(min(LOOKAHEAD, n_chains))]
        for ci in range(n_chains):
            if ci + LOOKAHEAD < n_chains:
                pending.append(scores(ci + LOOKAHEAD, k_src))
            update(ci, pending.pop(0), vt_src)

    @pl.when(kv == 0)
    def _init():
        m_ref[...] = jnp.full(m_ref.shape, NEG_BIG, F32)
        acc_ref[...] = jnp.zeros(acc_ref.shape, F32)
        if n_maps == 2:
            for hd in range(hb):
                q = q_ref[0, :, hd * dq:(hd + 1) * dq]
                lane = lax.broadcasted_iota(jnp.int32, q.shape, 1)
                qm_ref[2 * hd] = jnp.where(lane < DIFF_HEAD_DIM, q, jnp.zeros_like(q))
                qm_ref[2 * hd + 1] = jnp.where(lane >= DIFF_HEAD_DIM, q, jnp.zeros_like(q))
        if has_prefix:
            all_steps(kc_ref, vct_ref)

    all_steps(k_ref, vt_ref)

    @pl.when(kv == pl.num_programs(3) - 1)
    def _finish():
        if n_maps == 2:
            lf = lam_ref[...]
            l1 = jnp.sum(lf[0:1] * lf[1:2], axis=1, keepdims=True)
            l2 = jnp.sum(lf[2:3] * lf[3:4], axis=1, keepdims=True)
            lam = jnp.exp(l1) - jnp.exp(l2) + lam_init
        for hd in range(hb):
            a0 = acc_ref[hd * n_maps]
            o = a0[:LANES] / a0[LANES:LANES + 1]
            if n_maps == 2:
                a1 = acc_ref[hd * n_maps + 1]
                o = o - lam * (a1[:LANES] / a1[LANES:LANES + 1])
                ms = jnp.mean(o * o, axis=0, keepdims=True)
                o = o * lax.rsqrt(ms + NORM_EPS) * ng_ref[...] * (1.0 - lam_init)
            o_ref[0, :, hd * LANES:(hd + 1) * LANES] = o.T.astype(o_ref.dtype)


def _flash(q, k, vt, prefix, extras, *, heads, dq, tq, tk, lam_init=0.0, hb=1):
    b, s, _ = q.shape
    n_maps = 2 if extras is not None else 1
    in_specs = [
        pl.BlockSpec((1, tq, hb * dq), lambda bi, h, qi, kv: (bi, qi, h)),
        pl.BlockSpec((1, tk, hb * dq), lambda bi, h, qi, kv: (bi, kv, h)),
        pl.BlockSpec((1, hb * LANES, tk), lambda bi, h, qi, kv: (bi, h, kv)),
    ]
    args = [q, k, vt]
    if prefix is not None:
        cl = prefix[0].shape[1]
        in_specs += [
            pl.BlockSpec((1, cl, hb * dq), lambda bi, h, qi, kv: (bi, 0, h)),
            pl.BlockSpec((1, hb * LANES, cl), lambda bi, h, qi, kv: (bi, h, 0)),
        ]
        args += list(prefix)
    scratch = []
    if n_maps == 2:
        in_specs += [_const_spec(extras[0].shape), _const_spec(extras[1].shape)]
        args += list(extras)
        scratch.append(pltpu.VMEM((2 * hb, tq, dq), BF16))
    scratch += [pltpu.VMEM((hb * n_maps, 1, tq), F32), pltpu.VMEM((hb * n_maps, LANES + ONES_ROWS, tq), F32)]
    return pl.pallas_call(
        functools.partial(_flash_kernel, n_maps=n_maps, has_prefix=prefix is not None, lam_init=lam_init,
                          hb=hb, dq=dq),
        grid=(b, heads // hb, s // tq, k.shape[1] // tk),
        in_specs=in_specs,
        out_specs=pl.BlockSpec((1, tq, hb * LANES), lambda bi, h, qi, kv: (bi, qi, h)),
        out_shape=jax.ShapeDtypeStruct((b, s, heads * LANES), BF16),
        scratch_shapes=scratch,
        compiler_params=_params("parallel", "parallel", "parallel", "arbitrary"),
        name="flash_diff" if n_maps == 2 else "flash_mla",
    )(*args)


def _gla_proj_kernel(x_ref, mod_ref, g_ref, w_ref, wr_ref, wg_ref, bg_ref, q_ref, k_ref, v_ref, og_ref, gate_ref):
    h = _norm_mod(x_ref[0], g_ref[...], mod_ref[0, 0:1, :], mod_ref[0, 1:2, :]).astype(BF16)
    col = 0
    for o_ref, width, scale in ((q_ref, GLA_KEY, GLA_DK ** -0.5), (k_ref, GLA_KEY, None),
                                (v_ref, GLA_VAL, None), (og_ref, GLA_VAL, None)):
        for c in range(width // 256):
            y = _dot(h, w_ref[:, col + c * 256: col + (c + 1) * 256])
            if scale is not None:
                y = y * scale
            o_ref[0, :, c * 256:(c + 1) * 256] = y.astype(BF16)
        col += width
    r_hi, r_lo = _split_bf16(_dot(h, wr_ref[...]))
    for c in range(2 * GLA_KEY // 256):
        wg = wg_ref[:, c * 256:(c + 1) * 256]
        z = _dot(r_hi, wg) + _dot(r_lo, wg) + bg_ref[:, c * 256:(c + 1) * 256]
        log_sig = jnp.minimum(z, 0.0) - jnp.log(1.0 + jnp.exp(-jnp.abs(z)))
        gate_ref[0, :, c * 256:(c + 1) * 256] = log_sig / GLA_GATE_NORM


def _gla_proj(x, mod, mod_row, norm_g, w, wr, wg, bg, tm):
    bx, s, d = x.shape
    x_spec, mod_spec = _token_specs(tm, d, mod_row)
    return pl.pallas_call(
        _gla_proj_kernel,
        grid=(bx, s // tm),
        in_specs=[x_spec, mod_spec, _const_spec((1, d)), _const_spec(w.shape), _const_spec(wr.shape),
                  _const_spec(wg.shape), _const_spec(bg.shape)],
        out_specs=[_out_spec(tm, GLA_KEY), _out_spec(tm, GLA_KEY), _out_spec(tm, GLA_VAL),
                   _out_spec(tm, GLA_VAL), _out_spec(tm, 2 * GLA_KEY)],
        out_shape=[jax.ShapeDtypeStruct((bx, s, GLA_KEY), BF16), jax.ShapeDtypeStruct((bx, s, GLA_KEY), BF16),
                   jax.ShapeDtypeStruct((bx, s, GLA_VAL), BF16), jax.ShapeDtypeStruct((bx, s, GLA_VAL), BF16),
                   jax.ShapeDtypeStruct((bx, s, 2 * GLA_KEY), F32)],
        compiler_params=_params("parallel", "parallel"),
        name="gla_proj",
    )(x, mod, norm_g, w, wr, wg, bg)


def _gla_scan_kernel(q_ref, k_ref, v_ref, g_ref, s0_ref, o_ref, sfin_ref, st_ref):
    direction = pl.program_id(1)
    i = pl.program_id(2)
    n_chunks = q_ref.shape[1] // GLA_CHUNK

    @pl.when(i == 0)
    def _load_state():
        st_ref[...] = s0_ref[0, 0]

    def run(backward):
        tb = q_ref.shape[1]
        row = lax.broadcasted_iota(jnp.int32, (tb, tb), 0)
        colm = lax.broadcasted_iota(jnp.int32, (tb, tb), 1)
        same = (row // GLA_CHUNK) == (colm // GLA_CHUNK)
        seen = (colm >= row) if backward else (colm <= row)
        allowed = seen[:GLA_CHUNK, :GLA_CHUNK]
        seen_m = jnp.where(same & seen, 1.0, 0.0).astype(BF16)
        rest_m = jnp.where(same & jnp.logical_not(seen), 1.0, 0.0).astype(BF16)
        g_hi, g_lo = _split_bf16(g_ref[0])
        cum = _dot(seen_m, g_hi) + _dot(seen_m, g_lo)
        rest = _dot(rest_m, g_hi) + _dot(rest_m, g_lo)
        q = q_ref[0].astype(F32)
        k = k_ref[0].astype(F32)
        q_dec = (q * jnp.exp(cum)).astype(BF16)
        k_inv = (k * jnp.exp(-cum)).astype(BF16)
        k_end = (k * jnp.exp(rest)).astype(BF16)
        decay = jnp.exp(cum + rest)
        order = range(n_chunks - 1, -1, -1) if backward else range(n_chunks)
        for c in order:
            sl = slice(c * GLA_CHUNK, (c + 1) * GLA_CHUNK)
            for h in range(GLA_HEADS):
                ksl = slice(h * GLA_DK, (h + 1) * GLA_DK)
                vsl = slice(h * GLA_DV, (h + 1) * GLA_DV)
                v = v_ref[0, sl, vsl]
                qd = q_dec[sl, ksl]
                a = jnp.where(allowed, _dot_nt(qd, k_inv[sl, ksl]), 0.0)
                st = st_ref[h]
                o_ref[0, 0, sl, vsl] = (_dot(a.astype(BF16), v) + _dot_nt(qd, st.astype(BF16))).astype(o_ref.dtype)
                st_ref[h] = st * decay[c * GLA_CHUNK:c * GLA_CHUNK + 1, ksl] + _dot_tn(v, k_end[sl, ksl])

    pl.when(direction == 0)(lambda: run(False))
    pl.when(direction == 1)(lambda: run(True))

    @pl.when(i == pl.num_programs(2) - 1)
    def _store_state():
        sfin_ref[0, 0] = st_ref[...]


def _gla_scan(q, k, v, g, s0, tb):
    b, s, _ = q.shape
    nb = s // tb

    def blk(d, i):
        return i + d * (nb - 1 - 2 * i)

    st_spec = pl.BlockSpec((1, 1, GLA_HEADS, GLA_DV, GLA_DK), lambda bi, d, i: (bi, d, 0, 0, 0))
    return pl.pallas_call(
        _gla_scan_kernel,
        grid=(b, 2, nb),
        in_specs=[
            pl.BlockSpec((1, tb, GLA_KEY), lambda bi, d, i: (bi, blk(d, i), 0)),
            pl.BlockSpec((1, tb, GLA_KEY), lambda bi, d, i: (bi, blk(d, i), 0)),
            pl.BlockSpec((1, tb, GLA_VAL), lambda bi, d, i: (bi, blk(d, i), 0)),
            pl.BlockSpec((1, tb, GLA_KEY), lambda bi, d, i: (bi, blk(d, i), d)),
            st_spec,
        ],
        out_specs=[pl.BlockSpec((1, 1, tb, GLA_VAL), lambda bi, d, i: (d, bi, blk(d, i), 0)), st_spec],
        out_shape=[jax.ShapeDtypeStruct((2, b, s, GLA_VAL), BF16),
                   jax.ShapeDtypeStruct((b, 2, GLA_HEADS, GLA_DV, GLA_DK), F32)],
        scratch_shapes=[pltpu.VMEM((GLA_HEADS, GLA_DV, GLA_DK), F32)],
        compiler_params=_params("parallel", "parallel", "arbitrary"),
        name="gla_scan",
    )(q, k, v, g, s0)


def _mla_proj_kernel(*refs, rope):
    if rope:
        (x_ref, mod_ref, g_ref, w_ref, qg_ref, wq_ref, kvg_ref, wkn_ref, wvt_ref,
         cos_ref, sa_ref, sb_ref, q_ref, k_ref, vt_ref) = refs
        cos, sa, sb = cos_ref[...], sa_ref[...], sb_ref[...]
    else:
        x_ref, mod_ref, g_ref, w_ref, qg_ref, wq_ref, kvg_ref, wkn_ref, wvt_ref, q_ref, k_ref, vt_ref = refs
    h = _norm_mod(x_ref[0], g_ref[...], mod_ref[0, 0:1, :], mod_ref[0, 1:2, :]).astype(BF16)
    y = _dot(h, w_ref[...])
    c_q = _rms(y[:, :MLA_Q_LORA], qg_ref[...]).astype(BF16)
    c_kv = _rms(y[:, MLA_Q_LORA:MLA_Q_LORA + MLA_KV_LORA], kvg_ref[...]).astype(BF16)
    k_rope = y[:, MLA_Q_LORA + MLA_KV_LORA:]
    if rope:
        k_rope = _rope_chunk(k_rope, cos, sa, sb)
    k_rope = k_rope.astype(BF16)
    for hd in range(MLA_HEADS):
        base = hd * MLA_QK_PAD
        qh = _dot(c_q, wq_ref[:, base: base + MLA_QK_PAD])
        q_rope = qh[:, LANES:]
        if rope:
            q_rope = _rope_chunk(q_rope, cos, sa, sb)
        q_ref[0, :, base: base + LANES] = (qh[:, :LANES] * (MLA_SCALE * LOG2E)).astype(BF16)
        q_ref[0, :, base + LANES: base + MLA_QK_PAD] = (q_rope * (MLA_SCALE * LOG2E)).astype(BF16)
        k_ref[0, :, base + LANES: base + MLA_QK_PAD] = k_rope
    vt_ref[0] = _dot_nt(wvt_ref[...], c_kv).astype(BF16)
    for c in range(MLA_HEADS * MLA_NOPE // 256):
        kn = _dot(c_kv, wkn_ref[:, c * 256:(c + 1) * 256]).astype(BF16)
        for s in range(2):
            hd = 2 * c + s
            k_ref[0, :, hd * MLA_QK_PAD: hd * MLA_QK_PAD + LANES] = kn[:, s * LANES:(s + 1) * LANES]


def _mla_proj(x, mod, mod_row, norm_g, w, qg, wq, kvg, wkn, wvt, tables, tm):
    bx, s, d = x.shape
    rope = tables is not None
    x_spec, mod_spec = _token_specs(tm, d, mod_row)
    consts = [norm_g, w, qg, wq, kvg, wkn, wvt]
    in_specs = [x_spec, mod_spec] + [_const_spec(a.shape) for a in consts]
    args = [x, mod] + consts
    if rope:
        in_specs += _rope_specs(tm)
        args += list(tables)
    qk_w = MLA_HEADS * MLA_QK_PAD
    return pl.pallas_call(
        functools.partial(_mla_proj_kernel, rope=rope),
        grid=(bx, s // tm),
        in_specs=in_specs,
        out_specs=[_out_spec(tm, qk_w), _out_spec(tm, qk_w), _vt_spec(tm, MLA_HEADS * MLA_V)],
        out_shape=[jax.ShapeDtypeStruct((bx, s, qk_w), BF16), jax.ShapeDtypeStruct((bx, s, qk_w), BF16),
                   jax.ShapeDtypeStruct((bx, MLA_HEADS * MLA_V, s), BF16)],
        compiler_params=_params("parallel", "parallel"),
        name="mla_proj",
    )(*args)


def _top2_sum(a, b, c, d):
    hi1, lo1 = jnp.maximum(a, b), jnp.minimum(a, b)
    hi2, lo2 = jnp.maximum(c, d), jnp.minimum(c, d)
    return jnp.maximum(hi1, hi2) + jnp.maximum(jnp.minimum(hi1, hi2), jnp.maximum(lo1, lo2))


def _route_rows(scores, sel):
    eg = EXPERTS_PER_GROUP
    gs = [_top2_sum(*sel[g * eg:(g + 1) * eg]) for g in range(N_GROUPS)]
    best = jnp.maximum(jnp.maximum(gs[0], gs[1]), jnp.maximum(gs[2], gs[3]))
    gidx = jnp.where(gs[0] >= best, 0, jnp.where(gs[1] >= best, 1, jnp.where(gs[2] >= best, 2, 3)))

    def pick(rows, j):
        out = rows[(N_GROUPS - 1) * eg + j]
        for g in range(N_GROUPS - 2, -1, -1):
            out = jnp.where(gidx == g, rows[g * eg + j], out)
        return out

    loc_sel = [pick(sel, j) for j in range(eg)]
    loc_sc = [pick(scores, j) for j in range(eg)]
    weights = []
    for j in range(eg):
        rank = jnp.zeros_like(loc_sel[j])
        for i in range(eg):
            if i == j:
                continue
            beats = (loc_sel[i] >= loc_sel[j]) if i < j else (loc_sel[i] > loc_sel[j])
            rank = rank + jnp.where(beats, 1.0, 0.0)
        weights.append(jnp.where(rank < 1.5, loc_sc[j], 0.0))
    den = weights[0] + weights[1] + weights[2] + weights[3]
    weights = [w / den for w in weights]
    return [jnp.where(gidx == e // eg, weights[e % eg], 0.0) for e in range(N_EXPERTS)], gidx


def _mixer_output(pre, refs):
    if pre == "gla":
        of_ref, ob_ref, og_ref, ng_ref, w_ref = refs
        o = of_ref[0, 0].astype(F32) + ob_ref[0, 0].astype(F32)
        parts = []
        for h in range(GLA_HEADS):
            sl = slice(h * GLA_DV, (h + 1) * GLA_DV)
            parts.append((_rms(o[:, sl], ng_ref[...]) * _silu(og_ref[0, :, sl].astype(F32))).astype(BF16))
        return _dot(jnp.concatenate(parts, axis=1), w_ref[...])
    o_ref, w_ref = refs
    return _dot(o_ref[0], w_ref[...])


def _moe_kernel(*refs, pre, final_norm):
    n_pre = 5 if pre == "gla" else 2
    (x_ref, mod_ref, g_ref, rw_ref, rb_ref, shu_ref, shd_ref, tri_ref, wup_ref, wdn_ref, fg_ref,
     out_ref, perm_ref, hs_ref, cs_ref, ys_ref) = refs[n_pre:]
    x = x_ref[0] + mod_ref[0, 2:3, :] * _mixer_output(pre, refs[:n_pre])
    tm = x.shape[0]
    n_rows = hs_ref.shape[0]
    h, h_lo = _split_bf16(_norm_mod(x, g_ref[...], mod_ref[0, 3:4, :], mod_ref[0, 4:5, :]))
    both = _dot_nt(rw_ref[...], h)
    logits = both[:N_EXPERTS] + both[N_EXPERTS:] + _dot_nt(rw_ref[:N_EXPERTS, :], h_lo)
    scores = jax.nn.sigmoid(logits)
    sel = scores + rb_ref[...]
    rows, gidx = _route_rows([scores[e:e + 1, :] for e in range(N_EXPERTS)],
                             [sel[e:e + 1, :] for e in range(N_EXPERTS)])
    comb = jnp.concatenate(rows + [jnp.zeros((LANES - N_EXPERTS, tm), F32)], axis=0).T

    member = [jnp.where(gidx == g, 1.0, 0.0) for g in range(N_GROUPS)]
    member_m = jnp.concatenate(member + [jnp.zeros((8 - N_GROUPS, tm), F32)], axis=0).astype(BF16)
    rank = _dot(member_m, tri_ref[...])
    dest = jnp.zeros((1, tm), F32)
    start = jnp.zeros((1, 1), F32)
    seg_start, seg_count = [], []
    for g in range(N_GROUPS):
        count = jnp.sum(member[g], axis=1, keepdims=True)
        dest = dest + member[g] * (start + rank[g:g + 1] - 1.0)
        seg_start.append(start[0, 0].astype(jnp.int32))
        seg_count.append(count[0, 0].astype(jnp.int32))
        start = jnp.floor((start + count + (MOE_ALIGN - 1)) * (1.0 / MOE_ALIGN)) * MOE_ALIGN
    row_id = lax.broadcasted_iota(jnp.int32, (n_rows, tm), 0)
    perm = jnp.where(row_id == dest.astype(jnp.int32), 1.0, 0.0).astype(BF16)
    perm_ref[...] = perm
    hs_ref[...] = _dot(perm, h).astype(BF16)
    c_hi, c_lo = _split_bf16(comb)
    cs_ref[...] = _dot(perm, c_hi) + _dot(perm, c_lo)
    ys_ref[...] = jnp.zeros(ys_ref.shape, F32)
    a = _dot(h, shu_ref[:, :SHARED_FF])
    u = _dot(h, shu_ref[:, SHARED_FF:])
    out_ref[0] = x + mod_ref[0, 5:6, :] * _dot((_silu(a) * u).astype(BF16), shd_ref[...])

    def run_group(grp, size, n_passes):
        def one_pass(i, carry):
            r0 = pl.multiple_of(seg_start[grp] + i * size, MOE_ALIGN)
            hc = hs_ref[pl.ds(r0, size), :]
            cc = cs_ref[pl.ds(r0, size), :]
            acts = []
            for j in range(EXPERTS_PER_GROUP):
                e = grp * EXPERTS_PER_GROUP + j
                a = _dot(hc, wup_ref[0, e, :, :EXPERT_FF])
                u = _dot(hc, wup_ref[0, e, :, EXPERT_FF:])
                scale = jnp.broadcast_to(cc[:, e:e + 1], a.shape)
                acts.append((_silu(a) * u * scale).astype(BF16))
            w_dn = wdn_ref[0, grp * EXPERTS_PER_GROUP:(grp + 1) * EXPERTS_PER_GROUP]
            y = _dot(jnp.concatenate(acts, axis=1), w_dn.reshape(EXPERTS_PER_GROUP * EXPERT_FF, -1))
            ys_ref[pl.ds(r0, size), :] = ys_ref[pl.ds(r0, size), :] + y
            return carry

        lax.fori_loop(0, n_passes, one_pass, 0)

    for grp in range(N_GROUPS):
        n = seg_count[grp]
        fits = n <= MOE_CHUNK_SMALL
        run_group(grp, MOE_CHUNK_SMALL, jnp.where(fits, jnp.minimum(n, 1), 0))
        run_group(grp, MOE_CHUNK_BIG, jnp.where(fits, 0, (n + (MOE_CHUNK_BIG - 1)) // MOE_CHUNK_BIG))

    out = out_ref[0] + mod_ref[0, 5:6, :] * _dot_tn(perm_ref[...], ys_ref[...].astype(BF16))
    if final_norm:
        out = _rms(out, fg_ref[...])
    out_ref[0] = out


def _moe(mixer, x, mod, mod_row, norm_g, rw, rb, sh_up, sh_dn, w_up, w_dn, layer, final_g, final_norm, tm):
    bx, s, d = x.shape
    x_spec, mod_spec = _token_specs(tm, d, mod_row)
    pre = mixer[0]
    if pre == "gla":
        o2, og, ng, w_out = mixer[1:]
        o2 = o2.reshape(2, bx, s, GLA_VAL)
        pre_specs = [pl.BlockSpec((1, 1, tm, GLA_VAL), lambda b, t: (0, b, t, 0)),
                     pl.BlockSpec((1, 1, tm, GLA_VAL), lambda b, t: (1, b, t, 0)),
                     _out_spec(tm, GLA_VAL), _const_spec(ng.shape, True), _const_spec(w_out.shape, True)]
        pre_args = [o2, o2, og.reshape(bx, s, GLA_VAL), ng, w_out]
    else:
        o, w_out = mixer[1:]
        o = o.reshape(bx, s, -1)
        pre_specs = [_out_spec(tm, o.shape[2]), _const_spec(w_out.shape, True)]
        pre_args = [o, w_out]
    tri = (jnp.arange(tm)[:, None] <= jnp.arange(tm)[None, :]).astype(BF16)
    consts = [norm_g, rw, rb, sh_up, sh_dn, tri]
    n_rows = -(-(tm + (N_GROUPS - 1) * (MOE_ALIGN - 1) + MOE_CHUNK_BIG) // 256) * 256
    resident = pl.Buffered(1)
    in_specs = (pre_specs + [x_spec, mod_spec] + [_const_spec(a.shape, True) for a in consts] + [
        pl.BlockSpec((1,) + w_up.shape[1:], lambda b, t: (layer, 0, 0, 0), pipeline_mode=resident),
        pl.BlockSpec((1,) + w_dn.shape[1:], lambda b, t: (layer, 0, 0, 0), pipeline_mode=resident),
        _const_spec(final_g.shape, True)])
    return pl.pallas_call(
        functools.partial(_moe_kernel, pre=pre, final_norm=final_norm),
        grid=(bx, s // tm),
        in_specs=in_specs,
        out_specs=x_spec,
        out_shape=jax.ShapeDtypeStruct(x.shape, F32),
        scratch_shapes=[pltpu.VMEM((n_rows, tm), BF16), pltpu.VMEM((n_rows, d), BF16),
                        pltpu.VMEM((n_rows, LANES), F32), pltpu.VMEM((n_rows, d), F32)],
        compiler_params=_params("parallel", "parallel"),
        name="moe",
    )(*pre_args, x, mod, *consts, w_up, w_dn, final_g)


def _diff_weights(w_in):
    d = w_in.shape[0]

    def regroup(w):
        return w.reshape(d, 2, DIFF_HEADS, DIFF_HEAD_DIM).transpose(0, 2, 1, 3).reshape(d, DIFF_QK)

    w_qk = jnp.concatenate([regroup(w_in[:, :DIFF_QK]), regroup(w_in[:, DIFF_QK:2 * DIFF_QK])], axis=1)
    return w_qk.astype(BF16), w_in[:, 2 * DIFF_QK:].T.astype(BF16)


def _gla_weights(w_in, gate_w, gate_b):
    d = w_in.shape[0]
    main = 2 * GLA_KEY + 2 * GLA_VAL
    w_r = jnp.zeros((d, LANES), F32).at[:, :2 * GLA_GATE_RANK].set(w_in[:, main:]).astype(BF16)
    w_g = jnp.zeros((LANES, 2 * GLA_KEY), F32)
    w_g = w_g.at[:GLA_GATE_RANK, :GLA_KEY].set(gate_w[0])
    w_g = w_g.at[GLA_GATE_RANK:2 * GLA_GATE_RANK, GLA_KEY:].set(gate_w[1]).astype(BF16)
    return w_in[:, :main].astype(BF16), w_r, w_g, gate_b.reshape(1, 2 * GLA_KEY)


def _mla_weights(w_in, w_uq, w_ukv):
    d = w_in.shape[0]
    w = jnp.zeros((d, MLA_Q_LORA + MLA_KV_LORA + LANES), F32).at[:, :w_in.shape[1]].set(w_in).astype(BF16)
    wq = w_uq.reshape(MLA_Q_LORA, MLA_HEADS, MLA_NOPE + MLA_ROPE)
    wq = jnp.pad(wq, ((0, 0), (0, 0), (0, MLA_QK_PAD - MLA_NOPE - MLA_ROPE)))
    wq = wq.reshape(MLA_Q_LORA, MLA_HEADS * MLA_QK_PAD).astype(BF16)
    wkv = w_ukv.reshape(MLA_KV_LORA, MLA_HEADS, MLA_NOPE + MLA_V)
    wkn = wkv[:, :, :MLA_NOPE].reshape(MLA_KV_LORA, MLA_HEADS * MLA_NOPE).astype(BF16)
    wvt = wkv[:, :, MLA_NOPE:].reshape(MLA_KV_LORA, MLA_HEADS * MLA_V).T.astype(BF16)
    return w, wq, wkn, wvt


def _ctx_views(qkv, b, cl):
    q, k, vt = qkv
    vt = vt.reshape(vt.shape[1], b, cl).transpose(1, 0, 2)
    return q.reshape(b, cl, -1), k.reshape(b, cl, -1), vt


def kernel(x, c, ctx, c_ctx, ada_w, ada_b, norm_g, router_w, router_b, moe_w_up, moe_w_down, shared_w_up,
           shared_w_down, diff_w_in, diff_lam, diff_norm_g, diff_w_out, gla_w_in, gla_gate_w, gla_gate_b,
           gla_norm_g, gla_w_out, mla_w_in, mla_q_norm_g, mla_w_uq, mla_kv_norm_g, mla_w_ukv, mla_w_out,
           final_norm_g):
    b, s, d = x.shape
    cl = ctx.shape[1]
    assert b + 1 <= COND_ROWS and d == D_MODEL
    depth = ada_w.shape[0]
    ctx_row = b

    cond = jnp.zeros((COND_ROWS, d), F32).at[:b].set(c).at[ctx_row].set(c_ctx)
    mods = _ada_params(cond, ada_w, ada_b).reshape(depth, COND_ROWS, N_MOD, d)

    tables = _rope_tables(s)
    rw = jnp.concatenate(_split_bf16(router_w.T), axis=0)
    rb = router_b.reshape(N_EXPERTS, 1)
    w_up_all = moe_w_up.astype(BF16)
    w_dn_all = moe_w_down.astype(BF16)
    final_g = final_norm_g.reshape(1, d)

    tm_lat = 512
    n_ctx = b * cl
    tm_ctx = 512 if n_ctx % 512 == 0 else cl
    x_lat = x
    x_ctx = ctx.reshape(1, n_ctx, d)

    for i in range(depth):
        need_ctx = i < depth - 1
        mod = mods[i]
        g1 = norm_g[i, 0].reshape(1, d)
        g2 = norm_g[i, 1].reshape(1, d)
        kind, j = i % N_MIXERS, i // N_MIXERS
        if kind == 0:
            lam_init = 0.8 - 0.6 * math.exp(-0.3 * i)
            w, wvt = _diff_weights(diff_w_in[j])
            w_out = diff_w_out[j].astype(BF16)
            extras = (diff_lam[j], diff_norm_g[j].reshape(2 * DIFF_HEAD_DIM, 1))
            q_l, k_l, v_l = _diff_proj(x_lat, mod, None, g1, w, wvt, tables, tm_lat)
            q_c, k_c, v_c = _ctx_views(_diff_proj(x_ctx, mod, ctx_row, g1, w, wvt, None, tm_ctx), b, cl)
            o_lat = _flash(q_l, k_l, v_l, (k_c, v_c), extras, heads=DIFF_HEADS, dq=2 * DIFF_HEAD_DIM,
                           tq=512, tk=1024, lam_init=lam_init, hb=DIFF_HEADS)
            mix_lat = ("plain", o_lat, w_out)
            if need_ctx:
                o_ctx = _flash(q_c, k_c, v_c, None, extras, heads=DIFF_HEADS, dq=2 * DIFF_HEAD_DIM,
                               tq=cl, tk=cl, lam_init=lam_init, hb=DIFF_HEADS)
                mix_ctx = ("plain", o_ctx, w_out)
        elif kind == 1:
            w, w_r, w_g, b_g = _gla_weights(gla_w_in[j], gla_gate_w[j], gla_gate_b[j])
            w_out = gla_w_out[j].astype(BF16)
            ng = gla_norm_g[j].reshape(1, GLA_DV)
            q_l, k_l, v_l, og_l, gt_l = _gla_proj(x_lat, mod, None, g1, w, w_r, w_g, b_g, tm_lat)
            q_c, k_c, v_c, og_c, gt_c = (t.reshape(b, cl, -1)
                                         for t in _gla_proj(x_ctx, mod, ctx_row, g1, w, w_r, w_g, b_g, tm_ctx))
            s0 = jnp.zeros((b, 2, GLA_HEADS, GLA_DV, GLA_DK), F32)
            o_c, s_c = _gla_scan(q_c, k_c, v_c, gt_c, s0, cl)
            o_l, _ = _gla_scan(q_l, k_l, v_l, gt_l, s_c, 256)
            mix_lat = ("gla", o_l, og_l, ng, w_out)
            if need_ctx:
                mix_ctx = ("gla", o_c, og_c, ng, w_out)
        else:
            w, wq, wkn, wvt = _mla_weights(mla_w_in[j], mla_w_uq[j], mla_w_ukv[j])
            w_out = mla_w_out[j].astype(BF16)
            qg = mla_q_norm_g[j].reshape(1, MLA_Q_LORA)
            kvg = mla_kv_norm_g[j].reshape(1, MLA_KV_LORA)
            q_l, k_l, v_l = _mla_proj(x_lat, mod, None, g1, w, qg, wq, kvg, wkn, wvt, tables, tm_lat)
            q_c, k_c, v_c = _ctx_views(_mla_proj(x_ctx, mod, ctx_row, g1, w, qg, wq, kvg, wkn, wvt, None, tm_ctx),
                                       b, cl)
            o_lat = _flash(q_l, k_l, v_l, (k_c, v_c), None, heads=MLA_HEADS, dq=MLA_QK_PAD, tq=512, tk=1024,
                           hb=MLA_HEADS)
            mix_lat = ("plain", o_lat, w_out)
            if need_ctx:
                o_ctx = _flash(q_c, k_c, v_c, None, None, heads=MLA_HEADS, dq=MLA_QK_PAD, tq=cl, tk=cl,
                               hb=MLA_HEADS)
                mix_ctx = ("plain", o_ctx, w_out)

        sh_up = shared_w_up[i].astype(BF16)
        sh_dn = shared_w_down[i].astype(BF16)
        last = i == depth - 1
        x_lat = _moe(mix_lat, x_lat, mod, None, g2, rw, rb, sh_up, sh_dn, w_up_all, w_dn_all, i, final_g, last,
                     tm_lat)
        if need_ctx:
            x_ctx = _moe(mix_ctx, x_ctx, mod, ctx_row, g2, rw, rb, sh_up, sh_dn, w_up_all, w_dn_all, i, final_g,
                         False, tm_ctx)
    return x_lat
```

```python
import functools
import math

import jax
import jax.numpy as jnp
from jax import lax
from jax.experimental import pallas as pl
from jax.experimental.pallas import tpu as pltpu

F32 = jnp.float32
BF16 = jnp.bfloat16

D_MODEL = 1024
DEPTH = 4
GRID_W = 64
N_MIXERS = 3
NORM_EPS = 1e-6
ROPE_BASE = 10000.0
N_MOD = 6

DIFF_HEADS = 8
DIFF_HEAD_DIM = 64
DIFF_QK = 2 * DIFF_HEADS * DIFF_HEAD_DIM
DIFF_V = 2 * DIFF_HEADS * DIFF_HEAD_DIM

GLA_HEADS = 4
GLA_KEY = D_MODEL // 2
GLA_VAL = D_MODEL
GLA_DK = GLA_KEY // GLA_HEADS
GLA_DV = GLA_VAL // GLA_HEADS
GLA_GATE_RANK = 16
GLA_GATE_NORM = 16.0
GLA_CHUNK = 64

MLA_HEADS = 8
MLA_Q_LORA = 384
MLA_KV_LORA = 256
MLA_NOPE = 128
MLA_ROPE = 64
MLA_V = 128
MLA_SCALE = (MLA_NOPE + MLA_ROPE) ** -0.5
MLA_QK_PAD = 256

N_EXPERTS = 16
N_GROUPS = 4
EXPERTS_PER_GROUP = N_EXPERTS // N_GROUPS
EXPERT_FF = 256
SHARED_FF = 256

LANES = 128
MOE_CHUNK_SMALL = 128
MOE_CHUNK_BIG = 208
MOE_ALIGN = 16
LOOKAHEAD = 2
ONES_ROWS = 16
COND_ROWS = 16
NEG_BIG = -1e30
LOG2E = math.log2(math.e)
VMEM_LIMIT = 56 * 1024 * 1024

NT_DIMS = (((1,), (1,)), ((), ()))
TN_DIMS = (((0,), (0,)), ((), ()))


def _dot(a, b):
    return jnp.dot(a, b, preferred_element_type=F32)


def _dot_nt(a, b):
    return lax.dot_general(a, b, NT_DIMS, preferred_element_type=F32)


def _dot_tn(a, b):
    return lax.dot_general(a, b, TN_DIMS, preferred_element_type=F32)


def _split_bf16(a):
    hi = a.astype(BF16)
    lo = (a - hi.astype(F32)).astype(BF16)
    return hi, lo


def _silu(a):
    return a * jax.nn.sigmoid(a)


def _params(*sem):
    return pltpu.CompilerParams(dimension_semantics=sem, vmem_limit_bytes=VMEM_LIMIT)


def _const_spec(shape, single=False):
    nd = len(shape)
    if single:
        return pl.BlockSpec(shape, lambda *_: (0,) * nd, pipeline_mode=pl.Buffered(1))
    return pl.BlockSpec(shape, lambda *_: (0,) * nd)


def _ada_kernel(cond_ref, w_ref, b_ref, o_ref):
    a_hi, a_lo = _split_bf16(_silu(cond_ref[...]))
    w_hi, w_lo = _split_bf16(w_ref[0])
    o_ref[0] = _dot(a_hi, w_hi) + _dot(a_lo, w_hi) + _dot(a_hi, w_lo) + b_ref[0]


def _ada_params(cond, ada_w, ada_b):
    depth, d, n = ada_w.shape
    tn = 1024
    return pl.pallas_call(
        _ada_kernel,
        grid=(depth, n // tn),
        in_specs=[
            pl.BlockSpec((COND_ROWS, d), lambda i, j: (0, 0)),
            pl.BlockSpec((1, d, tn), lambda i, j: (i, 0, j)),
            pl.BlockSpec((1, 1, tn), lambda i, j: (i, 0, j)),
        ],
        out_specs=pl.BlockSpec((1, COND_ROWS, tn), lambda i, j: (i, 0, j)),
        out_shape=jax.ShapeDtypeStruct((depth, COND_ROWS, n), F32),
        compiler_params=_params("parallel", "parallel"),
        name="ada_params",
    )(cond, ada_w, ada_b.reshape(depth, 1, n))


def _norm_mod(x, g, shift, scale):
    ms = jnp.mean(x * x, axis=-1, keepdims=True)
    return (x * lax.rsqrt(ms + NORM_EPS) * g) * (1.0 + scale) + shift


def _rms(x, g):
    ms = jnp.mean(x * x, axis=-1, keepdims=True)
    return x * lax.rsqrt(ms + NORM_EPS) * g


def _rope_chunk(c, cos, sa, sb):
    return c * cos + pltpu.roll(c, LANES - 16, 1) * sa + pltpu.roll(c, 16, 1) * sb


def _rope_tables(seq):
    t = jnp.arange(seq)
    pos_row = (t // GRID_W).astype(F32)
    pos_col = (t % GRID_W).astype(F32)
    inv = ROPE_BASE ** (-jnp.arange(0, 32, 2, dtype=F32) / 32)
    lane = jnp.arange(LANES)
    d = lane % 64
    r = d % 32
    first = (r < 16)[None, :]
    pos = jnp.where((d // 32)[None, :] == 0, pos_row[:, None], pos_col[:, None])
    ang = pos * inv[r % 16][None, :]
    cos, sin = jnp.cos(ang), jnp.sin(ang)
    return cos, jnp.where(first, -sin, 0.0), jnp.where(first, 0.0, sin)


def _token_specs(tm, d, mod_row):
    x_spec = pl.BlockSpec((1, tm, d), lambda b, t: (b, t, 0))
    if mod_row is None:
        mod_spec = pl.BlockSpec((1, N_MOD, d), lambda b, t: (b, 0, 0))
    else:
        mod_spec = pl.BlockSpec((1, N_MOD, d), lambda b, t: (mod_row, 0, 0))
    return x_spec, mod_spec


def _rope_specs(tm):
    return [pl.BlockSpec((tm, LANES), lambda b, t: (t, 0))] * 3


def _out_spec(tm, n):
    return pl.BlockSpec((1, tm, n), lambda b, t: (b, t, 0))


def _diff_proj_kernel(*refs, rope):
    if rope:
        x_ref, mod_ref, g_ref, w_ref, wvt_ref, cos_ref, sa_ref, sb_ref, q_ref, k_ref, vt_ref = refs
        cos, sa, sb = cos_ref[...], sa_ref[...], sb_ref[...]
    else:
        x_ref, mod_ref, g_ref, w_ref, wvt_ref, q_ref, k_ref, vt_ref = refs
    h = _norm_mod(x_ref[0], g_ref[...], mod_ref[0, 0:1, :], mod_ref[0, 1:2, :]).astype(BF16)
    for idx, o_ref in enumerate((q_ref, k_ref)):
        for c in range(DIFF_QK // 256):
            y = _dot(h, w_ref[:, idx * DIFF_QK + c * 256: idx * DIFF_QK + (c + 1) * 256])
            for s in range(2):
                ys = y[:, s * LANES:(s + 1) * LANES]
                if rope:
                    ys = _rope_chunk(ys, cos, sa, sb)
                if idx == 0:
                    ys = ys * (DIFF_HEAD_DIM ** -0.5 * LOG2E)
                o_ref[0, :, c * 256 + s * LANES: c * 256 + (s + 1) * LANES] = ys.astype(BF16)
    vt_ref[0] = _dot_nt(wvt_ref[...], h).astype(BF16)


def _vt_spec(tm, n):
    return pl.BlockSpec((1, n, tm), lambda b, t: (b, 0, t))


def _diff_proj(x, mod, mod_row, norm_g, w, wvt, tables, tm):
    bx, s, d = x.shape
    rope = tables is not None
    x_spec, mod_spec = _token_specs(tm, d, mod_row)
    in_specs = [x_spec, mod_spec, _const_spec((1, d)), _const_spec(w.shape), _const_spec(wvt.shape)]
    args = [x, mod, norm_g, w, wvt]
    if rope:
        in_specs += _rope_specs(tm)
        args += list(tables)
    out = jax.ShapeDtypeStruct((bx, s, DIFF_QK), BF16)
    return pl.pallas_call(
        functools.partial(_diff_proj_kernel, rope=rope),
        grid=(bx, s // tm),
        in_specs=in_specs,
        out_specs=[_out_spec(tm, DIFF_QK), _out_spec(tm, DIFF_QK), _vt_spec(tm, DIFF_V)],
        out_shape=[out, out, jax.ShapeDtypeStruct((bx, DIFF_V, s), BF16)],
        compiler_params=_params("parallel", "parallel"),
        name="diff_proj",
    )(*args)


def _col_max(s, rows=64):
    parts = [jnp.max(s[i:i + rows], axis=0, keepdims=True) for i in range(0, s.shape[0], rows)]
    while len(parts) > 1:
        parts = [jnp.maximum(a, b) for a, b in zip(parts[::2], parts[1::2])]
    return parts[0]


def _flash_kernel(*refs, n_maps, has_prefix, lam_init, hb, dq):
    refs = list(refs)
    q_ref, k_ref, vt_ref = refs[:3]
    pos = 3
    if has_prefix:
        kc_ref, vct_ref = refs[pos:pos + 2]
        pos += 2
    if n_maps == 2:
        lam_ref, ng_ref = refs[pos:pos + 2]
        pos += 2
    o_ref = refs[pos]
    scratch = refs[pos + 1:]
    if n_maps == 2:
        qm_ref, m_ref, acc_ref = scratch
    else:
        m_ref, acc_ref = scratch
    kv = pl.program_id(3)

    def scores(ci, k_src):
        hd = ci // n_maps
        q = qm_ref[ci] if n_maps == 2 else q_ref[0, :, hd * dq:(hd + 1) * dq]
        return _dot_nt(k_src[0, :, hd * dq:(hd + 1) * dq], q)

    def update(ci, s, vt_src):
        hd = ci // n_maps
        vt = vt_src[0, hd * LANES:(hd + 1) * LANES, :]
        m_prev = m_ref[ci]
        m_new = jnp.maximum(m_prev, _col_max(s))
        alpha = jnp.exp2(m_prev - m_new)
        p = jnp.exp2(s - m_new).astype(BF16)
        v1 = jnp.concatenate([vt, jnp.ones((ONES_ROWS, vt.shape[1]), BF16)], axis=0)
        acc_ref[ci] = acc_ref[ci] * alpha + _dot(v1, p)
        m_ref[ci] = m_new

    def all_steps(k_src, vt_src):
        n_chains = hb * n_maps
        pending = [scores(ci, k_src) for ci in range(min(LOOKAHEAD, n_chains))]
        for ci in range(n_chains):
            if ci + LOOKAHEAD < n_chains:
                pending.append(scores(ci + LOOKAHEAD, k_src))
            update(ci, pending.pop(0), vt_src)

    @pl.when(kv == 0)
    def _init():
        m_ref[...] = jnp.full(m_ref.shape, NEG_BIG, F32)
        acc_ref[...] = jnp.zeros(acc_ref.shape, F32)
        if n_maps == 2:
            for hd in range(hb):
                q = q_ref[0, :, hd * dq:(hd + 1) * dq]
                lane = lax.broadcasted_iota(jnp.int32, q.shape, 1)
                qm_ref[2 * hd] = jnp.where(lane < DIFF_HEAD_DIM, q, jnp.zeros_like(q))
                qm_ref[2 * hd + 1] = jnp.where(lane >= DIFF_HEAD_DIM, q, jnp.zeros_like(q))
        if has_prefix:
            all_steps(kc_ref, vct_ref)

    all_steps(k_ref, vt_ref)

    @pl.when(kv == pl.num_programs(3) - 1)
    def _finish():
        if n_maps == 2:
            lf = lam_ref[...]
            l1 = jnp.sum(lf[0:1] * lf[1:2], axis=1, keepdims=True)
            l2 = jnp.sum(lf[2:3] * lf[3:4], axis=1, keepdims=True)
            lam = jnp.exp(l1) - jnp.exp(l2) + lam_init
        for hd in range(hb):
            a0 = acc_ref[hd * n_maps]
            o = a0[:LANES] / a0[LANES:LANES + 1]
            if n_maps == 2:
                a1 = acc_ref[hd * n_maps + 1]
                o = o - lam * (a1[:LANES] / a1[LANES:LANES + 1])
                ms = jnp.mean(o * o, axis=0, keepdims=True)
                o = o * lax.rsqrt(ms + NORM_EPS) * ng_ref[...] * (1.0 - lam_init)
            o_ref[0, :, hd * LANES:(hd + 1) * LANES] = o.T.astype(o_ref.dtype)


def _flash(q, k, vt, prefix, extras, *, heads, dq, tq, tk, lam_init=0.0, hb=1):
    b, s, _ = q.shape
    n_maps = 2 if extras is not None else 1
    in_specs = [
        pl.BlockSpec((1, tq, hb * dq), lambda bi, h, qi, kv: (bi, qi, h)),
        pl.BlockSpec((1, tk, hb * dq), lambda bi, h, qi, kv: (bi, kv, h)),
        pl.BlockSpec((1, hb * LANES, tk), lambda bi, h, qi, kv: (bi, h, kv)),
    ]
    args = [q, k, vt]
    if prefix is not None:
        cl = prefix[0].shape[1]
        in_specs += [
            pl.BlockSpec((1, cl, hb * dq), lambda bi, h, qi, kv: (bi, 0, h)),
            pl.BlockSpec((1, hb * LANES, cl), lambda bi, h, qi, kv: (bi, h, 0)),
        ]
        args += list(prefix)
    scratch = []
    if n_maps == 2:
        in_specs += [_const_spec(extras[0].shape), _const_spec(extras[1].shape)]
        args += list(extras)
        scratch.append(pltpu.VMEM((2 * hb, tq, dq), BF16))
    scratch += [pltpu.VMEM((hb * n_maps, 1, tq), F32), pltpu.VMEM((hb * n_maps, LANES + ONES_ROWS, tq), F32)]
    return pl.pallas_call(
        functools.partial(_flash_kernel, n_maps=n_maps, has_prefix=prefix is not None, lam_init=lam_init,
                          hb=hb, dq=dq),
        grid=(b, heads // hb, s // tq, k.shape[1] // tk),
        in_specs=in_specs,
        out_specs=pl.BlockSpec((1, tq, hb * LANES), lambda bi, h, qi, kv: (bi, qi, h)),
        out_shape=jax.ShapeDtypeStruct((b, s, heads * LANES), BF16),
        scratch_shapes=scratch,
        compiler_params=_params("parallel", "parallel", "parallel", "arbitrary"),
        name="flash_diff" if n_maps == 2 else "flash_mla",
    )(*args)


def _gla_proj_kernel(x_ref, mod_ref, g_ref, w_ref, wr_ref, wg_ref, bg_ref, q_ref, k_ref, v_ref, og_ref, gate_ref):
    h = _norm_mod(x_ref[0], g_ref[...], mod_ref[0, 0:1, :], mod_ref[0, 1:2, :]).astype(BF16)
    col = 0
    for o_ref, width, scale in ((q_ref, GLA_KEY, GLA_DK ** -0.5), (k_ref, GLA_KEY, None),
                                (v_ref, GLA_VAL, None), (og_ref, GLA_VAL, None)):
        for c in range(width // 256):
            y = _dot(h, w_ref[:, col + c * 256: col + (c + 1) * 256])
            if scale is not None:
                y = y * scale
            o_ref[0, :, c * 256:(c + 1) * 256] = y.astype(BF16)
        col += width
    r_hi, r_lo = _split_bf16(_dot(h, wr_ref[...]))
    for c in range(2 * GLA_KEY // 256):
        wg = wg_ref[:, c * 256:(c + 1) * 256]
        z = _dot(r_hi, wg) + _dot(r_lo, wg) + bg_ref[:, c * 256:(c + 1) * 256]
        log_sig = jnp.minimum(z, 0.0) - jnp.log(1.0 + jnp.exp(-jnp.abs(z)))
        gate_ref[0, :, c * 256:(c + 1) * 256] = log_sig / GLA_GATE_NORM


def _gla_proj(x, mod, mod_row, norm_g, w, wr, wg, bg, tm):
    bx, s, d = x.shape
    x_spec, mod_spec = _token_specs(tm, d, mod_row)
    return pl.pallas_call(
        _gla_proj_kernel,
        grid=(bx, s // tm),
        in_specs=[x_spec, mod_spec, _const_spec((1, d)), _const_spec(w.shape), _const_spec(wr.shape),
                  _const_spec(wg.shape), _const_spec(bg.shape)],
        out_specs=[_out_spec(tm, GLA_KEY), _out_spec(tm, GLA_KEY), _out_spec(tm, GLA_VAL),
                   _out_spec(tm, GLA_VAL), _out_spec(tm, 2 * GLA_KEY)],
        out_shape=[jax.ShapeDtypeStruct((bx, s, GLA_KEY), BF16), jax.ShapeDtypeStruct((bx, s, GLA_KEY), BF16),
                   jax.ShapeDtypeStruct((bx, s, GLA_VAL), BF16), jax.ShapeDtypeStruct((bx, s, GLA_VAL), BF16),
                   jax.ShapeDtypeStruct((bx, s, 2 * GLA_KEY), F32)],
        compiler_params=_params("parallel", "parallel"),
        name="gla_proj",
    )(x, mod, norm_g, w, wr, wg, bg)


def _gla_scan_kernel(q_ref, k_ref, v_ref, g_ref, s0_ref, o_ref, sfin_ref, st_ref):
    direction = pl.program_id(1)
    i = pl.program_id(2)
    n_chunks = q_ref.shape[1] // GLA_CHUNK

    @pl.when(i == 0)
    def _load_state():
        st_ref[...] = s0_ref[0, 0]

    def run(backward):
        tb = q_ref.shape[1]
        row = lax.broadcasted_iota(jnp.int32, (tb, tb), 0)
        colm = lax.broadcasted_iota(jnp.int32, (tb, tb), 1)
        same = (row // GLA_CHUNK) == (colm // GLA_CHUNK)
        seen = (colm >= row) if backward else (colm <= row)
        seen_m = jnp.where(same & seen, 1.0, 0.0).astype(BF16)
        rest_m = jnp.where(same & jnp.logical_not(seen), 1.0, 0.0).astype(BF16)
        g_hi, g_lo = _split_bf16(g_ref[0])
        cum = _dot(seen_m, g_hi) + _dot(seen_m, g_lo)
        rest = _dot(rest_m, g_hi) + _dot(rest_m, g_lo)
        q = q_ref[0].astype(F32)
        k = k_ref[0].astype(F32)
        q_dec = (q * jnp.exp(cum)).astype(BF16)
        k_inv = (k * jnp.exp(-cum)).astype(BF16)
        k_end = (k * jnp.exp(rest)).astype(BF16)
        decay = jnp.exp(cum + rest)
        within = same & seen
        chunk_of_row = lax.broadcasted_iota(jnp.int32, (tb, GLA_DK), 0) // GLA_CHUNK
        order = list(range(n_chunks - 1, -1, -1) if backward else range(n_chunks))
        for h in range(GLA_HEADS):
            ksl = slice(h * GLA_DK, (h + 1) * GLA_DK)
            vsl = slice(h * GLA_DV, (h + 1) * GLA_DV)
            v = v_ref[0, :, vsl]
            qd = q_dec[:, ksl]
            a = jnp.where(within, _dot_nt(qd, k_inv[:, ksl]), 0.0)
            o_intra = _dot(a.astype(BF16), v)
            ke = k_end[:, ksl]
            ke_wide = jnp.concatenate([jnp.where(chunk_of_row == c, ke, jnp.zeros_like(ke)) for c in range(n_chunks)],
                                      axis=1)
            inc = _dot_tn(v, ke_wide)
            st = st_ref[h]
            for c in order:
                sl = slice(c * GLA_CHUNK, (c + 1) * GLA_CHUNK)
                o_ref[0, 0, sl, vsl] = (o_intra[sl] + _dot_nt(qd[sl], st.astype(BF16))).astype(o_ref.dtype)
                st = st * decay[c * GLA_CHUNK:c * GLA_CHUNK + 1, ksl] + inc[:, c * GLA_DK:(c + 1) * GLA_DK]
            st_ref[h] = st

    pl.when(direction == 0)(lambda: run(False))
    pl.when(direction == 1)(lambda: run(True))

    @pl.when(i == pl.num_programs(2) - 1)
    def _store_state():
        sfin_ref[0, 0] = st_ref[...]


def _gla_scan(q, k, v, g, s0, tb):
    b, s, _ = q.shape
    nb = s // tb

    def blk(d, i):
        return i + d * (nb - 1 - 2 * i)

    st_spec = pl.BlockSpec((1, 1, GLA_HEADS, GLA_DV, GLA_DK), lambda bi, d, i: (bi, d, 0, 0, 0))
    return pl.pallas_call(
        _gla_scan_kernel,
        grid=(b, 2, nb),
        in_specs=[
            pl.BlockSpec((1, tb, GLA_KEY), lambda bi, d, i: (bi, blk(d, i), 0)),
            pl.BlockSpec((1, tb, GLA_KEY), lambda bi, d, i: (bi, blk(d, i), 0)),
            pl.BlockSpec((1, tb, GLA_VAL), lambda bi, d, i: (bi, blk(d, i), 0)),
            pl.BlockSpec((1, tb, GLA_KEY), lambda bi, d, i: (bi, blk(d, i), d)),
            st_spec,
        ],
        out_specs=[pl.BlockSpec((1, 1, tb, GLA_VAL), lambda bi, d, i: (d, bi, blk(d, i), 0)), st_spec],
        out_shape=[jax.ShapeDtypeStruct((2, b, s, GLA_VAL), BF16),
                   jax.ShapeDtypeStruct((b, 2, GLA_HEADS, GLA_DV, GLA_DK), F32)],
        scratch_shapes=[pltpu.VMEM((GLA_HEADS, GLA_DV, GLA_DK), F32)],
        compiler_params=_params("parallel", "parallel", "arbitrary"),
        name="gla_scan",
    )(q, k, v, g, s0)


def _mla_proj_kernel(*refs, rope):
    if rope:
        (x_ref, mod_ref, g_ref, w_ref, qg_ref, wq_ref, kvg_ref, wkn_ref, wvt_ref,
         cos_ref, sa_ref, sb_ref, q_ref, k_ref, vt_ref) = refs
        cos, sa, sb = cos_ref[...], sa_ref[...], sb_ref[...]
    else:
        x_ref, mod_ref, g_ref, w_ref, qg_ref, wq_ref, kvg_ref, wkn_ref, wvt_ref, q_ref, k_ref, vt_ref = refs
    h = _norm_mod(x_ref[0], g_ref[...], mod_ref[0, 0:1, :], mod_ref[0, 1:2, :]).astype(BF16)
    y = _dot(h, w_ref[...])
    c_q = _rms(y[:, :MLA_Q_LORA], qg_ref[...]).astype(BF16)
    c_kv = _rms(y[:, MLA_Q_LORA:MLA_Q_LORA + MLA_KV_LORA], kvg_ref[...]).astype(BF16)
    k_rope = y[:, MLA_Q_LORA + MLA_KV_LORA:]
    if rope:
        k_rope = _rope_chunk(k_rope, cos, sa, sb)
    k_rope = k_rope.astype(BF16)
    for hd in range(MLA_HEADS):
        base = hd * MLA_QK_PAD
        qh = _dot(c_q, wq_ref[:, base: base + MLA_QK_PAD])
        q_rope = qh[:, LANES:]
        if rope:
            q_rope = _rope_chunk(q_rope, cos, sa, sb)
        q_ref[0, :, base: base + LANES] = (qh[:, :LANES] * (MLA_SCALE * LOG2E)).astype(BF16)
        q_ref[0, :, base + LANES: base + MLA_QK_PAD] = (q_rope * (MLA_SCALE * LOG2E)).astype(BF16)
        k_ref[0, :, base + LANES: base + MLA_QK_PAD] = k_rope
    vt_ref[0] = _dot_nt(wvt_ref[...], c_kv).astype(BF16)
    for c in range(MLA_HEADS * MLA_NOPE // 256):
        kn = _dot(c_kv, wkn_ref[:, c * 256:(c + 1) * 256]).astype(BF16)
        for s in range(2):
            hd = 2 * c + s
            k_ref[0, :, hd * MLA_QK_PAD: hd * MLA_QK_PAD + LANES] = kn[:, s * LANES:(s + 1) * LANES]


def _mla_proj(x, mod, mod_row, norm_g, w, qg, wq, kvg, wkn, wvt, tables, tm):
    bx, s, d = x.shape
    rope = tables is not None
    x_spec, mod_spec = _token_specs(tm, d, mod_row)
    consts = [norm_g, w, qg, wq, kvg, wkn, wvt]
    in_specs = [x_spec, mod_spec] + [_const_spec(a.shape) for a in consts]
    args = [x, mod] + consts
    if rope:
        in_specs += _rope_specs(tm)
        args += list(tables)
    qk_w = MLA_HEADS * MLA_QK_PAD
    return pl.pallas_call(
        functools.partial(_mla_proj_kernel, rope=rope),
        grid=(bx, s // tm),
        in_specs=in_specs,
        out_specs=[_out_spec(tm, qk_w), _out_spec(tm, qk_w), _vt_spec(tm, MLA_HEADS * MLA_V)],
        out_shape=[jax.ShapeDtypeStruct((bx, s, qk_w), BF16), jax.ShapeDtypeStruct((bx, s, qk_w), BF16),
                   jax.ShapeDtypeStruct((bx, MLA_HEADS * MLA_V, s), BF16)],
        compiler_params=_params("parallel", "parallel"),
        name="mla_proj",
    )(*args)


def _top2_sum(a, b, c, d):
    hi1, lo1 = jnp.maximum(a, b), jnp.minimum(a, b)
    hi2, lo2 = jnp.maximum(c, d), jnp.minimum(c, d)
    return jnp.maximum(hi1, hi2) + jnp.maximum(jnp.minimum(hi1, hi2), jnp.maximum(lo1, lo2))


def _route_rows(scores, sel):
    eg = EXPERTS_PER_GROUP
    gs = [_top2_sum(*sel[g * eg:(g + 1) * eg]) for g in range(N_GROUPS)]
    best = jnp.maximum(jnp.maximum(gs[0], gs[1]), jnp.maximum(gs[2], gs[3]))
    gidx = jnp.where(gs[0] >= best, 0, jnp.where(gs[1] >= best, 1, jnp.where(gs[2] >= best, 2, 3)))

    def pick(rows, j):
        out = rows[(N_GROUPS - 1) * eg + j]
        for g in range(N_GROUPS - 2, -1, -1):
            out = jnp.where(gidx == g, rows[g * eg + j], out)
        return out

    loc_sel = [pick(sel, j) for j in range(eg)]
    loc_sc = [pick(scores, j) for j in range(eg)]
    weights = []
    for j in range(eg):
        rank = jnp.zeros_like(loc_sel[j])
        for i in range(eg):
            if i == j:
                continue
            beats = (loc_sel[i] >= loc_sel[j]) if i < j else (loc_sel[i] > loc_sel[j])
            rank = rank + jnp.where(beats, 1.0, 0.0)
        weights.append(jnp.where(rank < 1.5, loc_sc[j], 0.0))
    den = weights[0] + weights[1] + weights[2] + weights[3]
    weights = [w / den for w in weights]
    return [jnp.where(gidx == e // eg, weights[e % eg], 0.0) for e in range(N_EXPERTS)], gidx


def _mixer_output(pre, refs):
    if pre == "gla":
        of_ref, ob_ref, og_ref, ng_ref, w_ref = refs
        o = of_ref[0, 0].astype(F32) + ob_ref[0, 0].astype(F32)
        parts = []
        for h in range(GLA_HEADS):
            sl = slice(h * GLA_DV, (h + 1) * GLA_DV)
            parts.append((_rms(o[:, sl], ng_ref[...]) * _silu(og_ref[0, :, sl].astype(F32))).astype(BF16))
        return _dot(jnp.concatenate(parts, axis=1), w_ref[...])
    o_ref, w_ref = refs
    return _dot(o_ref[0], w_ref[...])


def _moe_kernel(*refs, pre, final_norm):
    n_pre = 5 if pre == "gla" else 2
    (x_ref, mod_ref, g_ref, rw_ref, rb_ref, shu_ref, shd_ref, tri_ref, wup_ref, wdn_ref, fg_ref,
     out_ref, perm_ref, hs_ref, cs_ref, ys_ref) = refs[n_pre:]
    x = x_ref[0] + mod_ref[0, 2:3, :] * _mixer_output(pre, refs[:n_pre])
    tm = x.shape[0]
    n_rows = hs_ref.shape[0]
    h, h_lo = _split_bf16(_norm_mod(x, g_ref[...], mod_ref[0, 3:4, :], mod_ref[0, 4:5, :]))
    both = _dot_nt(rw_ref[...], h)
    logits = both[:N_EXPERTS] + both[N_EXPERTS:] + _dot_nt(rw_ref[:N_EXPERTS, :], h_lo)
    scores = jax.nn.sigmoid(logits)
    sel = scores + rb_ref[...]
    rows, gidx = _route_rows([scores[e:e + 1, :] for e in range(N_EXPERTS)],
                             [sel[e:e + 1, :] for e in range(N_EXPERTS)])
    comb = jnp.concatenate(rows + [jnp.zeros((LANES - N_EXPERTS, tm), F32)], axis=0).T

    member = [jnp.where(gidx == g, 1.0, 0.0) for g in range(N_GROUPS)]
    member_m = jnp.concatenate(member + [jnp.zeros((8 - N_GROUPS, tm), F32)], axis=0).astype(BF16)
    rank = _dot(member_m, tri_ref[...])
    dest = jnp.zeros((1, tm), F32)
    start = jnp.zeros((1, 1), F32)
    seg_start, seg_count = [], []
    for g in range(N_GROUPS):
        count = jnp.sum(member[g], axis=1, keepdims=True)
        dest = dest + member[g] * (start + rank[g:g + 1] - 1.0)
        seg_start.append(start[0, 0].astype(jnp.int32))
        seg_count.append(count[0, 0].astype(jnp.int32))
        start = jnp.floor((start + count + (MOE_ALIGN - 1)) * (1.0 / MOE_ALIGN)) * MOE_ALIGN
    row_id = lax.broadcasted_iota(jnp.int32, (n_rows, tm), 0)
    perm = jnp.where(row_id == dest.astype(jnp.int32), 1.0, 0.0).astype(BF16)
    perm_ref[...] = perm
    c_hi, c_lo = _split_bf16(comb)
    moved = _dot(perm, jnp.concatenate([h, c_hi, c_lo], axis=1))
    d = h.shape[1]
    hs_ref[...] = moved[:, :d].astype(BF16)
    cs_ref[...] = moved[:, d:d + LANES] + moved[:, d + LANES:]
    ys_ref[...] = jnp.zeros(ys_ref.shape, F32)
    a = _dot(h, shu_ref[:, :SHARED_FF])
    u = _dot(h, shu_ref[:, SHARED_FF:])
    out_ref[0] = x + mod_ref[0, 5:6, :] * _dot((_silu(a) * u).astype(BF16), shd_ref[...])

    def run_group(grp, size, n_passes):
        def one_pass(i, carry):
            r0 = pl.multiple_of(seg_start[grp] + i * size, MOE_ALIGN)
            hc = hs_ref[pl.ds(r0, size), :]
            cc = cs_ref[pl.ds(r0, size), :]
            acts = []
            for j in range(EXPERTS_PER_GROUP):
                e = grp * EXPERTS_PER_GROUP + j
                a = _dot(hc, wup_ref[0, e, :, :EXPERT_FF])
                u = _dot(hc, wup_ref[0, e, :, EXPERT_FF:])
                scale = jnp.broadcast_to(cc[:, e:e + 1], a.shape)
                acts.append((_silu(a) * u * scale).astype(BF16))
            w_dn = wdn_ref[0, grp * EXPERTS_PER_GROUP:(grp + 1) * EXPERTS_PER_GROUP]
            y = _dot(jnp.concatenate(acts, axis=1), w_dn.reshape(EXPERTS_PER_GROUP * EXPERT_FF, -1))
            ys_ref[pl.ds(r0, size), :] = ys_ref[pl.ds(r0, size), :] + y
            return carry

        lax.fori_loop(0, n_passes, one_pass, 0)

    for grp in range(N_GROUPS):
        n = seg_count[grp]
        fits = n <= MOE_CHUNK_SMALL
        run_group(grp, MOE_CHUNK_SMALL, jnp.where(fits, jnp.minimum(n, 1), 0))
        run_group(grp, MOE_CHUNK_BIG, jnp.where(fits, 0, (n + (MOE_CHUNK_BIG - 1)) // MOE_CHUNK_BIG))

    out = out_ref[0] + mod_ref[0, 5:6, :] * _dot_tn(perm_ref[...], ys_ref[...].astype(BF16))
    if final_norm:
        out = _rms(out, fg_ref[...])
    out_ref[0] = out


def _moe(mixer, x, mod, mod_row, norm_g, rw, rb, sh_up, sh_dn, w_up, w_dn, layer, final_g, final_norm, tm):
    bx, s, d = x.shape
    x_spec, mod_spec = _token_specs(tm, d, mod_row)
    pre = mixer[0]
    if pre == "gla":
        o2, og, ng, w_out = mixer[1:]
        o2 = o2.reshape(2, bx, s, GLA_VAL)
        pre_specs = [pl.BlockSpec((1, 1, tm, GLA_VAL), lambda b, t: (0, b, t, 0)),
                     pl.BlockSpec((1, 1, tm, GLA_VAL), lambda b, t: (1, b, t, 0)),
                     _out_spec(tm, GLA_VAL), _const_spec(ng.shape, True), _const_spec(w_out.shape, True)]
        pre_args = [o2, o2, og.reshape(bx, s, GLA_VAL), ng, w_out]
    else:
        o, w_out = mixer[1:]
        o = o.reshape(bx, s, -1)
        pre_specs = [_out_spec(tm, o.shape[2]), _const_spec(w_out.shape, True)]
        pre_args = [o, w_out]
    tri = (jnp.arange(tm)[:, None] <= jnp.arange(tm)[None, :]).astype(BF16)
    consts = [norm_g, rw, rb, sh_up, sh_dn, tri]
    n_rows = -(-(tm + (N_GROUPS - 1) * (MOE_ALIGN - 1) + MOE_CHUNK_BIG) // 256) * 256
    resident = pl.Buffered(1)
    in_specs = (pre_specs + [x_spec, mod_spec] + [_const_spec(a.shape, True) for a in consts] + [
        pl.BlockSpec((1,) + w_up.shape[1:], lambda b, t: (layer, 0, 0, 0), pipeline_mode=resident),
        pl.BlockSpec((1,) + w_dn.shape[1:], lambda b, t: (layer, 0, 0, 0), pipeline_mode=resident),
        _const_spec(final_g.shape, True)])
    return pl.pallas_call(
        functools.partial(_moe_kernel, pre=pre, final_norm=final_norm),
        grid=(bx, s // tm),
        in_specs=in_specs,
        out_specs=x_spec,
        out_shape=jax.ShapeDtypeStruct(x.shape, F32),
        scratch_shapes=[pltpu.VMEM((n_rows, tm), BF16), pltpu.VMEM((n_rows, d), BF16),
                        pltpu.VMEM((n_rows, LANES), F32), pltpu.VMEM((n_rows, d), F32)],
        compiler_params=_params("parallel", "parallel"),
        name="moe",
    )(*pre_args, x, mod, *consts, w_up, w_dn, final_g)


def _diff_weights(w_in):
    d = w_in.shape[0]

    def regroup(w):
        return w.reshape(d, 2, DIFF_HEADS, DIFF_HEAD_DIM).transpose(0, 2, 1, 3).reshape(d, DIFF_QK)

    w_qk = jnp.concatenate([regroup(w_in[:, :DIFF_QK]), regroup(w_in[:, DIFF_QK:2 * DIFF_QK])], axis=1)
    return w_qk.astype(BF16), w_in[:, 2 * DIFF_QK:].T.astype(BF16)


def _gla_weights(w_in, gate_w, gate_b):
    d = w_in.shape[0]
    main = 2 * GLA_KEY + 2 * GLA_VAL
    w_r = jnp.zeros((d, LANES), F32).at[:, :2 * GLA_GATE_RANK].set(w_in[:, main:]).astype(BF16)
    w_g = jnp.zeros((LANES, 2 * GLA_KEY), F32)
    w_g = w_g.at[:GLA_GATE_RANK, :GLA_KEY].set(gate_w[0])
    w_g = w_g.at[GLA_GATE_RANK:2 * GLA_GATE_RANK, GLA_KEY:].set(gate_w[1]).astype(BF16)
    return w_in[:, :main].astype(BF16), w_r, w_g, gate_b.reshape(1, 2 * GLA_KEY)


def _mla_weights(w_in, w_uq, w_ukv):
    d = w_in.shape[0]
    w = jnp.zeros((d, MLA_Q_LORA + MLA_KV_LORA + LANES), F32).at[:, :w_in.shape[1]].set(w_in).astype(BF16)
    wq = w_uq.reshape(MLA_Q_LORA, MLA_HEADS, MLA_NOPE + MLA_ROPE)
    wq = jnp.pad(wq, ((0, 0), (0, 0), (0, MLA_QK_PAD - MLA_NOPE - MLA_ROPE)))
    wq = wq.reshape(MLA_Q_LORA, MLA_HEADS * MLA_QK_PAD).astype(BF16)
    wkv = w_ukv.reshape(MLA_KV_LORA, MLA_HEADS, MLA_NOPE + MLA_V)
    wkn = wkv[:, :, :MLA_NOPE].reshape(MLA_KV_LORA, MLA_HEADS * MLA_NOPE).astype(BF16)
    wvt = wkv[:, :, MLA_NOPE:].reshape(MLA_KV_LORA, MLA_HEADS * MLA_V).T.astype(BF16)
    return w, wq, wkn, wvt


def _ctx_views(qkv, b, cl):
    q, k, vt = qkv
    vt = vt.reshape(vt.shape[1], b, cl).transpose(1, 0, 2)
    return q.reshape(b, cl, -1), k.reshape(b, cl, -1), vt


def kernel(x, c, ctx, c_ctx, ada_w, ada_b, norm_g, router_w, router_b, moe_w_up, moe_w_down, shared_w_up,
           shared_w_down, diff_w_in, diff_lam, diff_norm_g, diff_w_out, gla_w_in, gla_gate_w, gla_gate_b,
           gla_norm_g, gla_w_out, mla_w_in, mla_q_norm_g, mla_w_uq, mla_kv_norm_g, mla_w_ukv, mla_w_out,
           final_norm_g):
    b, s, d = x.shape
    cl = ctx.shape[1]
    assert b + 1 <= COND_ROWS and d == D_MODEL
    depth = ada_w.shape[0]
    ctx_row = b

    cond = jnp.zeros((COND_ROWS, d), F32).at[:b].set(c).at[ctx_row].set(c_ctx)
    mods = _ada_params(cond, ada_w, ada_b).reshape(depth, COND_ROWS, N_MOD, d)

    tables = _rope_tables(s)
    rw = jnp.concatenate(_split_bf16(router_w.T), axis=0)
    rb = router_b.reshape(N_EXPERTS, 1)
    w_up_all = moe_w_up.astype(BF16)
    w_dn_all = moe_w_down.astype(BF16)
    final_g = final_norm_g.reshape(1, d)

    tm_lat = 512
    n_ctx = b * cl
    tm_ctx = 512 if n_ctx % 512 == 0 else cl
    x_lat = x
    x_ctx = ctx.reshape(1, n_ctx, d)

    for i in range(depth):
        need_ctx = i < depth - 1
        mod = mods[i]
        g1 = norm_g[i, 0].reshape(1, d)
        g2 = norm_g[i, 1].reshape(1, d)
        kind, j = i % N_MIXERS, i // N_MIXERS
        if kind == 0:
            lam_init = 0.8 - 0.6 * math.exp(-0.3 * i)
            w, wvt = _diff_weights(diff_w_in[j])
            w_out = diff_w_out[j].astype(BF16)
            extras = (diff_lam[j], diff_norm_g[j].reshape(2 * DIFF_HEAD_DIM, 1))
            q_l, k_l, v_l = _diff_proj(x_lat, mod, None, g1, w, wvt, tables, tm_lat)
            q_c, k_c, v_c = _ctx_views(_diff_proj(x_ctx, mod, ctx_row, g1, w, wvt, None, tm_ctx), b, cl)
            o_lat = _flash(q_l, k_l, v_l, (k_c, v_c), extras, heads=DIFF_HEADS, dq=2 * DIFF_HEAD_DIM,
                           tq=512, tk=1024, lam_init=lam_init, hb=DIFF_HEADS)
            mix_lat = ("plain", o_lat, w_out)
            if need_ctx:
                o_ctx = _flash(q_c, k_c, v_c, None, extras, heads=DIFF_HEADS, dq=2 * DIFF_HEAD_DIM,
                               tq=cl, tk=cl, lam_init=lam_init, hb=DIFF_HEADS)
                mix_ctx = ("plain", o_ctx, w_out)
        elif kind == 1:
            w, w_r, w_g, b_g = _gla_weights(gla_w_in[j], gla_gate_w[j], gla_gate_b[j])
            w_out = gla_w_out[j].astype(BF16)
            ng = gla_norm_g[j].reshape(1, GLA_DV)
            q_l, k_l, v_l, og_l, gt_l = _gla_proj(x_lat, mod, None, g1, w, w_r, w_g, b_g, tm_lat)
            q_c, k_c, v_c, og_c, gt_c = (t.reshape(b, cl, -1)
                                         for t in _gla_proj(x_ctx, mod, ctx_row, g1, w, w_r, w_g, b_g, tm_ctx))
            s0 = jnp.zeros((b, 2, GLA_HEADS, GLA_DV, GLA_DK), F32)
            o_c, s_c = _gla_scan(q_c, k_c, v_c, gt_c, s0, cl)
            o_l, _ = _gla_scan(q_l, k_l, v_l, gt_l, s_c, 256)
            mix_lat = ("gla", o_l, og_l, ng, w_out)
            if need_ctx:
                mix_ctx = ("gla", o_c, og_c, ng, w_out)
        else:
            w, wq, wkn, wvt = _mla_weights(mla_w_in[j], mla_w_uq[j], mla_w_ukv[j])
            w_out = mla_w_out[j].astype(BF16)
            qg = mla_q_norm_g[j].reshape(1, MLA_Q_LORA)
            kvg = mla_kv_norm_g[j].reshape(1, MLA_KV_LORA)
            q_l, k_l, v_l = _mla_proj(x_lat, mod, None, g1, w, qg, wq, kvg, wkn, wvt, tables, tm_lat)
            q_c, k_c, v_c = _ctx_views(_mla_proj(x_ctx, mod, ctx_row, g1, w, qg, wq, kvg, wkn, wvt, None, tm_ctx),
                                       b, cl)
            o_lat = _flash(q_l, k_l, v_l, (k_c, v_c), None, heads=MLA_HEADS, dq=MLA_QK_PAD, tq=512, tk=1024,
                           hb=MLA_HEADS)
            mix_lat = ("plain", o_lat, w_out)
            if need_ctx:
                o_ctx = _flash(q_c, k_c, v_c, None, None, heads=MLA_HEADS, dq=MLA_QK_PAD, tq=cl, tk=cl,
                               hb=MLA_HEADS)
                mix_ctx = ("plain", o_ctx, w_out)

        sh_up = shared_w_up[i].astype(BF16)
        sh_dn = shared_w_down[i].astype(BF16)
        last = i == depth - 1
        x_lat = _moe(mix_lat, x_lat, mod, None, g2, rw, rb, sh_up, sh_dn, w_up_all, w_dn_all, i, final_g, last,
                     tm_lat)
        if need_ctx:
            x_ctx = _moe(mix_ctx, x_ctx, mod, ctx_row, g2, rw, rb, sh_up, sh_dn, w_up_all, w_dn_all, i, final_g,
                         False, tm_ctx)
    return x_lat
```

```python
import functools
import math

import jax
import jax.numpy as jnp
from jax import lax
from jax.experimental import pallas as pl
from jax.experimental.pallas import tpu as pltpu

F32 = jnp.float32
BF16 = jnp.bfloat16

D_MODEL = 1024
DEPTH = 4
GRID_W = 64
N_MIXERS = 3
NORM_EPS = 1e-6
ROPE_BASE = 10000.0
N_MOD = 6

DIFF_HEADS = 8
DIFF_HEAD_DIM = 64
DIFF_QK = 2 * DIFF_HEADS * DIFF_HEAD_DIM
DIFF_V = 2 * DIFF_HEADS * DIFF_HEAD_DIM

GLA_HEADS = 4
GLA_KEY = D_MODEL // 2
GLA_VAL = D_MODEL
GLA_DK = GLA_KEY // GLA_HEADS
GLA_DV = GLA_VAL // GLA_HEADS
GLA_GATE_RANK = 16
GLA_GATE_NORM = 16.0
GLA_CHUNK = 64

MLA_HEADS = 8
MLA_Q_LORA = 384
MLA_KV_LORA = 256
MLA_NOPE = 128
MLA_ROPE = 64
MLA_V = 128
MLA_SCALE = (MLA_NOPE + MLA_ROPE) ** -0.5
MLA_QK_PAD = 256

N_EXPERTS = 16
N_GROUPS = 4
EXPERTS_PER_GROUP = N_EXPERTS // N_GROUPS
EXPERT_FF = 256
SHARED_FF = 256

LANES = 128
BF16_ROWS = 16
MXU_TILE = 256
TOKEN_TILE = 512
ADA_COL_TILE = 1024
FLASH_TQ = 512
FLASH_TK = 1024
LOOKAHEAD = 2
ONES_ROWS = BF16_ROWS
GLA_BLOCK = 256
MOE_CHUNK_SMALL = 128
MOE_CHUNK_BIG = 208
MOE_ALIGN = BF16_ROWS
COND_ROWS = 16
NEG_BIG = -1e30
LOG2E = math.log2(math.e)
VMEM_LIMIT = 56 * 1024 * 1024

NT_DIMS = (((1,), (1,)), ((), ()))
TN_DIMS = (((0,), (0,)), ((), ()))


def _dot(a, b):
    return jnp.dot(a, b, preferred_element_type=F32)


def _dot_nt(a, b):
    return lax.dot_general(a, b, NT_DIMS, preferred_element_type=F32)


def _dot_tn(a, b):
    return lax.dot_general(a, b, TN_DIMS, preferred_element_type=F32)


def _split_bf16(a):
    hi = a.astype(BF16)
    lo = (a - hi.astype(F32)).astype(BF16)
    return hi, lo


def _silu(a):
    return a * jax.nn.sigmoid(a)


def _params(*sem):
    return pltpu.CompilerParams(dimension_semantics=sem, vmem_limit_bytes=VMEM_LIMIT)


def _const_spec(shape, single=False):
    nd = len(shape)
    if single:
        return pl.BlockSpec(shape, lambda *_: (0,) * nd, pipeline_mode=pl.Buffered(1))
    return pl.BlockSpec(shape, lambda *_: (0,) * nd)


def _ada_kernel(cond_ref, w_ref, b_ref, o_ref):
    a_hi, a_lo = _split_bf16(_silu(cond_ref[...]))
    w_hi, w_lo = _split_bf16(w_ref[0])
    o_ref[0] = _dot(a_hi, w_hi) + _dot(a_lo, w_hi) + _dot(a_hi, w_lo) + b_ref[0]


def _ada_params(cond, ada_w, ada_b):
    depth, d, n = ada_w.shape
    tn = ADA_COL_TILE
    return pl.pallas_call(
        _ada_kernel,
        grid=(depth, n // tn),
        in_specs=[
            pl.BlockSpec((COND_ROWS, d), lambda i, j: (0, 0)),
            pl.BlockSpec((1, d, tn), lambda i, j: (i, 0, j)),
            pl.BlockSpec((1, 1, tn), lambda i, j: (i, 0, j)),
        ],
        out_specs=pl.BlockSpec((1, COND_ROWS, tn), lambda i, j: (i, 0, j)),
        out_shape=jax.ShapeDtypeStruct((depth, COND_ROWS, n), F32),
        compiler_params=_params("parallel", "parallel"),
        name="ada_params",
    )(cond, ada_w, ada_b.reshape(depth, 1, n))


def _norm_mod(x, g, shift, scale):
    ms = jnp.mean(x * x, axis=-1, keepdims=True)
    return (x * lax.rsqrt(ms + NORM_EPS) * g) * (1.0 + scale) + shift


def _rms(x, g):
    ms = jnp.mean(x * x, axis=-1, keepdims=True)
    return x * lax.rsqrt(ms + NORM_EPS) * g


def _rope_chunk(c, cos, sa, sb):
    return c * cos + pltpu.roll(c, LANES - 16, 1) * sa + pltpu.roll(c, 16, 1) * sb


def _rope_tables(seq):
    t = jnp.arange(seq)
    pos_row = (t // GRID_W).astype(F32)
    pos_col = (t % GRID_W).astype(F32)
    inv = ROPE_BASE ** (-jnp.arange(0, 32, 2, dtype=F32) / 32)
    lane = jnp.arange(LANES)
    d = lane % 64
    r = d % 32
    first = (r < 16)[None, :]
    pos = jnp.where((d // 32)[None, :] == 0, pos_row[:, None], pos_col[:, None])
    ang = pos * inv[r % 16][None, :]
    cos, sin = jnp.cos(ang), jnp.sin(ang)
    return cos, jnp.where(first, -sin, 0.0), jnp.where(first, 0.0, sin)


def _token_specs(tm, d, mod_row):
    x_spec = pl.BlockSpec((1, tm, d), lambda b, t: (b, t, 0))
    if mod_row is None:
        mod_spec = pl.BlockSpec((1, N_MOD, d), lambda b, t: (b, 0, 0))
    else:
        mod_spec = pl.BlockSpec((1, N_MOD, d), lambda b, t: (mod_row, 0, 0))
    return x_spec, mod_spec


def _rope_specs(tm):
    return [pl.BlockSpec((tm, LANES), lambda b, t: (t, 0))] * 3


def _out_spec(tm, n):
    return pl.BlockSpec((1, tm, n), lambda b, t: (b, t, 0))


def _diff_proj_kernel(*refs, rope):
    if rope:
        x_ref, mod_ref, g_ref, w_ref, wvt_ref, cos_ref, sa_ref, sb_ref, q_ref, k_ref, vt_ref = refs
        cos, sa, sb = cos_ref[...], sa_ref[...], sb_ref[...]
    else:
        x_ref, mod_ref, g_ref, w_ref, wvt_ref, q_ref, k_ref, vt_ref = refs
    h = _norm_mod(x_ref[0], g_ref[...], mod_ref[0, 0:1, :], mod_ref[0, 1:2, :]).astype(BF16)
    for idx, o_ref in enumerate((q_ref, k_ref)):
        for c in range(DIFF_QK // MXU_TILE):
            y = _dot(h, w_ref[:, idx * DIFF_QK + c * MXU_TILE: idx * DIFF_QK + (c + 1) * MXU_TILE])
            for s in range(2):
                ys = y[:, s * LANES:(s + 1) * LANES]
                if rope:
                    ys = _rope_chunk(ys, cos, sa, sb)
                if idx == 0:
                    ys = ys * (DIFF_HEAD_DIM ** -0.5 * LOG2E)
                o_ref[0, :, c * MXU_TILE + s * LANES: c * MXU_TILE + (s + 1) * LANES] = ys.astype(BF16)
    vt_ref[0] = _dot_nt(wvt_ref[...], h).astype(BF16)


def _vt_spec(tm, n):
    return pl.BlockSpec((1, n, tm), lambda b, t: (b, 0, t))


def _diff_proj(x, mod, mod_row, norm_g, w, wvt, tables, tm):
    bx, s, d = x.shape
    rope = tables is not None
    x_spec, mod_spec = _token_specs(tm, d, mod_row)
    in_specs = [x_spec, mod_spec, _const_spec((1, d)), _const_spec(w.shape), _const_spec(wvt.shape)]
    args = [x, mod, norm_g, w, wvt]
    if rope:
        in_specs += _rope_specs(tm)
        args += list(tables)
    out = jax.ShapeDtypeStruct((bx, s, DIFF_QK), BF16)
    return pl.pallas_call(
        functools.partial(_diff_proj_kernel, rope=rope),
        grid=(bx, s // tm),
        in_specs=in_specs,
        out_specs=[_out_spec(tm, DIFF_QK), _out_spec(tm, DIFF_QK), _vt_spec(tm, DIFF_V)],
        out_shape=[out, out, jax.ShapeDtypeStruct((bx, DIFF_V, s), BF16)],
        compiler_params=_params("parallel", "parallel"),
        name="diff_proj",
    )(*args)


def _col_max(s, rows=64):
    parts = [jnp.max(s[i:i + rows], axis=0, keepdims=True) for i in range(0, s.shape[0], rows)]
    while len(parts) > 1:
        parts = [jnp.maximum(a, b) for a, b in zip(parts[::2], parts[1::2])]
    return parts[0]


def _flash_kernel(*refs, n_maps, has_prefix, lam_init, hb, dq):
    refs = list(refs)
    q_ref, k_ref, vt_ref = refs[:3]
    pos = 3
    if has_prefix:
        kc_ref, vct_ref = refs[pos:pos + 2]
        pos += 2
    if n_maps == 2:
        lam_ref, ng_ref = refs[pos:pos + 2]
        pos += 2
    o_ref = refs[pos]
    scratch = refs[pos + 1:]
    if n_maps == 2:
        qm_ref, m_ref, acc_ref = scratch
    else:
        m_ref, acc_ref = scratch
    kv = pl.program_id(3)

    def scores(ci, k_src):
        hd = ci // n_maps
        q = qm_ref[ci] if n_maps == 2 else q_ref[0, :, hd * dq:(hd + 1) * dq]
        return _dot_nt(k_src[0, :, hd * dq:(hd + 1) * dq], q)

    def update(ci, s, vt_src):
        hd = ci // n_maps
        vt = vt_src[0, hd * LANES:(hd + 1) * LANES, :]
        m_prev = m_ref[ci]
        m_new = jnp.maximum(m_prev, _col_max(s))
        alpha = jnp.exp2(m_prev - m_new)
        p = jnp.exp2(s - m_new).astype(BF16)
        v1 = jnp.concatenate([vt, jnp.ones((ONES_ROWS, vt.shape[1]), BF16)], axis=0)
        acc_ref[ci] = acc_ref[ci] * alpha + _dot(v1, p)
        m_ref[ci] = m_new

    def all_steps(k_src, vt_src):
        n_chains = hb * n_maps
        pending = [scores(ci, k_src) for ci in range(min(LOOKAHEAD, n_chains))]
        for ci in range(n_chains):
            if ci + LOOKAHEAD < n_chains:
                pending.append(scores(ci + LOOKAHEAD, k_src))
            update(ci, pending.pop(0), vt_src)

    @pl.when(kv == 0)
    def _init():
        m_ref[...] = jnp.full(m_ref.shape, NEG_BIG, F32)
        acc_ref[...] = jnp.zeros(acc_ref.shape, F32)
        if n_maps == 2:
            for hd in range(hb):
                q = q_ref[0, :, hd * dq:(hd + 1) * dq]
                lane = lax.broadcasted_iota(jnp.int32, q.shape, 1)
                qm_ref[2 * hd] = jnp.where(lane < DIFF_HEAD_DIM, q, jnp.zeros_like(q))
                qm_ref[2 * hd + 1] = jnp.where(lane >= DIFF_HEAD_DIM, q, jnp.zeros_like(q))
        if has_prefix:
            all_steps(kc_ref, vct_ref)

    all_steps(k_ref, vt_ref)

    @pl.when(kv == pl.num_programs(3) - 1)
    def _finish():
        if n_maps == 2:
            lf = lam_ref[...]
            l1 = jnp.sum(lf[0:1] * lf[1:2], axis=1, keepdims=True)
            l2 = jnp.sum(lf[2:3] * lf[3:4], axis=1, keepdims=True)
            lam = jnp.exp(l1) - jnp.exp(l2) + lam_init
        for hd in range(hb):
            a0 = acc_ref[hd * n_maps]
            o = a0[:LANES] / a0[LANES:LANES + 1]
            if n_maps == 2:
                a1 = acc_ref[hd * n_maps + 1]
                o = o - lam * (a1[:LANES] / a1[LANES:LANES + 1])
                ms = jnp.mean(o * o, axis=0, keepdims=True)
                o = o * lax.rsqrt(ms + NORM_EPS) * ng_ref[...] * (1.0 - lam_init)
            o_ref[0, :, hd * LANES:(hd + 1) * LANES] = o.T.astype(o_ref.dtype)


def _flash(q, k, vt, prefix, extras, *, heads, dq, tq, tk, lam_init=0.0, hb=1):
    b, s, _ = q.shape
    n_maps = 2 if extras is not None else 1
    in_specs = [
        pl.BlockSpec((1, tq, hb * dq), lambda bi, h, qi, kv: (bi, qi, h)),
        pl.BlockSpec((1, tk, hb * dq), lambda bi, h, qi, kv: (bi, kv, h)),
        pl.BlockSpec((1, hb * LANES, tk), lambda bi, h, qi, kv: (bi, h, kv)),
    ]
    args = [q, k, vt]
    if prefix is not None:
        cl = prefix[0].shape[1]
        in_specs += [
            pl.BlockSpec((1, cl, hb * dq), lambda bi, h, qi, kv: (bi, 0, h)),
            pl.BlockSpec((1, hb * LANES, cl), lambda bi, h, qi, kv: (bi, h, 0)),
        ]
        args += list(prefix)
    scratch = []
    if n_maps == 2:
        in_specs += [_const_spec(extras[0].shape), _const_spec(extras[1].shape)]
        args += list(extras)
        scratch.append(pltpu.VMEM((2 * hb, tq, dq), BF16))
    scratch += [pltpu.VMEM((hb * n_maps, 1, tq), F32), pltpu.VMEM((hb * n_maps, LANES + ONES_ROWS, tq), F32)]
    return pl.pallas_call(
        functools.partial(_flash_kernel, n_maps=n_maps, has_prefix=prefix is not None, lam_init=lam_init,
                          hb=hb, dq=dq),
        grid=(b, heads // hb, s // tq, k.shape[1] // tk),
        in_specs=in_specs,
        out_specs=pl.BlockSpec((1, tq, hb * LANES), lambda bi, h, qi, kv: (bi, qi, h)),
        out_shape=jax.ShapeDtypeStruct((b, s, heads * LANES), BF16),
        scratch_shapes=scratch,
        compiler_params=_params("parallel", "parallel", "parallel", "arbitrary"),
        name="flash_diff" if n_maps == 2 else "flash_mla",
    )(*args)


def _gla_proj_kernel(x_ref, mod_ref, g_ref, w_ref, wr_ref, wg_ref, bg_ref, q_ref, k_ref, v_ref, og_ref, gate_ref):
    h = _norm_mod(x_ref[0], g_ref[...], mod_ref[0, 0:1, :], mod_ref[0, 1:2, :]).astype(BF16)
    col = 0
    for o_ref, width, scale in ((q_ref, GLA_KEY, GLA_DK ** -0.5), (k_ref, GLA_KEY, None),
                                (v_ref, GLA_VAL, None), (og_ref, GLA_VAL, None)):
        for c in range(width // MXU_TILE):
            y = _dot(h, w_ref[:, col + c * MXU_TILE: col + (c + 1) * MXU_TILE])
            if scale is not None:
                y = y * scale
            o_ref[0, :, c * MXU_TILE:(c + 1) * MXU_TILE] = y.astype(BF16)
        col += width
    r_hi, r_lo = _split_bf16(_dot(h, wr_ref[...]))
    for c in range(2 * GLA_KEY // MXU_TILE):
        wg = wg_ref[:, c * MXU_TILE:(c + 1) * MXU_TILE]
        z = _dot(r_hi, wg) + _dot(r_lo, wg) + bg_ref[:, c * MXU_TILE:(c + 1) * MXU_TILE]
        log_sig = jnp.minimum(z, 0.0) - jnp.log(1.0 + jnp.exp(-jnp.abs(z)))
        gate_ref[0, :, c * MXU_TILE:(c + 1) * MXU_TILE] = log_sig / GLA_GATE_NORM


def _gla_proj(x, mod, mod_row, norm_g, w, wr, wg, bg, tm):
    bx, s, d = x.shape
    x_spec, mod_spec = _token_specs(tm, d, mod_row)
    return pl.pallas_call(
        _gla_proj_kernel,
        grid=(bx, s // tm),
        in_specs=[x_spec, mod_spec, _const_spec((1, d)), _const_spec(w.shape), _const_spec(wr.shape),
                  _const_spec(wg.shape), _const_spec(bg.shape)],
        out_specs=[_out_spec(tm, GLA_KEY), _out_spec(tm, GLA_KEY), _out_spec(tm, GLA_VAL),
                   _out_spec(tm, GLA_VAL), _out_spec(tm, 2 * GLA_KEY)],
        out_shape=[jax.ShapeDtypeStruct((bx, s, GLA_KEY), BF16), jax.ShapeDtypeStruct((bx, s, GLA_KEY), BF16),
                   jax.ShapeDtypeStruct((bx, s, GLA_VAL), BF16), jax.ShapeDtypeStruct((bx, s, GLA_VAL), BF16),
                   jax.ShapeDtypeStruct((bx, s, 2 * GLA_KEY), F32)],
        compiler_params=_params("parallel", "parallel"),
        name="gla_proj",
    )(x, mod, norm_g, w, wr, wg, bg)


def _gla_scan_kernel(q_ref, k_ref, v_ref, g_ref, s0_ref, o_ref, sfin_ref, st_ref):
    direction = pl.program_id(1)
    i = pl.program_id(2)
    n_chunks = q_ref.shape[1] // GLA_CHUNK

    @pl.when(i == 0)
    def _load_state():
        st_ref[...] = s0_ref[0, 0]

    def run(backward):
        tb = q_ref.shape[1]
        row = lax.broadcasted_iota(jnp.int32, (tb, tb), 0)
        colm = lax.broadcasted_iota(jnp.int32, (tb, tb), 1)
        same = (row // GLA_CHUNK) == (colm // GLA_CHUNK)
        seen = (colm >= row) if backward else (colm <= row)
        seen_m = jnp.where(same & seen, 1.0, 0.0).astype(BF16)
        rest_m = jnp.where(same & jnp.logical_not(seen), 1.0, 0.0).astype(BF16)
        g_hi, g_lo = _split_bf16(g_ref[0])
        cum = _dot(seen_m, g_hi) + _dot(seen_m, g_lo)
        rest = _dot(rest_m, g_hi) + _dot(rest_m, g_lo)
        q = q_ref[0].astype(F32)
        k = k_ref[0].astype(F32)
        q_dec = (q * jnp.exp(cum)).astype(BF16)
        k_inv = (k * jnp.exp(-cum)).astype(BF16)
        k_end = (k * jnp.exp(rest)).astype(BF16)
        decay = jnp.exp(cum + rest)
        within = same & seen
        chunk_of_row = lax.broadcasted_iota(jnp.int32, (tb, GLA_DK), 0) // GLA_CHUNK
        order = list(range(n_chunks - 1, -1, -1) if backward else range(n_chunks))
        for h in range(GLA_HEADS):
            ksl = slice(h * GLA_DK, (h + 1) * GLA_DK)
            vsl = slice(h * GLA_DV, (h + 1) * GLA_DV)
            v = v_ref[0, :, vsl]
            qd = q_dec[:, ksl]
            a = jnp.where(within, _dot_nt(qd, k_inv[:, ksl]), 0.0)
            o_intra = _dot(a.astype(BF16), v)
            ke = k_end[:, ksl]
            ke_wide = jnp.concatenate([jnp.where(chunk_of_row == c, ke, jnp.zeros_like(ke)) for c in range(n_chunks)],
                                      axis=1)
            inc = _dot_tn(v, ke_wide)
            st = st_ref[h]
            for c in order:
                sl = slice(c * GLA_CHUNK, (c + 1) * GLA_CHUNK)
                o_ref[0, 0, sl, vsl] = (o_intra[sl] + _dot_nt(qd[sl], st.astype(BF16))).astype(o_ref.dtype)
                st = st * decay[c * GLA_CHUNK:c * GLA_CHUNK + 1, ksl] + inc[:, c * GLA_DK:(c + 1) * GLA_DK]
            st_ref[h] = st

    pl.when(direction == 0)(lambda: run(False))
    pl.when(direction == 1)(lambda: run(True))

    @pl.when(i == pl.num_programs(2) - 1)
    def _store_state():
        sfin_ref[0, 0] = st_ref[...]


def _gla_scan(q, k, v, g, s0, tb):
    b, s, _ = q.shape
    nb = s // tb

    def blk(d, i):
        return i + d * (nb - 1 - 2 * i)

    st_spec = pl.BlockSpec((1, 1, GLA_HEADS, GLA_DV, GLA_DK), lambda bi, d, i: (bi, d, 0, 0, 0))
    return pl.pallas_call(
        _gla_scan_kernel,
        grid=(b, 2, nb),
        in_specs=[
            pl.BlockSpec((1, tb, GLA_KEY), lambda bi, d, i: (bi, blk(d, i), 0)),
            pl.BlockSpec((1, tb, GLA_KEY), lambda bi, d, i: (bi, blk(d, i), 0)),
            pl.BlockSpec((1, tb, GLA_VAL), lambda bi, d, i: (bi, blk(d, i), 0)),
            pl.BlockSpec((1, tb, GLA_KEY), lambda bi, d, i: (bi, blk(d, i), d)),
            st_spec,
        ],
        out_specs=[pl.BlockSpec((1, 1, tb, GLA_VAL), lambda bi, d, i: (d, bi, blk(d, i), 0)), st_spec],
        out_shape=[jax.ShapeDtypeStruct((2, b, s, GLA_VAL), BF16),
                   jax.ShapeDtypeStruct((b, 2, GLA_HEADS, GLA_DV, GLA_DK), F32)],
        scratch_shapes=[pltpu.VMEM((GLA_HEADS, GLA_DV, GLA_DK), F32)],
        compiler_params=_params("parallel", "parallel", "arbitrary"),
        name="gla_scan",
    )(q, k, v, g, s0)


def _mla_proj_kernel(*refs, rope):
    if rope:
        (x_ref, mod_ref, g_ref, w_ref, qg_ref, wq_ref, kvg_ref, wkn_ref, wvt_ref,
         cos_ref, sa_ref, sb_ref, q_ref, k_ref, vt_ref) = refs
        cos, sa, sb = cos_ref[...], sa_ref[...], sb_ref[...]
    else:
        x_ref, mod_ref, g_ref, w_ref, qg_ref, wq_ref, kvg_ref, wkn_ref, wvt_ref, q_ref, k_ref, vt_ref = refs
    h = _norm_mod(x_ref[0], g_ref[...], mod_ref[0, 0:1, :], mod_ref[0, 1:2, :]).astype(BF16)
    y = _dot(h, w_ref[...])
    c_q = _rms(y[:, :MLA_Q_LORA], qg_ref[...]).astype(BF16)
    c_kv = _rms(y[:, MLA_Q_LORA:MLA_Q_LORA + MLA_KV_LORA], kvg_ref[...]).astype(BF16)
    k_rope = y[:, MLA_Q_LORA + MLA_KV_LORA:]
    if rope:
        k_rope = _rope_chunk(k_rope, cos, sa, sb)
    k_rope = k_rope.astype(BF16)
    for hd in range(MLA_HEADS):
        base = hd * MLA_QK_PAD
        qh = _dot(c_q, wq_ref[:, base: base + MLA_QK_PAD])
        q_rope = qh[:, LANES:]
        if rope:
            q_rope = _rope_chunk(q_rope, cos, sa, sb)
        q_ref[0, :, base: base + LANES] = (qh[:, :LANES] * (MLA_SCALE * LOG2E)).astype(BF16)
        q_ref[0, :, base + LANES: base + MLA_QK_PAD] = (q_rope * (MLA_SCALE * LOG2E)).astype(BF16)
        k_ref[0, :, base + LANES: base + MLA_QK_PAD] = k_rope
    vt_ref[0] = _dot_nt(wvt_ref[...], c_kv).astype(BF16)
    for c in range(MLA_HEADS * MLA_NOPE // MXU_TILE):
        kn = _dot(c_kv, wkn_ref[:, c * MXU_TILE:(c + 1) * MXU_TILE]).astype(BF16)
        for s in range(2):
            hd = 2 * c + s
            k_ref[0, :, hd * MLA_QK_PAD: hd * MLA_QK_PAD + LANES] = kn[:, s * LANES:(s + 1) * LANES]


def _mla_proj(x, mod, mod_row, norm_g, w, qg, wq, kvg, wkn, wvt, tables, tm):
    bx, s, d = x.shape
    rope = tables is not None
    x_spec, mod_spec = _token_specs(tm, d, mod_row)
    consts = [norm_g, w, qg, wq, kvg, wkn, wvt]
    in_specs = [x_spec, mod_spec] + [_const_spec(a.shape) for a in consts]
    args = [x, mod] + consts
    if rope:
        in_specs += _rope_specs(tm)
        args += list(tables)
    qk_w = MLA_HEADS * MLA_QK_PAD
    return pl.pallas_call(
        functools.partial(_mla_proj_kernel, rope=rope),
        grid=(bx, s // tm),
        in_specs=in_specs,
        out_specs=[_out_spec(tm, qk_w), _out_spec(tm, qk_w), _vt_spec(tm, MLA_HEADS * MLA_V)],
        out_shape=[jax.ShapeDtypeStruct((bx, s, qk_w), BF16), jax.ShapeDtypeStruct((bx, s, qk_w), BF16),
                   jax.ShapeDtypeStruct((bx, MLA_HEADS * MLA_V, s), BF16)],
        compiler_params=_params("parallel", "parallel"),
        name="mla_proj",
    )(*args)


def _top2_sum(a, b, c, d):
    hi1, lo1 = jnp.maximum(a, b), jnp.minimum(a, b)
    hi2, lo2 = jnp.maximum(c, d), jnp.minimum(c, d)
    return jnp.maximum(hi1, hi2) + jnp.maximum(jnp.minimum(hi1, hi2), jnp.maximum(lo1, lo2))


def _route_rows(scores, sel):
    eg = EXPERTS_PER_GROUP
    gs = [_top2_sum(*sel[g * eg:(g + 1) * eg]) for g in range(N_GROUPS)]
    best = jnp.maximum(jnp.maximum(gs[0], gs[1]), jnp.maximum(gs[2], gs[3]))
    gidx = jnp.where(gs[0] >= best, 0, jnp.where(gs[1] >= best, 1, jnp.where(gs[2] >= best, 2, 3)))

    def pick(rows, j):
        out = rows[(N_GROUPS - 1) * eg + j]
        for g in range(N_GROUPS - 2, -1, -1):
            out = jnp.where(gidx == g, rows[g * eg + j], out)
        return out

    loc_sel = [pick(sel, j) for j in range(eg)]
    loc_sc = [pick(scores, j) for j in range(eg)]
    weights = []
    for j in range(eg):
        rank = jnp.zeros_like(loc_sel[j])
        for i in range(eg):
            if i == j:
                continue
            beats = (loc_sel[i] >= loc_sel[j]) if i < j else (loc_sel[i] > loc_sel[j])
            rank = rank + jnp.where(beats, 1.0, 0.0)
        weights.append(jnp.where(rank < 1.5, loc_sc[j], 0.0))
    den = weights[0] + weights[1] + weights[2] + weights[3]
    weights = [w / den for w in weights]
    return [jnp.where(gidx == e // eg, weights[e % eg], 0.0) for e in range(N_EXPERTS)], gidx


def _mixer_output(pre, refs):
    if pre == "gla":
        of_ref, ob_ref, og_ref, ng_ref, w_ref = refs
        o = of_ref[0, 0].astype(F32) + ob_ref[0, 0].astype(F32)
        parts = []
        for h in range(GLA_HEADS):
            sl = slice(h * GLA_DV, (h + 1) * GLA_DV)
            parts.append((_rms(o[:, sl], ng_ref[...]) * _silu(og_ref[0, :, sl].astype(F32))).astype(BF16))
        return _dot(jnp.concatenate(parts, axis=1), w_ref[...])
    o_ref, w_ref = refs
    return _dot(o_ref[0], w_ref[...])


def _moe_kernel(*refs, pre, final_norm):
    n_pre = 5 if pre == "gla" else 2
    (x_ref, mod_ref, g_ref, rw_ref, rb_ref, shu_ref, shd_ref, tri_ref, wup_ref, wdn_ref, fg_ref,
     out_ref, perm_ref, hs_ref, cs_ref, ys_ref) = refs[n_pre:]
    x = x_ref[0] + mod_ref[0, 2:3, :] * _mixer_output(pre, refs[:n_pre])
    tm = x.shape[0]
    n_rows = hs_ref.shape[0]
    h, h_lo = _split_bf16(_norm_mod(x, g_ref[...], mod_ref[0, 3:4, :], mod_ref[0, 4:5, :]))
    both = _dot_nt(rw_ref[...], h)
    logits = both[:N_EXPERTS] + both[N_EXPERTS:] + _dot_nt(rw_ref[:N_EXPERTS, :], h_lo)
    scores = jax.nn.sigmoid(logits)
    sel = scores + rb_ref[...]
    rows, gidx = _route_rows([scores[e:e + 1, :] for e in range(N_EXPERTS)],
                             [sel[e:e + 1, :] for e in range(N_EXPERTS)])
    comb = jnp.concatenate(rows + [jnp.zeros((LANES - N_EXPERTS, tm), F32)], axis=0).T

    member = [jnp.where(gidx == g, 1.0, 0.0) for g in range(N_GROUPS)]
    member_m = jnp.concatenate(member + [jnp.zeros((8 - N_GROUPS, tm), F32)], axis=0).astype(BF16)
    rank = _dot(member_m, tri_ref[...])
    dest = jnp.zeros((1, tm), F32)
    start = jnp.zeros((1, 1), F32)
    seg_start, seg_count = [], []
    for g in range(N_GROUPS):
        count = jnp.sum(member[g], axis=1, keepdims=True)
        dest = dest + member[g] * (start + rank[g:g + 1] - 1.0)
        seg_start.append(start[0, 0].astype(jnp.int32))
        seg_count.append(count[0, 0].astype(jnp.int32))
        start = jnp.floor((start + count + (MOE_ALIGN - 1)) * (1.0 / MOE_ALIGN)) * MOE_ALIGN
    row_id = lax.broadcasted_iota(jnp.int32, (n_rows, tm), 0)
    perm = jnp.where(row_id == dest.astype(jnp.int32), 1.0, 0.0).astype(BF16)
    perm_ref[...] = perm
    c_hi, c_lo = _split_bf16(comb)
    moved = _dot(perm, jnp.concatenate([h, c_hi, c_lo], axis=1))
    d = h.shape[1]
    hs_ref[...] = moved[:, :d].astype(BF16)
    cs_ref[...] = moved[:, d:d + LANES] + moved[:, d + LANES:]
    ys_ref[...] = jnp.zeros(ys_ref.shape, F32)
    a = _dot(h, shu_ref[:, :SHARED_FF])
    u = _dot(h, shu_ref[:, SHARED_FF:])
    out_ref[0] = x + mod_ref[0, 5:6, :] * _dot((_silu(a) * u).astype(BF16), shd_ref[...])

    def run_group(grp, size, n_passes):
        def one_pass(i, carry):
            r0 = pl.multiple_of(seg_start[grp] + i * size, MOE_ALIGN)
            hc = hs_ref[pl.ds(r0, size), :]
            cc = cs_ref[pl.ds(r0, size), :]
            acts = []
            for j in range(EXPERTS_PER_GROUP):
                e = grp * EXPERTS_PER_GROUP + j
                a = _dot(hc, wup_ref[0, e, :, :EXPERT_FF])
                u = _dot(hc, wup_ref[0, e, :, EXPERT_FF:])
                scale = jnp.broadcast_to(cc[:, e:e + 1], a.shape)
                acts.append((_silu(a) * u * scale).astype(BF16))
            w_dn = wdn_ref[0, grp * EXPERTS_PER_GROUP:(grp + 1) * EXPERTS_PER_GROUP]
            y = _dot(jnp.concatenate(acts, axis=1), w_dn.reshape(EXPERTS_PER_GROUP * EXPERT_FF, -1))
            ys_ref[pl.ds(r0, size), :] = ys_ref[pl.ds(r0, size), :] + y
            return carry

        lax.fori_loop(0, n_passes, one_pass, 0)

    for grp in range(N_GROUPS):
        n = seg_count[grp]
        fits = n <= MOE_CHUNK_SMALL
        run_group(grp, MOE_CHUNK_SMALL, jnp.where(fits, jnp.minimum(n, 1), 0))
        run_group(grp, MOE_CHUNK_BIG, jnp.where(fits, 0, (n + (MOE_CHUNK_BIG - 1)) // MOE_CHUNK_BIG))

    out = out_ref[0] + mod_ref[0, 5:6, :] * _dot_tn(perm_ref[...], ys_ref[...].astype(BF16))
    if final_norm:
        out = _rms(out, fg_ref[...])
    out_ref[0] = out


def _moe(mixer, x, mod, mod_row, norm_g, rw, rb, sh_up, sh_dn, w_up, w_dn, layer, final_g, final_norm, tm):
    bx, s, d = x.shape
    x_spec, mod_spec = _token_specs(tm, d, mod_row)
    pre = mixer[0]
    if pre == "gla":
        o2, og, ng, w_out = mixer[1:]
        o2 = o2.reshape(2, bx, s, GLA_VAL)
        pre_specs = [pl.BlockSpec((1, 1, tm, GLA_VAL), lambda b, t: (0, b, t, 0)),
                     pl.BlockSpec((1, 1, tm, GLA_VAL), lambda b, t: (1, b, t, 0)),
                     _out_spec(tm, GLA_VAL), _const_spec(ng.shape, True), _const_spec(w_out.shape, True)]
        pre_args = [o2, o2, og.reshape(bx, s, GLA_VAL), ng, w_out]
    else:
        o, w_out = mixer[1:]
        o = o.reshape(bx, s, -1)
        pre_specs = [_out_spec(tm, o.shape[2]), _const_spec(w_out.shape, True)]
        pre_args = [o, w_out]
    tri = (jnp.arange(tm)[:, None] <= jnp.arange(tm)[None, :]).astype(BF16)
    consts = [norm_g, rw, rb, sh_up, sh_dn, tri]
    n_rows = -(-(tm + (N_GROUPS - 1) * (MOE_ALIGN - 1) + MOE_CHUNK_BIG) // MXU_TILE) * MXU_TILE
    resident = pl.Buffered(1)
    in_specs = (pre_specs + [x_spec, mod_spec] + [_const_spec(a.shape, True) for a in consts] + [
        pl.BlockSpec((1,) + w_up.shape[1:], lambda b, t: (layer, 0, 0, 0), pipeline_mode=resident),
        pl.BlockSpec((1,) + w_dn.shape[1:], lambda b, t: (layer, 0, 0, 0), pipeline_mode=resident),
        _const_spec(final_g.shape, True)])
    return pl.pallas_call(
        functools.partial(_moe_kernel, pre=pre, final_norm=final_norm),
        grid=(bx, s // tm),
        in_specs=in_specs,
        out_specs=x_spec,
        out_shape=jax.ShapeDtypeStruct(x.shape, F32),
        scratch_shapes=[pltpu.VMEM((n_rows, tm), BF16), pltpu.VMEM((n_rows, d), BF16),
                        pltpu.VMEM((n_rows, LANES), F32), pltpu.VMEM((n_rows, d), F32)],
        compiler_params=_params("parallel", "parallel"),
        name="moe",
    )(*pre_args, x, mod, *consts, w_up, w_dn, final_g)


def _diff_weights(w_in):
    d = w_in.shape[0]

    def regroup(w):
        return w.reshape(d, 2, DIFF_HEADS, DIFF_HEAD_DIM).transpose(0, 2, 1, 3).reshape(d, DIFF_QK)

    w_qk = jnp.concatenate([regroup(w_in[:, :DIFF_QK]), regroup(w_in[:, DIFF_QK:2 * DIFF_QK])], axis=1)
    return w_qk.astype(BF16), w_in[:, 2 * DIFF_QK:].T.astype(BF16)


def _gla_weights(w_in, gate_w, gate_b):
    d = w_in.shape[0]
    main = 2 * GLA_KEY + 2 * GLA_VAL
    w_r = jnp.zeros((d, LANES), F32).at[:, :2 * GLA_GATE_RANK].set(w_in[:, main:]).astype(BF16)
    w_g = jnp.zeros((LANES, 2 * GLA_KEY), F32)
    w_g = w_g.at[:GLA_GATE_RANK, :GLA_KEY].set(gate_w[0])
    w_g = w_g.at[GLA_GATE_RANK:2 * GLA_GATE_RANK, GLA_KEY:].set(gate_w[1]).astype(BF16)
    return w_in[:, :main].astype(BF16), w_r, w_g, gate_b.reshape(1, 2 * GLA_KEY)


def _mla_weights(w_in, w_uq, w_ukv):
    d = w_in.shape[0]
    w = jnp.zeros((d, MLA_Q_LORA + MLA_KV_LORA + LANES), F32).at[:, :w_in.shape[1]].set(w_in).astype(BF16)
    wq = w_uq.reshape(MLA_Q_LORA, MLA_HEADS, MLA_NOPE + MLA_ROPE)
    wq = jnp.pad(wq, ((0, 0), (0, 0), (0, MLA_QK_PAD - MLA_NOPE - MLA_ROPE)))
    wq = wq.reshape(MLA_Q_LORA, MLA_HEADS * MLA_QK_PAD).astype(BF16)
    wkv = w_ukv.reshape(MLA_KV_LORA, MLA_HEADS, MLA_NOPE + MLA_V)
    wkn = wkv[:, :, :MLA_NOPE].reshape(MLA_KV_LORA, MLA_HEADS * MLA_NOPE).astype(BF16)
    wvt = wkv[:, :, MLA_NOPE:].reshape(MLA_KV_LORA, MLA_HEADS * MLA_V).T.astype(BF16)
    return w, wq, wkn, wvt


def _ctx_views(qkv, b, cl):
    q, k, vt = qkv
    vt = vt.reshape(vt.shape[1], b, cl).transpose(1, 0, 2)
    return q.reshape(b, cl, -1), k.reshape(b, cl, -1), vt


def kernel(x, c, ctx, c_ctx, ada_w, ada_b, norm_g, router_w, router_b, moe_w_up, moe_w_down, shared_w_up,
           shared_w_down, diff_w_in, diff_lam, diff_norm_g, diff_w_out, gla_w_in, gla_gate_w, gla_gate_b,
           gla_norm_g, gla_w_out, mla_w_in, mla_q_norm_g, mla_w_uq, mla_kv_norm_g, mla_w_ukv, mla_w_out,
           final_norm_g):
    b, s, d = x.shape
    cl = ctx.shape[1]
    assert b + 1 <= COND_ROWS and d == D_MODEL
    depth = ada_w.shape[0]
    ctx_row = b

    cond = jnp.zeros((COND_ROWS, d), F32).at[:b].set(c).at[ctx_row].set(c_ctx)
    mods = _ada_params(cond, ada_w, ada_b).reshape(depth, COND_ROWS, N_MOD, d)

    tables = _rope_tables(s)
    rw = jnp.concatenate(_split_bf16(router_w.T), axis=0)
    rb = router_b.reshape(N_EXPERTS, 1)
    w_up_all = moe_w_up.astype(BF16)
    w_dn_all = moe_w_down.astype(BF16)
    final_g = final_norm_g.reshape(1, d)

    tm_lat = TOKEN_TILE
    n_ctx = b * cl
    tm_ctx = TOKEN_TILE if n_ctx % TOKEN_TILE == 0 else cl
    assert s % FLASH_TK == 0 and s % TOKEN_TILE == 0 and s % GLA_BLOCK == 0 and cl % GLA_CHUNK == 0
    x_lat = x
    x_ctx = ctx.reshape(1, n_ctx, d)

    for i in range(depth):
        need_ctx = i < depth - 1
        mod = mods[i]
        g1 = norm_g[i, 0].reshape(1, d)
        g2 = norm_g[i, 1].reshape(1, d)
        kind, j = i % N_MIXERS, i // N_MIXERS
        if kind == 0:
            lam_init = 0.8 - 0.6 * math.exp(-0.3 * i)
            w, wvt = _diff_weights(diff_w_in[j])
            w_out = diff_w_out[j].astype(BF16)
            extras = (diff_lam[j], diff_norm_g[j].reshape(2 * DIFF_HEAD_DIM, 1))
            q_l, k_l, v_l = _diff_proj(x_lat, mod, None, g1, w, wvt, tables, tm_lat)
            q_c, k_c, v_c = _ctx_views(_diff_proj(x_ctx, mod, ctx_row, g1, w, wvt, None, tm_ctx), b, cl)
            o_lat = _flash(q_l, k_l, v_l, (k_c, v_c), extras, heads=DIFF_HEADS, dq=2 * DIFF_HEAD_DIM,
                           tq=FLASH_TQ, tk=FLASH_TK, lam_init=lam_init, hb=DIFF_HEADS)
            mix_lat = ("plain", o_lat, w_out)
            if need_ctx:
                o_ctx = _flash(q_c, k_c, v_c, None, extras, heads=DIFF_HEADS, dq=2 * DIFF_HEAD_DIM,
                               tq=cl, tk=cl, lam_init=lam_init, hb=DIFF_HEADS)
                mix_ctx = ("plain", o_ctx, w_out)
        elif kind == 1:
            w, w_r, w_g, b_g = _gla_weights(gla_w_in[j], gla_gate_w[j], gla_gate_b[j])
            w_out = gla_w_out[j].astype(BF16)
            ng = gla_norm_g[j].reshape(1, GLA_DV)
            q_l, k_l, v_l, og_l, gt_l = _gla_proj(x_lat, mod, None, g1, w, w_r, w_g, b_g, tm_lat)
            q_c, k_c, v_c, og_c, gt_c = (t.reshape(b, cl, -1)
                                         for t in _gla_proj(x_ctx, mod, ctx_row, g1, w, w_r, w_g, b_g, tm_ctx))
            s0 = jnp.zeros((b, 2, GLA_HEADS, GLA_DV, GLA_DK), F32)
            o_c, s_c = _gla_scan(q_c, k_c, v_c, gt_c, s0, cl)
            o_l, _ = _gla_scan(q_l, k_l, v_l, gt_l, s_c, GLA_BLOCK)
            mix_lat = ("gla", o_l, og_l, ng, w_out)
            if need_ctx:
                mix_ctx = ("gla", o_c, og_c, ng, w_out)
        else:
            w, wq, wkn, wvt = _mla_weights(mla_w_in[j], mla_w_uq[j], mla_w_ukv[j])
            w_out = mla_w_out[j].astype(BF16)
            qg = mla_q_norm_g[j].reshape(1, MLA_Q_LORA)
            kvg = mla_kv_norm_g[j].reshape(1, MLA_KV_LORA)
            q_l, k_l, v_l = _mla_proj(x_lat, mod, None, g1, w, qg, wq, kvg, wkn, wvt, tables, tm_lat)
            q_c, k_c, v_c = _ctx_views(_mla_proj(x_ctx, mod, ctx_row, g1, w, qg, wq, kvg, wkn, wvt, None, tm_ctx),
                                       b, cl)
            o_lat = _flash(q_l, k_l, v_l, (k_c, v_c), None, heads=MLA_HEADS, dq=MLA_QK_PAD, tq=FLASH_TQ, tk=FLASH_TK,
                           hb=MLA_HEADS)
            mix_lat = ("plain", o_lat, w_out)
            if need_ctx:
                o_ctx = _flash(q_c, k_c, v_c, None, None, heads=MLA_HEADS, dq=MLA_QK_PAD, tq=cl, tk=cl,
                               hb=MLA_HEADS)
                mix_ctx = ("plain", o_ctx, w_out)

        sh_up = shared_w_up[i].astype(BF16)
        sh_dn = shared_w_down[i].astype(BF16)
        last = i == depth - 1
        x_lat = _moe(mix_lat, x_lat, mod, None, g2, rw, rb, sh_up, sh_dn, w_up_all, w_dn_all, i, final_g, last,
                     tm_lat)
        if need_ctx:
            x_ctx = _moe(mix_ctx, x_ctx, mod, ctx_row, g2, rw, rb, sh_up, sh_dn, w_up_all, w_dn_all, i, final_g,
                         False, tm_ctx)
    return x_lat
```

```python
import functools
import math

import jax
import jax.numpy as jnp
from jax import lax
from jax.experimental import pallas as pl
from jax.experimental.pallas import tpu as pltpu

F32 = jnp.float32
BF16 = jnp.bfloat16

D_MODEL = 1024
DEPTH = 4
GRID_W = 64
N_MIXERS = 3
NORM_EPS = 1e-6
ROPE_BASE = 10000.0
N_MOD = 6

DIFF_HEADS = 8
DIFF_HEAD_DIM = 64
DIFF_QK = 2 * DIFF_HEADS * DIFF_HEAD_DIM
DIFF_V = 2 * DIFF_HEADS * DIFF_HEAD_DIM

GLA_HEADS = 4
GLA_KEY = D_MODEL // 2
GLA_VAL = D_MODEL
GLA_DK = GLA_KEY // GLA_HEADS
GLA_DV = GLA_VAL // GLA_HEADS
GLA_GATE_RANK = 16
GLA_GATE_NORM = 16.0
GLA_CHUNK = 64

MLA_HEADS = 8
MLA_Q_LORA = 384
MLA_KV_LORA = 256
MLA_NOPE = 128
MLA_ROPE = 64
MLA_V = 128
MLA_SCALE = (MLA_NOPE + MLA_ROPE) ** -0.5
MLA_QK_PAD = 256

N_EXPERTS = 16
N_GROUPS = 4
EXPERTS_PER_GROUP = N_EXPERTS // N_GROUPS
EXPERT_FF = 256
SHARED_FF = 256

LANES = 128
BF16_ROWS = 16
MXU_TILE = 256
TOKEN_TILE = 512
ADA_COL_TILE = 1024
FLASH_TQ = 512
FLASH_TK = 1024
LOOKAHEAD = 2
ONES_ROWS = BF16_ROWS
GLA_BLOCK = 256
MOE_CHUNK_SMALL = 128
MOE_CHUNK_BIG = 208
MOE_ALIGN = BF16_ROWS
COND_ROWS = 16
NEG_BIG = -1e30
LOG2E = math.log2(math.e)
VMEM_LIMIT = 56 * 1024 * 1024

NT_DIMS = (((1,), (1,)), ((), ()))
TN_DIMS = (((0,), (0,)), ((), ()))


def _dot(a, b):
    return jnp.dot(a, b, preferred_element_type=F32)


def _dot_nt(a, b):
    return lax.dot_general(a, b, NT_DIMS, preferred_element_type=F32)


def _dot_tn(a, b):
    return lax.dot_general(a, b, TN_DIMS, preferred_element_type=F32)


def _split_bf16(a):
    hi = a.astype(BF16)
    lo = (a - hi.astype(F32)).astype(BF16)
    return hi, lo


def _silu(a):
    return a * jax.nn.sigmoid(a)


def _params(*sem):
    return pltpu.CompilerParams(dimension_semantics=sem, vmem_limit_bytes=VMEM_LIMIT)


def _const_spec(shape, single=False):
    nd = len(shape)
    if single:
        return pl.BlockSpec(shape, lambda *_: (0,) * nd, pipeline_mode=pl.Buffered(1))
    return pl.BlockSpec(shape, lambda *_: (0,) * nd)


def _ada_kernel(cond_ref, w_ref, b_ref, o_ref):
    a_hi, a_lo = _split_bf16(_silu(cond_ref[...]))
    w_hi, w_lo = _split_bf16(w_ref[0])
    o_ref[0] = _dot(a_hi, w_hi) + _dot(a_lo, w_hi) + _dot(a_hi, w_lo) + b_ref[0]


def _ada_params(cond, ada_w, ada_b):
    depth, d, n = ada_w.shape
    tn = ADA_COL_TILE
    return pl.pallas_call(
        _ada_kernel,
        grid=(depth, n // tn),
        in_specs=[
            pl.BlockSpec((COND_ROWS, d), lambda i, j: (0, 0)),
            pl.BlockSpec((1, d, tn), lambda i, j: (i, 0, j)),
            pl.BlockSpec((1, 1, tn), lambda i, j: (i, 0, j)),
        ],
        out_specs=pl.BlockSpec((1, COND_ROWS, tn), lambda i, j: (i, 0, j)),
        out_shape=jax.ShapeDtypeStruct((depth, COND_ROWS, n), F32),
        compiler_params=_params("parallel", "parallel"),
        name="ada_params",
    )(cond, ada_w, ada_b.reshape(depth, 1, n))


def _norm_mod(x, g, shift, scale):
    ms = jnp.mean(x * x, axis=-1, keepdims=True)
    return (x * lax.rsqrt(ms + NORM_EPS) * g) * (1.0 + scale) + shift


def _rms(x, g):
    ms = jnp.mean(x * x, axis=-1, keepdims=True)
    return x * lax.rsqrt(ms + NORM_EPS) * g


def _rope_chunk(c, cos, sa, sb):
    return c * cos + pltpu.roll(c, LANES - 16, 1) * sa + pltpu.roll(c, 16, 1) * sb


def _rope_tables(seq):
    t = jnp.arange(seq)
    pos_row = (t // GRID_W).astype(F32)
    pos_col = (t % GRID_W).astype(F32)
    inv = ROPE_BASE ** (-jnp.arange(0, 32, 2, dtype=F32) / 32)
    lane = jnp.arange(LANES)
    d = lane % 64
    r = d % 32
    first = (r < 16)[None, :]
    pos = jnp.where((d // 32)[None, :] == 0, pos_row[:, None], pos_col[:, None])
    ang = pos * inv[r % 16][None, :]
    cos, sin = jnp.cos(ang), jnp.sin(ang)
    return cos, jnp.where(first, -sin, 0.0), jnp.where(first, 0.0, sin)


def _token_specs(tm, d, mod_row):
    x_spec = pl.BlockSpec((1, tm, d), lambda b, t: (b, t, 0))
    if mod_row is None:
        mod_spec = pl.BlockSpec((1, N_MOD, d), lambda b, t: (b, 0, 0))
    else:
        mod_spec = pl.BlockSpec((1, N_MOD, d), lambda b, t: (mod_row, 0, 0))
    return x_spec, mod_spec


def _rope_specs(tm):
    return [pl.BlockSpec((tm, LANES), lambda b, t: (t, 0))] * 3


def _out_spec(tm, n):
    return pl.BlockSpec((1, tm, n), lambda b, t: (b, t, 0))


def _diff_proj_kernel(*refs, rope):
    if rope:
        x_ref, mod_ref, g_ref, w_ref, wvt_ref, cos_ref, sa_ref, sb_ref, q_ref, k_ref, vt_ref = refs
        cos, sa, sb = cos_ref[...], sa_ref[...], sb_ref[...]
    else:
        x_ref, mod_ref, g_ref, w_ref, wvt_ref, q_ref, k_ref, vt_ref = refs
    h = _norm_mod(x_ref[0], g_ref[...], mod_ref[0, 0:1, :], mod_ref[0, 1:2, :]).astype(BF16)
    for idx, o_ref in enumerate((q_ref, k_ref)):
        for c in range(DIFF_QK // MXU_TILE):
            y = _dot(h, w_ref[:, idx * DIFF_QK + c * MXU_TILE: idx * DIFF_QK + (c + 1) * MXU_TILE])
            for s in range(2):
                ys = y[:, s * LANES:(s + 1) * LANES]
                if rope:
                    ys = _rope_chunk(ys, cos, sa, sb)
                if idx == 0:
                    ys = ys * (DIFF_HEAD_DIM ** -0.5 * LOG2E)
                o_ref[0, :, c * MXU_TILE + s * LANES: c * MXU_TILE + (s + 1) * LANES] = ys.astype(BF16)
    vt_ref[0] = _dot_nt(wvt_ref[...], h).astype(BF16)


def _vt_spec(tm, n):
    return pl.BlockSpec((1, n, tm), lambda b, t: (b, 0, t))


def _diff_proj(x, mod, mod_row, norm_g, w, wvt, tables, tm):
    bx, s, d = x.shape
    rope = tables is not None
    x_spec, mod_spec = _token_specs(tm, d, mod_row)
    in_specs = [x_spec, mod_spec, _const_spec((1, d)), _const_spec(w.shape), _const_spec(wvt.shape)]
    args = [x, mod, norm_g, w, wvt]
    if rope:
        in_specs += _rope_specs(tm)
        args += list(tables)
    out = jax.ShapeDtypeStruct((bx, s, DIFF_QK), BF16)
    return pl.pallas_call(
        functools.partial(_diff_proj_kernel, rope=rope),
        grid=(bx, s // tm),
        in_specs=in_specs,
        out_specs=[_out_spec(tm, DIFF_QK), _out_spec(tm, DIFF_QK), _vt_spec(tm, DIFF_V)],
        out_shape=[out, out, jax.ShapeDtypeStruct((bx, DIFF_V, s), BF16)],
        compiler_params=_params("parallel", "parallel"),
        name="diff_proj",
    )(*args)


def _col_max(s, rows=64):
    parts = [jnp.max(s[i:i + rows], axis=0, keepdims=True) for i in range(0, s.shape[0], rows)]
    while len(parts) > 1:
        parts = [jnp.maximum(a, b) for a, b in zip(parts[::2], parts[1::2])]
    return parts[0]


def _flash_kernel(*refs, n_maps, has_prefix, lam_init, hb, dq):
    refs = list(refs)
    q_ref, k_ref, vt_ref = refs[:3]
    pos = 3
    if has_prefix:
        kc_ref, vct_ref = refs[pos:pos + 2]
        pos += 2
    if n_maps == 2:
        lam_ref, ng_ref = refs[pos:pos + 2]
        pos += 2
    o_ref = refs[pos]
    scratch = refs[pos + 1:]
    if n_maps == 2:
        qm_ref, m_ref, acc_ref = scratch
    else:
        m_ref, acc_ref = scratch
    kv = pl.program_id(3)

    def scores(ci, k_src):
        hd = ci // n_maps
        q = qm_ref[ci] if n_maps == 2 else q_ref[0, :, hd * dq:(hd + 1) * dq]
        return _dot_nt(k_src[0, :, hd * dq:(hd + 1) * dq], q)

    def update(ci, s, vt_src):
        hd = ci // n_maps
        vt = vt_src[0, hd * LANES:(hd + 1) * LANES, :]
        m_prev = m_ref[ci]
        m_new = jnp.maximum(m_prev, _col_max(s))
        alpha = jnp.exp2(m_prev - m_new)
        p = jnp.exp2(s - m_new).astype(BF16)
        v1 = jnp.concatenate([vt, jnp.ones((ONES_ROWS, vt.shape[1]), BF16)], axis=0)
        acc_ref[ci] = acc_ref[ci] * alpha + _dot(v1, p)
        m_ref[ci] = m_new

    def all_steps(k_src, vt_src):
        n_chains = hb * n_maps
        pending = [scores(ci, k_src) for ci in range(min(LOOKAHEAD, n_chains))]
        for ci in range(n_chains):
            if ci + LOOKAHEAD < n_chains:
                pending.append(scores(ci + LOOKAHEAD, k_src))
            update(ci, pending.pop(0), vt_src)

    @pl.when(kv == 0)
    def _init():
        m_ref[...] = jnp.full(m_ref.shape, NEG_BIG, F32)
        acc_ref[...] = jnp.zeros(acc_ref.shape, F32)
        if n_maps == 2:
            for hd in range(hb):
                q = q_ref[0, :, hd * dq:(hd + 1) * dq]
                lane = lax.broadcasted_iota(jnp.int32, q.shape, 1)
                qm_ref[2 * hd] = jnp.where(lane < DIFF_HEAD_DIM, q, jnp.zeros_like(q))
                qm_ref[2 * hd + 1] = jnp.where(lane >= DIFF_HEAD_DIM, q, jnp.zeros_like(q))
        if has_prefix:
            all_steps(kc_ref, vct_ref)

    all_steps(k_ref, vt_ref)

    @pl.when(kv == pl.num_programs(3) - 1)
    def _finish():
        if n_maps == 2:
            lf = lam_ref[...]
            l1 = jnp.sum(lf[0:1] * lf[1:2], axis=1, keepdims=True)
            l2 = jnp.sum(lf[2:3] * lf[3:4], axis=1, keepdims=True)
            lam = jnp.exp(l1) - jnp.exp(l2) + lam_init
        for hd in range(hb):
            a0 = acc_ref[hd * n_maps]
            o = a0[:LANES] / a0[LANES:LANES + 1]
            if n_maps == 2:
                a1 = acc_ref[hd * n_maps + 1]
                o = o - lam * (a1[:LANES] / a1[LANES:LANES + 1])
                ms = jnp.mean(o * o, axis=0, keepdims=True)
                o = o * lax.rsqrt(ms + NORM_EPS) * ng_ref[...] * (1.0 - lam_init)
            o_ref[0, :, hd * LANES:(hd + 1) * LANES] = o.T.astype(o_ref.dtype)


def _flash(q, k, vt, prefix, extras, *, heads, dq, tq, tk, lam_init=0.0, hb=1):
    b, s, _ = q.shape
    n_maps = 2 if extras is not None else 1
    in_specs = [
        pl.BlockSpec((1, tq, hb * dq), lambda bi, h, qi, kv: (bi, qi, h)),
        pl.BlockSpec((1, tk, hb * dq), lambda bi, h, qi, kv: (bi, kv, h)),
        pl.BlockSpec((1, hb * LANES, tk), lambda bi, h, qi, kv: (bi, h, kv)),
    ]
    args = [q, k, vt]
    if prefix is not None:
        cl = prefix[0].shape[1]
        in_specs += [
            pl.BlockSpec((1, cl, hb * dq), lambda bi, h, qi, kv: (bi, 0, h)),
            pl.BlockSpec((1, hb * LANES, cl), lambda bi, h, qi, kv: (bi, h, 0)),
        ]
        args += list(prefix)
    scratch = []
    if n_maps == 2:
        in_specs += [_const_spec(extras[0].shape), _const_spec(extras[1].shape)]
        args += list(extras)
        scratch.append(pltpu.VMEM((2 * hb, tq, dq), BF16))
    scratch += [pltpu.VMEM((hb * n_maps, 1, tq), F32), pltpu.VMEM((hb * n_maps, LANES + ONES_ROWS, tq), F32)]
    return pl.pallas_call(
        functools.partial(_flash_kernel, n_maps=n_maps, has_prefix=prefix is not None, lam_init=lam_init,
                          hb=hb, dq=dq),
        grid=(b, heads // hb, s // tq, k.shape[1] // tk),
        in_specs=in_specs,
        out_specs=pl.BlockSpec((1, tq, hb * LANES), lambda bi, h, qi, kv: (bi, qi, h)),
        out_shape=jax.ShapeDtypeStruct((b, s, heads * LANES), BF16),
        scratch_shapes=scratch,
        compiler_params=_params("parallel", "parallel", "parallel", "arbitrary"),
        name="flash_diff" if n_maps == 2 else "flash_mla",
    )(*args)


def _gla_proj_kernel(x_ref, mod_ref, g_ref, w_ref, wr_ref, wg_ref, bg_ref, q_ref, k_ref, v_ref, og_ref, gate_ref):
    h = _norm_mod(x_ref[0], g_ref[...], mod_ref[0, 0:1, :], mod_ref[0, 1:2, :]).astype(BF16)
    col = 0
    for o_ref, width, scale in ((q_ref, GLA_KEY, GLA_DK ** -0.5), (k_ref, GLA_KEY, None),
                                (v_ref, GLA_VAL, None), (og_ref, GLA_VAL, None)):
        for c in range(width // MXU_TILE):
            y = _dot(h, w_ref[:, col + c * MXU_TILE: col + (c + 1) * MXU_TILE])
            if scale is not None:
                y = y * scale
            o_ref[0, :, c * MXU_TILE:(c + 1) * MXU_TILE] = y.astype(BF16)
        col += width
    r_hi, r_lo = _split_bf16(_dot(h, wr_ref[...]))
    for c in range(2 * GLA_KEY // MXU_TILE):
        wg = wg_ref[:, c * MXU_TILE:(c + 1) * MXU_TILE]
        z = _dot(r_hi, wg) + _dot(r_lo, wg) + bg_ref[:, c * MXU_TILE:(c + 1) * MXU_TILE]
        log_sig = jnp.minimum(z, 0.0) - jnp.log(1.0 + jnp.exp(-jnp.abs(z)))
        gate_ref[0, :, c * MXU_TILE:(c + 1) * MXU_TILE] = log_sig / GLA_GATE_NORM


def _gla_proj(x, mod, mod_row, norm_g, w, wr, wg, bg, tm):
    bx, s, d = x.shape
    x_spec, mod_spec = _token_specs(tm, d, mod_row)
    return pl.pallas_call(
        _gla_proj_kernel,
        grid=(bx, s // tm),
        in_specs=[x_spec, mod_spec, _const_spec((1, d)), _const_spec(w.shape), _const_spec(wr.shape),
                  _const_spec(wg.shape), _const_spec(bg.shape)],
        out_specs=[_out_spec(tm, GLA_KEY), _out_spec(tm, GLA_KEY), _out_spec(tm, GLA_VAL),
                   _out_spec(tm, GLA_VAL), _out_spec(tm, 2 * GLA_KEY)],
        out_shape=[jax.ShapeDtypeStruct((bx, s, GLA_KEY), BF16), jax.ShapeDtypeStruct((bx, s, GLA_KEY), BF16),
                   jax.ShapeDtypeStruct((bx, s, GLA_VAL), BF16), jax.ShapeDtypeStruct((bx, s, GLA_VAL), BF16),
                   jax.ShapeDtypeStruct((bx, s, 2 * GLA_KEY), F32)],
        compiler_params=_params("parallel", "parallel"),
        name="gla_proj",
    )(x, mod, norm_g, w, wr, wg, bg)


def _gla_scan_kernel(q_ref, k_ref, v_ref, g_ref, s0_ref, o_ref, sfin_ref, st_ref):
    direction = pl.program_id(1)
    i = pl.program_id(2)
    n_chunks = q_ref.shape[1] // GLA_CHUNK

    @pl.when(i == 0)
    def _load_state():
        st_ref[...] = s0_ref[0, 0]

    def run(backward):
        tb = q_ref.shape[1]
        row = lax.broadcasted_iota(jnp.int32, (tb, tb), 0)
        colm = lax.broadcasted_iota(jnp.int32, (tb, tb), 1)
        same = (row // GLA_CHUNK) == (colm // GLA_CHUNK)
        seen = (colm >= row) if backward else (colm <= row)
        seen_m = jnp.where(same & seen, 1.0, 0.0).astype(BF16)
        rest_m = jnp.where(same & jnp.logical_not(seen), 1.0, 0.0).astype(BF16)
        g_hi, g_lo = _split_bf16(g_ref[0])
        cum = _dot(seen_m, g_hi) + _dot(seen_m, g_lo)
        rest = _dot(rest_m, g_hi) + _dot(rest_m, g_lo)
        q = q_ref[0].astype(F32)
        k = k_ref[0].astype(F32)
        q_dec = (q * jnp.exp(cum)).astype(BF16)
        k_inv = (k * jnp.exp(-cum)).astype(BF16)
        k_end = (k * jnp.exp(rest)).astype(BF16)
        decay = jnp.exp(cum + rest)
        within = same & seen
        chunk_of_row = lax.broadcasted_iota(jnp.int32, (tb, GLA_DK), 0) // GLA_CHUNK
        order = list(range(n_chunks - 1, -1, -1) if backward else range(n_chunks))
        def head_matmuls(h):
            ksl = slice(h * GLA_DK, (h + 1) * GLA_DK)
            v = v_ref[0, :, h * GLA_DV:(h + 1) * GLA_DV]
            qd = q_dec[:, ksl]
            a = jnp.where(within, _dot_nt(qd, k_inv[:, ksl]), 0.0)
            o_intra = _dot(a.astype(BF16), v)
            ke = k_end[:, ksl]
            ke_wide = jnp.concatenate([jnp.where(chunk_of_row == c, ke, jnp.zeros_like(ke)) for c in range(n_chunks)],
                                      axis=1)
            return qd, o_intra, _dot_tn(v, ke_wide)

        ready = head_matmuls(0)
        for h in range(GLA_HEADS):
            qd, o_intra, inc = ready
            if h + 1 < GLA_HEADS:
                ready = head_matmuls(h + 1)
            ksl = slice(h * GLA_DK, (h + 1) * GLA_DK)
            vsl = slice(h * GLA_DV, (h + 1) * GLA_DV)
            st = st_ref[h]
            for c in order:
                sl = slice(c * GLA_CHUNK, (c + 1) * GLA_CHUNK)
                o_ref[0, 0, sl, vsl] = (o_intra[sl] + _dot_nt(qd[sl], st.astype(BF16))).astype(o_ref.dtype)
                st = st * decay[c * GLA_CHUNK:c * GLA_CHUNK + 1, ksl] + inc[:, c * GLA_DK:(c + 1) * GLA_DK]
            st_ref[h] = st

    pl.when(direction == 0)(lambda: run(False))
    pl.when(direction == 1)(lambda: run(True))

    @pl.when(i == pl.num_programs(2) - 1)
    def _store_state():
        sfin_ref[0, 0] = st_ref[...]


def _gla_scan(q, k, v, g, s0, tb):
    b, s, _ = q.shape
    nb = s // tb

    def blk(d, i):
        return i + d * (nb - 1 - 2 * i)

    st_spec = pl.BlockSpec((1, 1, GLA_HEADS, GLA_DV, GLA_DK), lambda bi, d, i: (bi, d, 0, 0, 0))
    return pl.pallas_call(
        _gla_scan_kernel,
        grid=(b, 2, nb),
        in_specs=[
            pl.BlockSpec((1, tb, GLA_KEY), lambda bi, d, i: (bi, blk(d, i), 0)),
            pl.BlockSpec((1, tb, GLA_KEY), lambda bi, d, i: (bi, blk(d, i), 0)),
            pl.BlockSpec((1, tb, GLA_VAL), lambda bi, d, i: (bi, blk(d, i), 0)),
            pl.BlockSpec((1, tb, GLA_KEY), lambda bi, d, i: (bi, blk(d, i), d)),
            st_spec,
        ],
        out_specs=[pl.BlockSpec((1, 1, tb, GLA_VAL), lambda bi, d, i: (d, bi, blk(d, i), 0)), st_spec],
        out_shape=[jax.ShapeDtypeStruct((2, b, s, GLA_VAL), BF16),
                   jax.ShapeDtypeStruct((b, 2, GLA_HEADS, GLA_DV, GLA_DK), F32)],
        scratch_shapes=[pltpu.VMEM((GLA_HEADS, GLA_DV, GLA_DK), F32)],
        compiler_params=_params("parallel", "parallel", "arbitrary"),
        name="gla_scan",
    )(q, k, v, g, s0)


def _mla_proj_kernel(*refs, rope):
    if rope:
        (x_ref, mod_ref, g_ref, w_ref, qg_ref, wq_ref, kvg_ref, wkn_ref, wvt_ref,
         cos_ref, sa_ref, sb_ref, q_ref, k_ref, vt_ref) = refs
        cos, sa, sb = cos_ref[...], sa_ref[...], sb_ref[...]
    else:
        x_ref, mod_ref, g_ref, w_ref, qg_ref, wq_ref, kvg_ref, wkn_ref, wvt_ref, q_ref, k_ref, vt_ref = refs
    h = _norm_mod(x_ref[0], g_ref[...], mod_ref[0, 0:1, :], mod_ref[0, 1:2, :]).astype(BF16)
    y = _dot(h, w_ref[...])
    c_q = _rms(y[:, :MLA_Q_LORA], qg_ref[...]).astype(BF16)
    c_kv = _rms(y[:, MLA_Q_LORA:MLA_Q_LORA + MLA_KV_LORA], kvg_ref[...]).astype(BF16)
    k_rope = y[:, MLA_Q_LORA + MLA_KV_LORA:]
    if rope:
        k_rope = _rope_chunk(k_rope, cos, sa, sb)
    k_rope = k_rope.astype(BF16)
    for hd in range(MLA_HEADS):
        base = hd * MLA_QK_PAD
        qh = _dot(c_q, wq_ref[:, base: base + MLA_QK_PAD])
        q_rope = qh[:, LANES:]
        if rope:
            q_rope = _rope_chunk(q_rope, cos, sa, sb)
        q_ref[0, :, base: base + LANES] = (qh[:, :LANES] * (MLA_SCALE * LOG2E)).astype(BF16)
        q_ref[0, :, base + LANES: base + MLA_QK_PAD] = (q_rope * (MLA_SCALE * LOG2E)).astype(BF16)
        k_ref[0, :, base + LANES: base + MLA_QK_PAD] = k_rope
    vt_ref[0] = _dot_nt(wvt_ref[...], c_kv).astype(BF16)
    for c in range(MLA_HEADS * MLA_NOPE // MXU_TILE):
        kn = _dot(c_kv, wkn_ref[:, c * MXU_TILE:(c + 1) * MXU_TILE]).astype(BF16)
        for s in range(2):
            hd = 2 * c + s
            k_ref[0, :, hd * MLA_QK_PAD: hd * MLA_QK_PAD + LANES] = kn[:, s * LANES:(s + 1) * LANES]


def _mla_proj(x, mod, mod_row, norm_g, w, qg, wq, kvg, wkn, wvt, tables, tm):
    bx, s, d = x.shape
    rope = tables is not None
    x_spec, mod_spec = _token_specs(tm, d, mod_row)
    consts = [norm_g, w, qg, wq, kvg, wkn, wvt]
    in_specs = [x_spec, mod_spec] + [_const_spec(a.shape) for a in consts]
    args = [x, mod] + consts
    if rope:
        in_specs += _rope_specs(tm)
        args += list(tables)
    qk_w = MLA_HEADS * MLA_QK_PAD
    return pl.pallas_call(
        functools.partial(_mla_proj_kernel, rope=rope),
        grid=(bx, s // tm),
        in_specs=in_specs,
        out_specs=[_out_spec(tm, qk_w), _out_spec(tm, qk_w), _vt_spec(tm, MLA_HEADS * MLA_V)],
        out_shape=[jax.ShapeDtypeStruct((bx, s, qk_w), BF16), jax.ShapeDtypeStruct((bx, s, qk_w), BF16),
                   jax.ShapeDtypeStruct((bx, MLA_HEADS * MLA_V, s), BF16)],
        compiler_params=_params("parallel", "parallel"),
        name="mla_proj",
    )(*args)


def _top2_sum(a, b, c, d):
    hi1, lo1 = jnp.maximum(a, b), jnp.minimum(a, b)
    hi2, lo2 = jnp.maximum(c, d), jnp.minimum(c, d)
    return jnp.maximum(hi1, hi2) + jnp.maximum(jnp.minimum(hi1, hi2), jnp.maximum(lo1, lo2))


def _route_rows(scores, sel):
    eg = EXPERTS_PER_GROUP
    gs = [_top2_sum(*sel[g * eg:(g + 1) * eg]) for g in range(N_GROUPS)]
    best = jnp.maximum(jnp.maximum(gs[0], gs[1]), jnp.maximum(gs[2], gs[3]))
    gidx = jnp.where(gs[0] >= best, 0, jnp.where(gs[1] >= best, 1, jnp.where(gs[2] >= best, 2, 3)))

    def pick(rows, j):
        out = rows[(N_GROUPS - 1) * eg + j]
        for g in range(N_GROUPS - 2, -1, -1):
            out = jnp.where(gidx == g, rows[g * eg + j], out)
        return out

    loc_sel = [pick(sel, j) for j in range(eg)]
    loc_sc = [pick(scores, j) for j in range(eg)]
    weights = []
    for j in range(eg):
        rank = jnp.zeros_like(loc_sel[j])
        for i in range(eg):
            if i == j:
                continue
            beats = (loc_sel[i] >= loc_sel[j]) if i < j else (loc_sel[i] > loc_sel[j])
            rank = rank + jnp.where(beats, 1.0, 0.0)
        weights.append(jnp.where(rank < 1.5, loc_sc[j], 0.0))
    den = weights[0] + weights[1] + weights[2] + weights[3]
    weights = [w / den for w in weights]
    return [jnp.where(gidx == e // eg, weights[e % eg], 0.0) for e in range(N_EXPERTS)], gidx


def _mixer_output(pre, refs):
    if pre == "gla":
        of_ref, ob_ref, og_ref, ng_ref, w_ref = refs
        o = of_ref[0, 0].astype(F32) + ob_ref[0, 0].astype(F32)
        parts = []
        for h in range(GLA_HEADS):
            sl = slice(h * GLA_DV, (h + 1) * GLA_DV)
            parts.append((_rms(o[:, sl], ng_ref[...]) * _silu(og_ref[0, :, sl].astype(F32))).astype(BF16))
        return _dot(jnp.concatenate(parts, axis=1), w_ref[...])
    o_ref, w_ref = refs
    return _dot(o_ref[0], w_ref[...])


def _moe_kernel(*refs, pre, final_norm):
    n_pre = 5 if pre == "gla" else 2
    (x_ref, mod_ref, g_ref, rw_ref, rb_ref, shu_ref, shd_ref, tri_ref, wup_ref, wdn_ref, fg_ref,
     out_ref, perm_ref, hs_ref, cs_ref, ys_ref) = refs[n_pre:]
    x = x_ref[0] + mod_ref[0, 2:3, :] * _mixer_output(pre, refs[:n_pre])
    tm = x.shape[0]
    n_rows = hs_ref.shape[0]
    h, h_lo = _split_bf16(_norm_mod(x, g_ref[...], mod_ref[0, 3:4, :], mod_ref[0, 4:5, :]))
    both = _dot_nt(rw_ref[...], h)
    logits = both[:N_EXPERTS] + both[N_EXPERTS:] + _dot_nt(rw_ref[:N_EXPERTS, :], h_lo)
    ys_ref[...] = jnp.zeros(ys_ref.shape, F32)
    a = _dot(h, shu_ref[:, :SHARED_FF])
    u = _dot(h, shu_ref[:, SHARED_FF:])
    out_ref[0] = x + mod_ref[0, 5:6, :] * _dot((_silu(a) * u).astype(BF16), shd_ref[...])
    scores = jax.nn.sigmoid(logits)
    sel = scores + rb_ref[...]
    rows, gidx = _route_rows([scores[e:e + 1, :] for e in range(N_EXPERTS)],
                             [sel[e:e + 1, :] for e in range(N_EXPERTS)])
    comb = jnp.concatenate(rows + [jnp.zeros((LANES - N_EXPERTS, tm), F32)], axis=0).T

    member = [jnp.where(gidx == g, 1.0, 0.0) for g in range(N_GROUPS)]
    member_m = jnp.concatenate(member + [jnp.zeros((8 - N_GROUPS, tm), F32)], axis=0).astype(BF16)
    rank = _dot(member_m, tri_ref[...])
    dest = jnp.zeros((1, tm), F32)
    start = jnp.zeros((1, 1), F32)
    seg_start, seg_count = [], []
    for g in range(N_GROUPS):
        count = jnp.sum(member[g], axis=1, keepdims=True)
        dest = dest + member[g] * (start + rank[g:g + 1] - 1.0)
        seg_start.append(start[0, 0].astype(jnp.int32))
        seg_count.append(count[0, 0].astype(jnp.int32))
        start = jnp.floor((start + count + (MOE_ALIGN - 1)) * (1.0 / MOE_ALIGN)) * MOE_ALIGN
    row_id = lax.broadcasted_iota(jnp.int32, (n_rows, tm), 0)
    perm = jnp.where(row_id == dest.astype(jnp.int32), 1.0, 0.0).astype(BF16)
    perm_ref[...] = perm
    c_hi, c_lo = _split_bf16(comb)
    moved = _dot(perm, jnp.concatenate([h, c_hi, c_lo], axis=1))
    d = h.shape[1]
    hs_ref[...] = moved[:, :d].astype(BF16)
    cs_ref[...] = moved[:, d:d + LANES] + moved[:, d + LANES:]

    def run_group(grp, size, n_passes):
        def one_pass(i, carry):
            r0 = pl.multiple_of(seg_start[grp] + i * size, MOE_ALIGN)
            hc = hs_ref[pl.ds(r0, size), :]
            cc = cs_ref[pl.ds(r0, size), :]
            acts = []
            for j in range(EXPERTS_PER_GROUP):
                e = grp * EXPERTS_PER_GROUP + j
                a = _dot(hc, wup_ref[0, e, :, :EXPERT_FF])
                u = _dot(hc, wup_ref[0, e, :, EXPERT_FF:])
                scale = jnp.broadcast_to(cc[:, e:e + 1], a.shape)
                acts.append((_silu(a) * u * scale).astype(BF16))
            w_dn = wdn_ref[0, grp * EXPERTS_PER_GROUP:(grp + 1) * EXPERTS_PER_GROUP]
            y = _dot(jnp.concatenate(acts, axis=1), w_dn.reshape(EXPERTS_PER_GROUP * EXPERT_FF, -1))
            ys_ref[pl.ds(r0, size), :] = ys_ref[pl.ds(r0, size), :] + y
            return carry

        lax.fori_loop(0, n_passes, one_pass, 0)

    for grp in range(N_GROUPS):
        n = seg_count[grp]
        fits = n <= MOE_CHUNK_SMALL
        run_group(grp, MOE_CHUNK_SMALL, jnp.where(fits, jnp.minimum(n, 1), 0))
        run_group(grp, MOE_CHUNK_BIG, jnp.where(fits, 0, (n + (MOE_CHUNK_BIG - 1)) // MOE_CHUNK_BIG))

    out = out_ref[0] + mod_ref[0, 5:6, :] * _dot_tn(perm_ref[...], ys_ref[...].astype(BF16))
    if final_norm:
        out = _rms(out, fg_ref[...])
    out_ref[0] = out


def _moe(mixer, x, mod, mod_row, norm_g, rw, rb, sh_up, sh_dn, w_up, w_dn, layer, final_g, final_norm, tm):
    bx, s, d = x.shape
    x_spec, mod_spec = _token_specs(tm, d, mod_row)
    pre = mixer[0]
    if pre == "gla":
        o2, og, ng, w_out = mixer[1:]
        o2 = o2.reshape(2, bx, s, GLA_VAL)
        pre_specs = [pl.BlockSpec((1, 1, tm, GLA_VAL), lambda b, t: (0, b, t, 0)),
                     pl.BlockSpec((1, 1, tm, GLA_VAL), lambda b, t: (1, b, t, 0)),
                     _out_spec(tm, GLA_VAL), _const_spec(ng.shape, True), _const_spec(w_out.shape, True)]
        pre_args = [o2, o2, og.reshape(bx, s, GLA_VAL), ng, w_out]
    else:
        o, w_out = mixer[1:]
        o = o.reshape(bx, s, -1)
        pre_specs = [_out_spec(tm, o.shape[2]), _const_spec(w_out.shape, True)]
        pre_args = [o, w_out]
    tri = (jnp.arange(tm)[:, None] <= jnp.arange(tm)[None, :]).astype(BF16)
    consts = [norm_g, rw, rb, sh_up, sh_dn, tri]
    n_rows = -(-(tm + (N_GROUPS - 1) * (MOE_ALIGN - 1) + MOE_CHUNK_BIG) // MXU_TILE) * MXU_TILE
    resident = pl.Buffered(1)
    in_specs = (pre_specs + [x_spec, mod_spec] + [_const_spec(a.shape, True) for a in consts] + [
        pl.BlockSpec((1,) + w_up.shape[1:], lambda b, t: (layer, 0, 0, 0), pipeline_mode=resident),
        pl.BlockSpec((1,) + w_dn.shape[1:], lambda b, t: (layer, 0, 0, 0), pipeline_mode=resident),
        _const_spec(final_g.shape, True)])
    return pl.pallas_call(
        functools.partial(_moe_kernel, pre=pre, final_norm=final_norm),
        grid=(bx, s // tm),
        in_specs=in_specs,
        out_specs=x_spec,
        out_shape=jax.ShapeDtypeStruct(x.shape, F32),
        scratch_shapes=[pltpu.VMEM((n_rows, tm), BF16), pltpu.VMEM((n_rows, d), BF16),
                        pltpu.VMEM((n_rows, LANES), F32), pltpu.VMEM((n_rows, d), F32)],
        compiler_params=_params("parallel", "parallel"),
        name="moe",
    )(*pre_args, x, mod, *consts, w_up, w_dn, final_g)


def _diff_weights(w_in):
    d = w_in.shape[0]

    def regroup(w):
        return w.reshape(d, 2, DIFF_HEADS, DIFF_HEAD_DIM).transpose(0, 2, 1, 3).reshape(d, DIFF_QK)

    w_qk = jnp.concatenate([regroup(w_in[:, :DIFF_QK]), regroup(w_in[:, DIFF_QK:2 * DIFF_QK])], axis=1)
    return w_qk.astype(BF16), w_in[:, 2 * DIFF_QK:].T.astype(BF16)


def _gla_weights(w_in, gate_w, gate_b):
    d = w_in.shape[0]
    main = 2 * GLA_KEY + 2 * GLA_VAL
    w_r = jnp.zeros((d, LANES), F32).at[:, :2 * GLA_GATE_RANK].set(w_in[:, main:]).astype(BF16)
    w_g = jnp.zeros((LANES, 2 * GLA_KEY), F32)
    w_g = w_g.at[:GLA_GATE_RANK, :GLA_KEY].set(gate_w[0])
    w_g = w_g.at[GLA_GATE_RANK:2 * GLA_GATE_RANK, GLA_KEY:].set(gate_w[1]).astype(BF16)
    return w_in[:, :main].astype(BF16), w_r, w_g, gate_b.reshape(1, 2 * GLA_KEY)


def _mla_weights(w_in, w_uq, w_ukv):
    d = w_in.shape[0]
    w = jnp.zeros((d, MLA_Q_LORA + MLA_KV_LORA + LANES), F32).at[:, :w_in.shape[1]].set(w_in).astype(BF16)
    wq = w_uq.reshape(MLA_Q_LORA, MLA_HEADS, MLA_NOPE + MLA_ROPE)
    wq = jnp.pad(wq, ((0, 0), (0, 0), (0, MLA_QK_PAD - MLA_NOPE - MLA_ROPE)))
    wq = wq.reshape(MLA_Q_LORA, MLA_HEADS * MLA_QK_PAD).astype(BF16)
    wkv = w_ukv.reshape(MLA_KV_LORA, MLA_HEADS, MLA_NOPE + MLA_V)
    wkn = wkv[:, :, :MLA_NOPE].reshape(MLA_KV_LORA, MLA_HEADS * MLA_NOPE).astype(BF16)
    wvt = wkv[:, :, MLA_NOPE:].reshape(MLA_KV_LORA, MLA_HEADS * MLA_V).T.astype(BF16)
    return w, wq, wkn, wvt


def _ctx_views(qkv, b, cl):
    q, k, vt = qkv
    vt = vt.reshape(vt.shape[1], b, cl).transpose(1, 0, 2)
    return q.reshape(b, cl, -1), k.reshape(b, cl, -1), vt


def kernel(x, c, ctx, c_ctx, ada_w, ada_b, norm_g, router_w, router_b, moe_w_up, moe_w_down, shared_w_up,
           shared_w_down, diff_w_in, diff_lam, diff_norm_g, diff_w_out, gla_w_in, gla_gate_w, gla_gate_b,
           gla_norm_g, gla_w_out, mla_w_in, mla_q_norm_g, mla_w_uq, mla_kv_norm_g, mla_w_ukv, mla_w_out,
           final_norm_g):
    b, s, d = x.shape
    cl = ctx.shape[1]
    assert b + 1 <= COND_ROWS and d == D_MODEL
    depth = ada_w.shape[0]
    ctx_row = b

    cond = jnp.zeros((COND_ROWS, d), F32).at[:b].set(c).at[ctx_row].set(c_ctx)
    mods = _ada_params(cond, ada_w, ada_b).reshape(depth, COND_ROWS, N_MOD, d)

    tables = _rope_tables(s)
    rw = jnp.concatenate(_split_bf16(router_w.T), axis=0)
    rb = router_b.reshape(N_EXPERTS, 1)
    w_up_all = moe_w_up.astype(BF16)
    w_dn_all = moe_w_down.astype(BF16)
    final_g = final_norm_g.reshape(1, d)

    tm_lat = TOKEN_TILE
    n_ctx = b * cl
    tm_ctx = TOKEN_TILE if n_ctx % TOKEN_TILE == 0 else cl
    assert s % FLASH_TK == 0 and s % TOKEN_TILE == 0 and s % GLA_BLOCK == 0 and cl % GLA_CHUNK == 0
    x_lat = x
    x_ctx = ctx.reshape(1, n_ctx, d)

    for i in range(depth):
        need_ctx = i < depth - 1
        mod = mods[i]
        g1 = norm_g[i, 0].reshape(1, d)
        g2 = norm_g[i, 1].reshape(1, d)
        kind, j = i % N_MIXERS, i // N_MIXERS
        if kind == 0:
            lam_init = 0.8 - 0.6 * math.exp(-0.3 * i)
            w, wvt = _diff_weights(diff_w_in[j])
            w_out = diff_w_out[j].astype(BF16)
            extras = (diff_lam[j], diff_norm_g[j].reshape(2 * DIFF_HEAD_DIM, 1))
            q_l, k_l, v_l = _diff_proj(x_lat, mod, None, g1, w, wvt, tables, tm_lat)
            q_c, k_c, v_c = _ctx_views(_diff_proj(x_ctx, mod, ctx_row, g1, w, wvt, None, tm_ctx), b, cl)
            o_lat = _flash(q_l, k_l, v_l, (k_c, v_c), extras, heads=DIFF_HEADS, dq=2 * DIFF_HEAD_DIM,
                           tq=FLASH_TQ, tk=FLASH_TK, lam_init=lam_init, hb=DIFF_HEADS)
            mix_lat = ("plain", o_lat, w_out)
            if need_ctx:
                o_ctx = _flash(q_c, k_c, v_c, None, extras, heads=DIFF_HEADS, dq=2 * DIFF_HEAD_DIM,
                               tq=cl, tk=cl, lam_init=lam_init, hb=DIFF_HEADS)
                mix_ctx = ("plain", o_ctx, w_out)
        elif kind == 1:
            w, w_r, w_g, b_g = _gla_weights(gla_w_in[j], gla_gate_w[j], gla_gate_b[j])
            w_out = gla_w_out[j].astype(BF16)
            ng = gla_norm_g[j].reshape(1, GLA_DV)
            q_l, k_l, v_l, og_l, gt_l = _gla_proj(x_lat, mod, None, g1, w, w_r, w_g, b_g, tm_lat)
            q_c, k_c, v_c, og_c, gt_c = (t.reshape(b, cl, -1)
                                         for t in _gla_proj(x_ctx, mod, ctx_row, g1, w, w_r, w_g, b_g, tm_ctx))
            s0 = jnp.zeros((b, 2, GLA_HEADS, GLA_DV, GLA_DK), F32)
            o_c, s_c = _gla_scan(q_c, k_c, v_c, gt_c, s0, cl)
            o_l, _ = _gla_scan(q_l, k_l, v_l, gt_l, s_c, GLA_BLOCK)
            mix_lat = ("gla", o_l, og_l, ng, w_out)
            if need_ctx:
                mix_ctx = ("gla", o_c, og_c, ng, w_out)
        else:
            w, wq, wkn, wvt = _mla_weights(mla_w_in[j], mla_w_uq[j], mla_w_ukv[j])
            w_out = mla_w_out[j].astype(BF16)
            qg = mla_q_norm_g[j].reshape(1, MLA_Q_LORA)
            kvg = mla_kv_norm_g[j].reshape(1, MLA_KV_LORA)
            q_l, k_l, v_l = _mla_proj(x_lat, mod, None, g1, w, qg, wq, kvg, wkn, wvt, tables, tm_lat)
            q_c, k_c, v_c = _ctx_views(_mla_proj(x_ctx, mod, ctx_row, g1, w, qg, wq, kvg, wkn, wvt, None, tm_ctx),
                                       b, cl)
            o_lat = _flash(q_l, k_l, v_l, (k_c, v_c), None, heads=MLA_HEADS, dq=MLA_QK_PAD, tq=FLASH_TQ, tk=FLASH_TK,
                           hb=MLA_HEADS)
            mix_lat = ("plain", o_lat, w_out)
            if need_ctx:
                o_ctx = _flash(q_c, k_c, v_c, None, None, heads=MLA_HEADS, dq=MLA_QK_PAD, tq=cl, tk=cl,
                               hb=MLA_HEADS)
                mix_ctx = ("plain", o_ctx, w_out)

        sh_up = shared_w_up[i].astype(BF16)
        sh_dn = shared_w_down[i].astype(BF16)
        last = i == depth - 1
        x_lat = _moe(mix_lat, x_lat, mod, None, g2, rw, rb, sh_up, sh_dn, w_up_all, w_dn_all, i, final_g, last,
                     tm_lat)
        if need_ctx:
            x_ctx = _moe(mix_ctx, x_ctx, mod, ctx_row, g2, rw, rb, sh_up, sh_dn, w_up_all, w_dn_all, i, final_g,
                         False, tm_ctx)
    return x_lat
```

```python
import functools
import math

import jax
import jax.numpy as jnp
from jax import lax
from jax.experimental import pallas as pl
from jax.experimental.pallas import tpu as pltpu

F32 = jnp.float32
BF16 = jnp.bfloat16

D_MODEL = 1024
DEPTH = 4
GRID_W = 64
N_MIXERS = 3
NORM_EPS = 1e-6
ROPE_BASE = 10000.0
N_MOD = 6

DIFF_HEADS = 8
DIFF_HEAD_DIM = 64
DIFF_QK = 2 * DIFF_HEADS * DIFF_HEAD_DIM
DIFF_V = 2 * DIFF_HEADS * DIFF_HEAD_DIM

GLA_HEADS = 4
GLA_KEY = D_MODEL // 2
GLA_VAL = D_MODEL
GLA_DK = GLA_KEY // GLA_HEADS
GLA_DV = GLA_VAL // GLA_HEADS
GLA_GATE_RANK = 16
GLA_GATE_NORM = 16.0
GLA_CHUNK = 64

MLA_HEADS = 8
MLA_Q_LORA = 384
MLA_KV_LORA = 256
MLA_NOPE = 128
MLA_ROPE = 64
MLA_V = 128
MLA_SCALE = (MLA_NOPE + MLA_ROPE) ** -0.5
MLA_QK_PAD = 256

N_EXPERTS = 16
N_GROUPS = 4
EXPERTS_PER_GROUP = N_EXPERTS // N_GROUPS
EXPERT_FF = 256
SHARED_FF = 256

LANES = 128
BF16_ROWS = 16
MXU_TILE = 256
TOKEN_TILE = 512
ADA_COL_TILE = 1024
FLASH_TQ = 512
FLASH_TK = 1024
LOOKAHEAD = 2
ONES_ROWS = BF16_ROWS
GLA_BLOCK = 256
MOE_CHUNK_SMALL = 128
MOE_CHUNK_BIG = 208
MOE_ALIGN = BF16_ROWS
PAD_ROWS = 32
COND_ROWS = 16
NEG_BIG = -1e30
LOG2E = math.log2(math.e)
VMEM_LIMIT = 56 * 1024 * 1024

NT_DIMS = (((1,), (1,)), ((), ()))
TN_DIMS = (((0,), (0,)), ((), ()))


def _dot(a, b):
    return jnp.dot(a, b, preferred_element_type=F32)


def _dot_nt(a, b):
    return lax.dot_general(a, b, NT_DIMS, preferred_element_type=F32)


def _dot_tn(a, b):
    return lax.dot_general(a, b, TN_DIMS, preferred_element_type=F32)


def _rows(v):
    return jnp.broadcast_to(v.reshape(1, -1), (PAD_ROWS, v.size))


def _split_bf16(a):
    hi = a.astype(BF16)
    lo = (a - hi.astype(F32)).astype(BF16)
    return hi, lo


def _silu(a):
    return a * jax.nn.sigmoid(a)


def _params(*sem):
    return pltpu.CompilerParams(dimension_semantics=sem, vmem_limit_bytes=VMEM_LIMIT)


def _const_spec(shape, single=False):
    nd = len(shape)
    if single:
        return pl.BlockSpec(shape, lambda *_: (0,) * nd, pipeline_mode=pl.Buffered(1))
    return pl.BlockSpec(shape, lambda *_: (0,) * nd)


def _ada_kernel(cond_ref, w_ref, b_ref, o_ref):
    a_hi, a_lo = _split_bf16(_silu(cond_ref[...]))
    w_hi, w_lo = _split_bf16(w_ref[0])
    o_ref[0] = _dot(a_hi, w_hi) + _dot(a_lo, w_hi) + _dot(a_hi, w_lo) + b_ref[0]


def _ada_params(cond, ada_w, ada_b):
    depth, d, n = ada_w.shape
    tn = ADA_COL_TILE
    return pl.pallas_call(
        _ada_kernel,
        grid=(depth, n // tn),
        in_specs=[
            pl.BlockSpec((COND_ROWS, d), lambda i, j: (0, 0)),
            pl.BlockSpec((1, d, tn), lambda i, j: (i, 0, j)),
            pl.BlockSpec((1, 1, tn), lambda i, j: (i, 0, j)),
        ],
        out_specs=pl.BlockSpec((1, COND_ROWS, tn), lambda i, j: (i, 0, j)),
        out_shape=jax.ShapeDtypeStruct((depth, COND_ROWS, n), F32),
        compiler_params=_params("parallel", "parallel"),
        name="ada_params",
    )(cond, ada_w, ada_b.reshape(depth, 1, n))


def _norm_mod(x, g, shift, scale):
    ms = jnp.mean(x * x, axis=-1, keepdims=True)
    return (x * lax.rsqrt(ms + NORM_EPS) * g) * (1.0 + scale) + shift


def _rms(x, g):
    ms = jnp.mean(x * x, axis=-1, keepdims=True)
    return x * lax.rsqrt(ms + NORM_EPS) * g


def _rope_chunk(c, cos, sa, sb):
    return c * cos + pltpu.roll(c, LANES - 16, 1) * sa + pltpu.roll(c, 16, 1) * sb


def _rope_tables(seq):
    t = jnp.arange(seq)
    pos_row = (t // GRID_W).astype(F32)
    pos_col = (t % GRID_W).astype(F32)
    inv = ROPE_BASE ** (-jnp.arange(0, 32, 2, dtype=F32) / 32)
    lane = jnp.arange(LANES)
    d = lane % 64
    r = d % 32
    first = (r < 16)[None, :]
    pos = jnp.where((d // 32)[None, :] == 0, pos_row[:, None], pos_col[:, None])
    ang = pos * inv[r % 16][None, :]
    cos, sin = jnp.cos(ang), jnp.sin(ang)
    return cos, jnp.where(first, -sin, 0.0), jnp.where(first, 0.0, sin)


def _token_specs(tm, d, mod_row):
    x_spec = pl.BlockSpec((1, tm, d), lambda b, t: (b, t, 0))
    if mod_row is None:
        mod_spec = pl.BlockSpec((1, N_MOD, d), lambda b, t: (b, 0, 0))
    else:
        mod_spec = pl.BlockSpec((1, N_MOD, d), lambda b, t: (mod_row, 0, 0))
    return x_spec, mod_spec


def _rope_specs(tm):
    return [pl.BlockSpec((tm, LANES), lambda b, t: (t, 0))] * 3


def _out_spec(tm, n):
    return pl.BlockSpec((1, tm, n), lambda b, t: (b, t, 0))


def _diff_proj_kernel(*refs, rope):
    if rope:
        x_ref, mod_ref, g_ref, w_ref, wvt_ref, cos_ref, sa_ref, sb_ref, q_ref, k_ref, vt_ref = refs
        cos, sa, sb = cos_ref[...], sa_ref[...], sb_ref[...]
    else:
        x_ref, mod_ref, g_ref, w_ref, wvt_ref, q_ref, k_ref, vt_ref = refs
    h = _norm_mod(x_ref[0], g_ref[0:1, :], mod_ref[0, 0:1, :], mod_ref[0, 1:2, :]).astype(BF16)
    for idx, o_ref in enumerate((q_ref, k_ref)):
        for c in range(DIFF_QK // MXU_TILE):
            y = _dot(h, w_ref[:, idx * DIFF_QK + c * MXU_TILE: idx * DIFF_QK + (c + 1) * MXU_TILE])
            for s in range(2):
                ys = y[:, s * LANES:(s + 1) * LANES]
                if rope:
                    ys = _rope_chunk(ys, cos, sa, sb)
                if idx == 0:
                    ys = ys * (DIFF_HEAD_DIM ** -0.5 * LOG2E)
                o_ref[0, :, c * MXU_TILE + s * LANES: c * MXU_TILE + (s + 1) * LANES] = ys.astype(BF16)
    vt_ref[0] = _dot_nt(wvt_ref[...], h).astype(BF16)


def _vt_spec(tm, n):
    return pl.BlockSpec((1, n, tm), lambda b, t: (b, 0, t))


def _diff_proj(x, mod, mod_row, norm_g, w, wvt, tables, tm):
    bx, s, d = x.shape
    rope = tables is not None
    x_spec, mod_spec = _token_specs(tm, d, mod_row)
    in_specs = [x_spec, mod_spec, _const_spec((PAD_ROWS, d)), _const_spec(w.shape), _const_spec(wvt.shape)]
    args = [x, mod, norm_g, w, wvt]
    if rope:
        in_specs += _rope_specs(tm)
        args += list(tables)
    out = jax.ShapeDtypeStruct((bx, s, DIFF_QK), BF16)
    return pl.pallas_call(
        functools.partial(_diff_proj_kernel, rope=rope),
        grid=(bx, s // tm),
        in_specs=in_specs,
        out_specs=[_out_spec(tm, DIFF_QK), _out_spec(tm, DIFF_QK), _vt_spec(tm, DIFF_V)],
        out_shape=[out, out, jax.ShapeDtypeStruct((bx, DIFF_V, s), BF16)],
        compiler_params=_params("parallel", "parallel"),
        name="diff_proj",
    )(*args)


def _col_max(s, rows=64):
    parts = [jnp.max(s[i:i + rows], axis=0, keepdims=True) for i in range(0, s.shape[0], rows)]
    while len(parts) > 1:
        parts = [jnp.maximum(a, b) for a, b in zip(parts[::2], parts[1::2])]
    return parts[0]


def _flash_kernel(*refs, n_maps, has_prefix, lam_init, hb, dq):
    refs = list(refs)
    q_ref, k_ref, vt_ref = refs[:3]
    pos = 3
    if has_prefix:
        kc_ref, vct_ref = refs[pos:pos + 2]
        pos += 2
    if n_maps == 2:
        lam_ref, ng_ref = refs[pos:pos + 2]
        pos += 2
    o_ref = refs[pos]
    scratch = refs[pos + 1:]
    if n_maps == 2:
        qm_ref, m_ref, acc_ref = scratch
    else:
        m_ref, acc_ref = scratch
    kv = pl.program_id(3)

    def scores(ci, k_src):
        hd = ci // n_maps
        q = qm_ref[ci] if n_maps == 2 else q_ref[0, :, hd * dq:(hd + 1) * dq]
        return _dot_nt(k_src[0, :, hd * dq:(hd + 1) * dq], q)

    def update(ci, s, vt_src):
        hd = ci // n_maps
        vt = vt_src[0, hd * LANES:(hd + 1) * LANES, :]
        m_prev = m_ref[ci]
        m_new = jnp.maximum(m_prev, _col_max(s))
        alpha = jnp.exp2(m_prev - m_new)
        p = jnp.exp2(s - m_new).astype(BF16)
        v1 = jnp.concatenate([vt, jnp.ones((ONES_ROWS, vt.shape[1]), BF16)], axis=0)
        acc_ref[ci] = acc_ref[ci] * alpha + _dot(v1, p)
        m_ref[ci] = m_new

    def all_steps(k_src, vt_src):
        n_chains = hb * n_maps
        pending = [scores(ci, k_src) for ci in range(min(LOOKAHEAD, n_chains))]
        for ci in range(n_chains):
            if ci + LOOKAHEAD < n_chains:
                pending.append(scores(ci + LOOKAHEAD, k_src))
            update(ci, pending.pop(0), vt_src)

    @pl.when(kv == 0)
    def _init():
        m_ref[...] = jnp.full(m_ref.shape, NEG_BIG, F32)
        acc_ref[...] = jnp.zeros(acc_ref.shape, F32)
        if n_maps == 2:
            for hd in range(hb):
                q = q_ref[0, :, hd * dq:(hd + 1) * dq]
                lane = lax.broadcasted_iota(jnp.int32, q.shape, 1)
                qm_ref[2 * hd] = jnp.where(lane < DIFF_HEAD_DIM, q, jnp.zeros_like(q))
                qm_ref[2 * hd + 1] = jnp.where(lane >= DIFF_HEAD_DIM, q, jnp.zeros_like(q))
        if has_prefix:
            all_steps(kc_ref, vct_ref)

    all_steps(k_ref, vt_ref)

    @pl.when(kv == pl.num_programs(3) - 1)
    def _finish():
        if n_maps == 2:
            lf = lam_ref[0:4, :]
            l1 = jnp.sum(lf[0:1] * lf[1:2], axis=1, keepdims=True)
            l2 = jnp.sum(lf[2:3] * lf[3:4], axis=1, keepdims=True)
            lam = jnp.exp(l1) - jnp.exp(l2) + lam_init
        for hd in range(hb):
            a0 = acc_ref[hd * n_maps]
            o = a0[:LANES] * (1.0 / a0[LANES:LANES + 1])
            if n_maps == 2:
                a1 = acc_ref[hd * n_maps + 1]
                o = o - a1[:LANES] * (lam / a1[LANES:LANES + 1])
                ms = jnp.mean(o * o, axis=0, keepdims=True)
                o = o * lax.rsqrt(ms + NORM_EPS) * ng_ref[...] * (1.0 - lam_init)
            o_ref[0, :, hd * LANES:(hd + 1) * LANES] = o.T.astype(o_ref.dtype)


def _flash(q, k, vt, prefix, extras, *, heads, dq, tq, tk, lam_init=0.0, hb=1):
    b, s, _ = q.shape
    n_maps = 2 if extras is not None else 1
    in_specs = [
        pl.BlockSpec((1, tq, hb * dq), lambda bi, h, qi, kv: (bi, qi, h)),
        pl.BlockSpec((1, tk, hb * dq), lambda bi, h, qi, kv: (bi, kv, h)),
        pl.BlockSpec((1, hb * LANES, tk), lambda bi, h, qi, kv: (bi, h, kv)),
    ]
    args = [q, k, vt]
    if prefix is not None:
        cl = prefix[0].shape[1]
        in_specs += [
            pl.BlockSpec((1, cl, hb * dq), lambda bi, h, qi, kv: (bi, 0, h)),
            pl.BlockSpec((1, hb * LANES, cl), lambda bi, h, qi, kv: (bi, h, 0)),
        ]
        args += list(prefix)
    scratch = []
    if n_maps == 2:
        in_specs += [_const_spec(extras[0].shape), _const_spec(extras[1].shape)]
        args += list(extras)
        scratch.append(pltpu.VMEM((2 * hb, tq, dq), BF16))
    scratch += [pltpu.VMEM((hb * n_maps, 1, tq), F32), pltpu.VMEM((hb * n_maps, LANES + ONES_ROWS, tq), F32)]
    return pl.pallas_call(
        functools.partial(_flash_kernel, n_maps=n_maps, has_prefix=prefix is not None, lam_init=lam_init,
                          hb=hb, dq=dq),
        grid=(b, heads // hb, s // tq, k.shape[1] // tk),
        in_specs=in_specs,
        out_specs=pl.BlockSpec((1, tq, hb * LANES), lambda bi, h, qi, kv: (bi, qi, h)),
        out_shape=jax.ShapeDtypeStruct((b, s, heads * LANES), BF16),
        scratch_shapes=scratch,
        compiler_params=_params("parallel", "parallel", "parallel", "arbitrary"),
        name="flash_diff" if n_maps == 2 else "flash_mla",
    )(*args)


def _gla_proj_kernel(x_ref, mod_ref, g_ref, w_ref, wr_ref, wg_ref, bg_ref, q_ref, k_ref, v_ref, og_ref, gate_ref):
    h = _norm_mod(x_ref[0], g_ref[0:1, :], mod_ref[0, 0:1, :], mod_ref[0, 1:2, :]).astype(BF16)
    col = 0
    for o_ref, width, scale in ((q_ref, GLA_KEY, GLA_DK ** -0.5), (k_ref, GLA_KEY, None),
                                (v_ref, GLA_VAL, None), (og_ref, GLA_VAL, None)):
        for c in range(width // MXU_TILE):
            y = _dot(h, w_ref[:, col + c * MXU_TILE: col + (c + 1) * MXU_TILE])
            if scale is not None:
                y = y * scale
            o_ref[0, :, c * MXU_TILE:(c + 1) * MXU_TILE] = y.astype(BF16)
        col += width
    r_hi, r_lo = _split_bf16(_dot(h, wr_ref[...]))
    for c in range(2 * GLA_KEY // MXU_TILE):
        wg = wg_ref[:, c * MXU_TILE:(c + 1) * MXU_TILE]
        z = _dot(r_hi, wg) + _dot(r_lo, wg) + bg_ref[0:1, c * MXU_TILE:(c + 1) * MXU_TILE]
        log_sig = jnp.minimum(z, 0.0) - jnp.log(1.0 + jnp.exp(-jnp.abs(z)))
        gate_ref[0, :, c * MXU_TILE:(c + 1) * MXU_TILE] = log_sig / GLA_GATE_NORM


def _gla_proj(x, mod, mod_row, norm_g, w, wr, wg, bg, tm):
    bx, s, d = x.shape
    x_spec, mod_spec = _token_specs(tm, d, mod_row)
    return pl.pallas_call(
        _gla_proj_kernel,
        grid=(bx, s // tm),
        in_specs=[x_spec, mod_spec, _const_spec((PAD_ROWS, d)), _const_spec(w.shape), _const_spec(wr.shape),
                  _const_spec(wg.shape), _const_spec(bg.shape)],
        out_specs=[_out_spec(tm, GLA_KEY), _out_spec(tm, GLA_KEY), _out_spec(tm, GLA_VAL),
                   _out_spec(tm, GLA_VAL), _out_spec(tm, 2 * GLA_KEY)],
        out_shape=[jax.ShapeDtypeStruct((bx, s, GLA_KEY), BF16), jax.ShapeDtypeStruct((bx, s, GLA_KEY), BF16),
                   jax.ShapeDtypeStruct((bx, s, GLA_VAL), BF16), jax.ShapeDtypeStruct((bx, s, GLA_VAL), BF16),
                   jax.ShapeDtypeStruct((bx, s, 2 * GLA_KEY), F32)],
        compiler_params=_params("parallel", "parallel"),
        name="gla_proj",
    )(x, mod, norm_g, w, wr, wg, bg)


def _gla_scan_kernel(q_ref, k_ref, v_ref, g_ref, s0_ref, o_ref, sfin_ref, st_ref):
    direction = pl.program_id(1)
    i = pl.program_id(2)
    n_chunks = q_ref.shape[1] // GLA_CHUNK

    @pl.when(i == 0)
    def _load_state():
        st_ref[...] = s0_ref[0, 0]

    def run(backward):
        tb = q_ref.shape[1]
        row = lax.broadcasted_iota(jnp.int32, (tb, tb), 0)
        colm = lax.broadcasted_iota(jnp.int32, (tb, tb), 1)
        same = (row // GLA_CHUNK) == (colm // GLA_CHUNK)
        seen = (colm >= row) if backward else (colm <= row)
        seen_m = jnp.where(same & seen, 1.0, 0.0).astype(BF16)
        rest_m = jnp.where(same & jnp.logical_not(seen), 1.0, 0.0).astype(BF16)
        g_hi, g_lo = _split_bf16(g_ref[0])
        cum = _dot(seen_m, g_hi) + _dot(seen_m, g_lo)
        rest = _dot(rest_m, g_hi) + _dot(rest_m, g_lo)
        q = q_ref[0].astype(F32)
        k = k_ref[0].astype(F32)
        q_dec = (q * jnp.exp(cum)).astype(BF16)
        k_inv = (k * jnp.exp(-cum)).astype(BF16)
        k_end = (k * jnp.exp(rest)).astype(BF16)
        decay = jnp.exp(cum + rest)
        within = same & seen
        chunk_of_row = lax.broadcasted_iota(jnp.int32, (tb, GLA_DK), 0) // GLA_CHUNK
        order = list(range(n_chunks - 1, -1, -1) if backward else range(n_chunks))
        def head_matmuls(h):
            ksl = slice(h * GLA_DK, (h + 1) * GLA_DK)
            v = v_ref[0, :, h * GLA_DV:(h + 1) * GLA_DV]
            qd = q_dec[:, ksl]
            a = jnp.where(within, _dot_nt(qd, k_inv[:, ksl]), 0.0)
            o_intra = _dot(a.astype(BF16), v)
            ke = k_end[:, ksl]
            ke_wide = jnp.concatenate([jnp.where(chunk_of_row == c, ke, jnp.zeros_like(ke)) for c in range(n_chunks)],
                                      axis=1)
            return qd, o_intra, _dot_tn(v, ke_wide)

        ready = head_matmuls(0)
        for h in range(GLA_HEADS):
            qd, o_intra, inc = ready
            if h + 1 < GLA_HEADS:
                ready = head_matmuls(h + 1)
            ksl = slice(h * GLA_DK, (h + 1) * GLA_DK)
            vsl = slice(h * GLA_DV, (h + 1) * GLA_DV)
            st = st_ref[h]
            for c in order:
                sl = slice(c * GLA_CHUNK, (c + 1) * GLA_CHUNK)
                o_ref[0, 0, sl, vsl] = (o_intra[sl] + _dot_nt(qd[sl], st.astype(BF16))).astype(o_ref.dtype)
                st = st * decay[c * GLA_CHUNK:c * GLA_CHUNK + 1, ksl] + inc[:, c * GLA_DK:(c + 1) * GLA_DK]
            st_ref[h] = st

    pl.when(direction == 0)(lambda: run(False))
    pl.when(direction == 1)(lambda: run(True))

    @pl.when(i == pl.num_programs(2) - 1)
    def _store_state():
        sfin_ref[0, 0] = st_ref[...]


def _gla_scan(q, k, v, g, s0, tb):
    b, s, _ = q.shape
    nb = s // tb

    def blk(d, i):
        return i + d * (nb - 1 - 2 * i)

    st_spec = pl.BlockSpec((1, 1, GLA_HEADS, GLA_DV, GLA_DK), lambda bi, d, i: (bi, d, 0, 0, 0))
    return pl.pallas_call(
        _gla_scan_kernel,
        grid=(b, 2, nb),
        in_specs=[
            pl.BlockSpec((1, tb, GLA_KEY), lambda bi, d, i: (bi, blk(d, i), 0)),
            pl.BlockSpec((1, tb, GLA_KEY), lambda bi, d, i: (bi, blk(d, i), 0)),
            pl.BlockSpec((1, tb, GLA_VAL), lambda bi, d, i: (bi, blk(d, i), 0)),
            pl.BlockSpec((1, tb, GLA_KEY), lambda bi, d, i: (bi, blk(d, i), d)),
            st_spec,
        ],
        out_specs=[pl.BlockSpec((1, 1, tb, GLA_VAL), lambda bi, d, i: (d, bi, blk(d, i), 0)), st_spec],
        out_shape=[jax.ShapeDtypeStruct((2, b, s, GLA_VAL), BF16),
                   jax.ShapeDtypeStruct((b, 2, GLA_HEADS, GLA_DV, GLA_DK), F32)],
        scratch_shapes=[pltpu.VMEM((GLA_HEADS, GLA_DV, GLA_DK), F32)],
        compiler_params=_params("parallel", "parallel", "arbitrary"),
        name="gla_scan",
    )(q, k, v, g, s0)


def _mla_proj_kernel(*refs, rope):
    if rope:
        (x_ref, mod_ref, g_ref, w_ref, qg_ref, wq_ref, kvg_ref, wkn_ref, wvt_ref,
         cos_ref, sa_ref, sb_ref, q_ref, k_ref, vt_ref) = refs
        cos, sa, sb = cos_ref[...], sa_ref[...], sb_ref[...]
    else:
        x_ref, mod_ref, g_ref, w_ref, qg_ref, wq_ref, kvg_ref, wkn_ref, wvt_ref, q_ref, k_ref, vt_ref = refs
    h = _norm_mod(x_ref[0], g_ref[0:1, :], mod_ref[0, 0:1, :], mod_ref[0, 1:2, :]).astype(BF16)
    y = _dot(h, w_ref[...])
    c_q = _rms(y[:, :MLA_Q_LORA], qg_ref[0:1, :]).astype(BF16)
    c_kv = _rms(y[:, MLA_Q_LORA:MLA_Q_LORA + MLA_KV_LORA], kvg_ref[0:1, :]).astype(BF16)
    k_rope = y[:, MLA_Q_LORA + MLA_KV_LORA:]
    if rope:
        k_rope = _rope_chunk(k_rope, cos, sa, sb)
    k_rope = k_rope.astype(BF16)
    for hd in range(MLA_HEADS):
        base = hd * MLA_QK_PAD
        qh = _dot(c_q, wq_ref[:, base: base + MLA_QK_PAD])
        q_rope = qh[:, LANES:]
        if rope:
            q_rope = _rope_chunk(q_rope, cos, sa, sb)
        q_ref[0, :, base: base + LANES] = (qh[:, :LANES] * (MLA_SCALE * LOG2E)).astype(BF16)
        q_ref[0, :, base + LANES: base + MLA_QK_PAD] = (q_rope * (MLA_SCALE * LOG2E)).astype(BF16)
        k_ref[0, :, base + LANES: base + MLA_QK_PAD] = k_rope
    vt_ref[0] = _dot_nt(wvt_ref[...], c_kv).astype(BF16)
    for c in range(MLA_HEADS * MLA_NOPE // MXU_TILE):
        kn = _dot(c_kv, wkn_ref[:, c * MXU_TILE:(c + 1) * MXU_TILE]).astype(BF16)
        for s in range(2):
            hd = 2 * c + s
            k_ref[0, :, hd * MLA_QK_PAD: hd * MLA_QK_PAD + LANES] = kn[:, s * LANES:(s + 1) * LANES]


def _mla_proj(x, mod, mod_row, norm_g, w, qg, wq, kvg, wkn, wvt, tables, tm):
    bx, s, d = x.shape
    rope = tables is not None
    x_spec, mod_spec = _token_specs(tm, d, mod_row)
    consts = [norm_g, w, qg, wq, kvg, wkn, wvt]
    in_specs = [x_spec, mod_spec] + [_const_spec(a.shape) for a in consts]
    args = [x, mod] + consts
    if rope:
        in_specs += _rope_specs(tm)
        args += list(tables)
    qk_w = MLA_HEADS * MLA_QK_PAD
    return pl.pallas_call(
        functools.partial(_mla_proj_kernel, rope=rope),
        grid=(bx, s // tm),
        in_specs=in_specs,
        out_specs=[_out_spec(tm, qk_w), _out_spec(tm, qk_w), _vt_spec(tm, MLA_HEADS * MLA_V)],
        out_shape=[jax.ShapeDtypeStruct((bx, s, qk_w), BF16), jax.ShapeDtypeStruct((bx, s, qk_w), BF16),
                   jax.ShapeDtypeStruct((bx, MLA_HEADS * MLA_V, s), BF16)],
        compiler_params=_params("parallel", "parallel"),
        name="mla_proj",
    )(*args)


def _top2_sum(a, b, c, d):
    hi1, lo1 = jnp.maximum(a, b), jnp.minimum(a, b)
    hi2, lo2 = jnp.maximum(c, d), jnp.minimum(c, d)
    return jnp.maximum(hi1, hi2) + jnp.maximum(jnp.minimum(hi1, hi2), jnp.maximum(lo1, lo2))


def _route_rows(scores, sel):
    eg = EXPERTS_PER_GROUP
    gs = [_top2_sum(*sel[g * eg:(g + 1) * eg]) for g in range(N_GROUPS)]
    best = jnp.maximum(jnp.maximum(gs[0], gs[1]), jnp.maximum(gs[2], gs[3]))
    gidx = jnp.where(gs[0] >= best, 0, jnp.where(gs[1] >= best, 1, jnp.where(gs[2] >= best, 2, 3)))

    def pick(rows, j):
        out = rows[(N_GROUPS - 1) * eg + j]
        for g in range(N_GROUPS - 2, -1, -1):
            out = jnp.where(gidx == g, rows[g * eg + j], out)
        return out

    loc_sel = [pick(sel, j) for j in range(eg)]
    loc_sc = [pick(scores, j) for j in range(eg)]
    weights = []
    for j in range(eg):
        rank = jnp.zeros_like(loc_sel[j])
        for i in range(eg):
            if i == j:
                continue
            beats = (loc_sel[i] >= loc_sel[j]) if i < j else (loc_sel[i] > loc_sel[j])
            rank = rank + jnp.where(beats, 1.0, 0.0)
        weights.append(jnp.where(rank < 1.5, loc_sc[j], 0.0))
    den = weights[0] + weights[1] + weights[2] + weights[3]
    weights = [w / den for w in weights]
    return [jnp.where(gidx == e // eg, weights[e % eg], 0.0) for e in range(N_EXPERTS)], gidx


def _mixer_output(pre, refs):
    if pre == "gla":
        of_ref, ob_ref, og_ref, ng_ref, w_ref = refs
        o = of_ref[0, 0].astype(F32) + ob_ref[0, 0].astype(F32)
        parts = []
        for h in range(GLA_HEADS):
            sl = slice(h * GLA_DV, (h + 1) * GLA_DV)
            parts.append((_rms(o[:, sl], ng_ref[0:1, :]) * _silu(og_ref[0, :, sl].astype(F32))).astype(BF16))
        return _dot(jnp.concatenate(parts, axis=1), w_ref[...])
    o_ref, w_ref = refs
    return _dot(o_ref[0], w_ref[...])


def _moe_kernel(*refs, pre, final_norm):
    n_pre = 5 if pre == "gla" else 2
    (x_ref, mod_ref, g_ref, rw_ref, rb_ref, shu_ref, shd_ref, tri_ref, wup_ref, wdn_ref, fg_ref,
     out_ref, perm_ref, hs_ref, cs_ref, ys_ref) = refs[n_pre:]
    x = x_ref[0] + mod_ref[0, 2:3, :] * _mixer_output(pre, refs[:n_pre])
    tm = x.shape[0]
    n_rows = hs_ref.shape[0]
    h, h_lo = _split_bf16(_norm_mod(x, g_ref[0:1, :], mod_ref[0, 3:4, :], mod_ref[0, 4:5, :]))
    both = _dot_nt(rw_ref[...], h)
    logits = both[:N_EXPERTS] + both[N_EXPERTS:] + _dot_nt(rw_ref[:N_EXPERTS, :], h_lo)
    ys_ref[...] = jnp.zeros(ys_ref.shape, F32)
    a = _dot(h, shu_ref[:, :SHARED_FF])
    u = _dot(h, shu_ref[:, SHARED_FF:])
    out_ref[0] = x + mod_ref[0, 5:6, :] * _dot((_silu(a) * u).astype(BF16), shd_ref[...])
    scores = jax.nn.sigmoid(logits)
    sel = scores + rb_ref[:N_EXPERTS]
    rows, gidx = _route_rows([scores[e:e + 1, :] for e in range(N_EXPERTS)],
                             [sel[e:e + 1, :] for e in range(N_EXPERTS)])
    comb = jnp.concatenate(rows + [jnp.zeros((LANES - N_EXPERTS, tm), F32)], axis=0).T

    member = [jnp.where(gidx == g, 1.0, 0.0) for g in range(N_GROUPS)]
    member_m = jnp.concatenate(member + [jnp.zeros((8 - N_GROUPS, tm), F32)], axis=0).astype(BF16)
    rank = _dot(member_m, tri_ref[...])
    dest = jnp.zeros((1, tm), F32)
    start = jnp.zeros((1, 1), F32)
    seg_start, seg_count = [], []
    for g in range(N_GROUPS):
        count = jnp.sum(member[g], axis=1, keepdims=True)
        dest = dest + member[g] * (start + rank[g:g + 1] - 1.0)
        seg_start.append(start[0, 0].astype(jnp.int32))
        seg_count.append(count[0, 0].astype(jnp.int32))
        start = jnp.floor((start + count + (MOE_ALIGN - 1)) * (1.0 / MOE_ALIGN)) * MOE_ALIGN
    row_id = lax.broadcasted_iota(jnp.int32, (n_rows, tm), 0)
    perm = jnp.where(row_id == dest.astype(jnp.int32), 1.0, 0.0).astype(BF16)
    perm_ref[...] = perm
    c_hi, c_lo = _split_bf16(comb)
    moved = _dot(perm, jnp.concatenate([h, c_hi, c_lo], axis=1))
    d = h.shape[1]
    hs_ref[...] = moved[:, :d].astype(BF16)
    cs_ref[...] = moved[:, d:d + LANES] + moved[:, d + LANES:]

    def run_group(grp, size, n_passes):
        def one_pass(i, carry):
            r0 = pl.multiple_of(seg_start[grp] + i * size, MOE_ALIGN)
            hc = hs_ref[pl.ds(r0, size), :]
            cc = cs_ref[pl.ds(r0, size), :]
            acts = []
            for j in range(EXPERTS_PER_GROUP):
                e = grp * EXPERTS_PER_GROUP + j
                a = _dot(hc, wup_ref[0, e, :, :EXPERT_FF])
                u = _dot(hc, wup_ref[0, e, :, EXPERT_FF:])
                scale = jnp.broadcast_to(cc[:, e:e + 1], a.shape)
                acts.append((_silu(a) * u * scale).astype(BF16))
            w_dn = wdn_ref[0, grp * EXPERTS_PER_GROUP:(grp + 1) * EXPERTS_PER_GROUP]
            y = _dot(jnp.concatenate(acts, axis=1), w_dn.reshape(EXPERTS_PER_GROUP * EXPERT_FF, -1))
            ys_ref[pl.ds(r0, size), :] = ys_ref[pl.ds(r0, size), :] + y
            return carry

        lax.fori_loop(0, n_passes, one_pass, 0)

    for grp in range(N_GROUPS):
        n = seg_count[grp]
        fits = n <= MOE_CHUNK_SMALL
        run_group(grp, MOE_CHUNK_SMALL, jnp.where(fits, jnp.minimum(n, 1), 0))
        run_group(grp, MOE_CHUNK_BIG, jnp.where(fits, 0, (n + (MOE_CHUNK_BIG - 1)) // MOE_CHUNK_BIG))

    out = out_ref[0] + mod_ref[0, 5:6, :] * _dot_tn(perm_ref[...], ys_ref[...].astype(BF16))
    if final_norm:
        out = _rms(out, fg_ref[0:1, :])
    out_ref[0] = out


def _moe(mixer, x, mod, mod_row, norm_g, rw, rb, sh_up, sh_dn, w_up, w_dn, layer, final_g, final_norm, tm):
    bx, s, d = x.shape
    x_spec, mod_spec = _token_specs(tm, d, mod_row)
    pre = mixer[0]
    if pre == "gla":
        o2, og, ng, w_out = mixer[1:]
        o2 = o2.reshape(2, bx, s, GLA_VAL)
        pre_specs = [pl.BlockSpec((1, 1, tm, GLA_VAL), lambda b, t: (0, b, t, 0)),
                     pl.BlockSpec((1, 1, tm, GLA_VAL), lambda b, t: (1, b, t, 0)),
                     _out_spec(tm, GLA_VAL), _const_spec(ng.shape, True), _const_spec(w_out.shape, True)]
        pre_args = [o2, o2, og.reshape(bx, s, GLA_VAL), ng, w_out]
    else:
        o, w_out = mixer[1:]
        o = o.reshape(bx, s, -1)
        pre_specs = [_out_spec(tm, o.shape[2]), _const_spec(w_out.shape, True)]
        pre_args = [o, w_out]
    tri = (jnp.arange(tm)[:, None] <= jnp.arange(tm)[None, :]).astype(BF16)
    consts = [norm_g, rw, rb, sh_up, sh_dn, tri]
    n_rows = -(-(tm + (N_GROUPS - 1) * (MOE_ALIGN - 1) + MOE_CHUNK_BIG) // MXU_TILE) * MXU_TILE
    resident = pl.Buffered(1)
    in_specs = (pre_specs + [x_spec, mod_spec] + [_const_spec(a.shape, True) for a in consts] + [
        pl.BlockSpec((1,) + w_up.shape[1:], lambda b, t: (layer, 0, 0, 0), pipeline_mode=resident),
        pl.BlockSpec((1,) + w_dn.shape[1:], lambda b, t: (layer, 0, 0, 0), pipeline_mode=resident),
        _const_spec(final_g.shape, True)])
    return pl.pallas_call(
        functools.partial(_moe_kernel, pre=pre, final_norm=final_norm),
        grid=(bx, s // tm),
        in_specs=in_specs,
        out_specs=x_spec,
        out_shape=jax.ShapeDtypeStruct(x.shape, F32),
        scratch_shapes=[pltpu.VMEM((n_rows, tm), BF16), pltpu.VMEM((n_rows, d), BF16),
                        pltpu.VMEM((n_rows, LANES), F32), pltpu.VMEM((n_rows, d), F32)],
        compiler_params=_params("parallel", "parallel"),
        name="moe",
    )(*pre_args, x, mod, *consts, w_up, w_dn, final_g)


def _diff_weights(w_in):
    d = w_in.shape[0]

    def regroup(w):
        return w.reshape(d, 2, DIFF_HEADS, DIFF_HEAD_DIM).transpose(0, 2, 1, 3).reshape(d, DIFF_QK)

    w_qk = jnp.concatenate([regroup(w_in[:, :DIFF_QK]), regroup(w_in[:, DIFF_QK:2 * DIFF_QK])], axis=1)
    return w_qk.astype(BF16), w_in[:, 2 * DIFF_QK:].T.astype(BF16)


def _gla_weights(w_in, gate_w, gate_b):
    d = w_in.shape[0]
    main = 2 * GLA_KEY + 2 * GLA_VAL
    w_r = jnp.zeros((d, LANES), F32).at[:, :2 * GLA_GATE_RANK].set(w_in[:, main:]).astype(BF16)
    w_g = jnp.zeros((LANES, 2 * GLA_KEY), F32)
    w_g = w_g.at[:GLA_GATE_RANK, :GLA_KEY].set(gate_w[0])
    w_g = w_g.at[GLA_GATE_RANK:2 * GLA_GATE_RANK, GLA_KEY:].set(gate_w[1]).astype(BF16)
    return w_in[:, :main].astype(BF16), w_r, w_g, _rows(gate_b)


def _mla_weights(w_in, w_uq, w_ukv):
    d = w_in.shape[0]
    w = jnp.zeros((d, MLA_Q_LORA + MLA_KV_LORA + LANES), F32).at[:, :w_in.shape[1]].set(w_in).astype(BF16)
    wq = w_uq.reshape(MLA_Q_LORA, MLA_HEADS, MLA_NOPE + MLA_ROPE)
    wq = jnp.pad(wq, ((0, 0), (0, 0), (0, MLA_QK_PAD - MLA_NOPE - MLA_ROPE)))
    wq = wq.reshape(MLA_Q_LORA, MLA_HEADS * MLA_QK_PAD).astype(BF16)
    wkv = w_ukv.reshape(MLA_KV_LORA, MLA_HEADS, MLA_NOPE + MLA_V)
    wkn = wkv[:, :, :MLA_NOPE].reshape(MLA_KV_LORA, MLA_HEADS * MLA_NOPE).astype(BF16)
    wvt = wkv[:, :, MLA_NOPE:].reshape(MLA_KV_LORA, MLA_HEADS * MLA_V).T.astype(BF16)
    return w, wq, wkn, wvt


def _ctx_views(qkv, b, cl):
    q, k, vt = qkv
    vt = vt.reshape(vt.shape[1], b, cl).transpose(1, 0, 2)
    return q.reshape(b, cl, -1), k.reshape(b, cl, -1), vt


def kernel(x, c, ctx, c_ctx, ada_w, ada_b, norm_g, router_w, router_b, moe_w_up, moe_w_down, shared_w_up,
           shared_w_down, diff_w_in, diff_lam, diff_norm_g, diff_w_out, gla_w_in, gla_gate_w, gla_gate_b,
           gla_norm_g, gla_w_out, mla_w_in, mla_q_norm_g, mla_w_uq, mla_kv_norm_g, mla_w_ukv, mla_w_out,
           final_norm_g):
    b, s, d = x.shape
    cl = ctx.shape[1]
    assert b + 1 <= COND_ROWS and d == D_MODEL
    depth = ada_w.shape[0]
    ctx_row = b

    cond = jnp.zeros((COND_ROWS, d), F32).at[:b].set(c).at[ctx_row].set(c_ctx)
    mods = _ada_params(cond, ada_w, ada_b).reshape(depth, COND_ROWS, N_MOD, d)

    tables = _rope_tables(s)
    rw = jnp.concatenate(_split_bf16(router_w.T), axis=0)
    rb = jnp.pad(router_b.reshape(N_EXPERTS, 1), ((0, PAD_ROWS - N_EXPERTS), (0, 0)))
    w_up_all = moe_w_up.astype(BF16)
    w_dn_all = moe_w_down.astype(BF16)
    final_g = _rows(final_norm_g)

    tm_lat = TOKEN_TILE
    n_ctx = b * cl
    tm_ctx = TOKEN_TILE if n_ctx % TOKEN_TILE == 0 else cl
    assert s % FLASH_TK == 0 and s % TOKEN_TILE == 0 and s % GLA_BLOCK == 0 and cl % GLA_CHUNK == 0
    x_lat = x
    x_ctx = ctx.reshape(1, n_ctx, d)

    for i in range(depth):
        need_ctx = i < depth - 1
        mod = mods[i]
        g1 = _rows(norm_g[i, 0])
        g2 = _rows(norm_g[i, 1])
        kind, j = i % N_MIXERS, i // N_MIXERS
        if kind == 0:
            lam_init = 0.8 - 0.6 * math.exp(-0.3 * i)
            w, wvt = _diff_weights(diff_w_in[j])
            w_out = diff_w_out[j].astype(BF16)
            lam_p = jnp.pad(diff_lam[j], ((0, PAD_ROWS - 4), (0, LANES - DIFF_HEAD_DIM)))
            extras = (lam_p, diff_norm_g[j].reshape(2 * DIFF_HEAD_DIM, 1))
            q_l, k_l, v_l = _diff_proj(x_lat, mod, None, g1, w, wvt, tables, tm_lat)
            q_c, k_c, v_c = _ctx_views(_diff_proj(x_ctx, mod, ctx_row, g1, w, wvt, None, tm_ctx), b, cl)
            o_lat = _flash(q_l, k_l, v_l, (k_c, v_c), extras, heads=DIFF_HEADS, dq=2 * DIFF_HEAD_DIM,
                           tq=FLASH_TQ, tk=FLASH_TK, lam_init=lam_init, hb=DIFF_HEADS)
            mix_lat = ("plain", o_lat, w_out)
            if need_ctx:
                o_ctx = _flash(q_c, k_c, v_c, None, extras, heads=DIFF_HEADS, dq=2 * DIFF_HEAD_DIM,
                               tq=cl, tk=cl, lam_init=lam_init, hb=DIFF_HEADS)
                mix_ctx = ("plain", o_ctx, w_out)
        elif kind == 1:
            w, w_r, w_g, b_g = _gla_weights(gla_w_in[j], gla_gate_w[j], gla_gate_b[j])
            w_out = gla_w_out[j].astype(BF16)
            ng = _rows(gla_norm_g[j])
            q_l, k_l, v_l, og_l, gt_l = _gla_proj(x_lat, mod, None, g1, w, w_r, w_g, b_g, tm_lat)
            q_c, k_c, v_c, og_c, gt_c = (t.reshape(b, cl, -1)
                                         for t in _gla_proj(x_ctx, mod, ctx_row, g1, w, w_r, w_g, b_g, tm_ctx))
            s0 = jnp.zeros((b, 2, GLA_HEADS, GLA_DV, GLA_DK), F32)
            o_c, s_c = _gla_scan(q_c, k_c, v_c, gt_c, s0, cl)
            o_l, _ = _gla_scan(q_l, k_l, v_l, gt_l, s_c, GLA_BLOCK)
            mix_lat = ("gla", o_l, og_l, ng, w_out)
            if need_ctx:
                mix_ctx = ("gla", o_c, og_c, ng, w_out)
        else:
            w, wq, wkn, wvt = _mla_weights(mla_w_in[j], mla_w_uq[j], mla_w_ukv[j])
            w_out = mla_w_out[j].astype(BF16)
            qg = _rows(mla_q_norm_g[j])
            kvg = _rows(mla_kv_norm_g[j])
            q_l, k_l, v_l = _mla_proj(x_lat, mod, None, g1, w, qg, wq, kvg, wkn, wvt, tables, tm_lat)
            q_c, k_c, v_c = _ctx_views(_mla_proj(x_ctx, mod, ctx_row, g1, w, qg, wq, kvg, wkn, wvt, None, tm_ctx),
                                       b, cl)
            o_lat = _flash(q_l, k_l, v_l, (k_c, v_c), None, heads=MLA_HEADS, dq=MLA_QK_PAD, tq=FLASH_TQ, tk=FLASH_TK,
                           hb=MLA_HEADS)
            mix_lat = ("plain", o_lat, w_out)
            if need_ctx:
                o_ctx = _flash(q_c, k_c, v_c, None, None, heads=MLA_HEADS, dq=MLA_QK_PAD, tq=cl, tk=cl,
                               hb=MLA_HEADS)
                mix_ctx = ("plain", o_ctx, w_out)

        sh_up = shared_w_up[i].astype(BF16)
        sh_dn = shared_w_down[i].astype(BF16)
        last = i == depth - 1
        x_lat = _moe(mix_lat, x_lat, mod, None, g2, rw, rb, sh_up, sh_dn, w_up_all, w_dn_all, i, final_g, last,
                     tm_lat)
        if need_ctx:
            x_ctx = _moe(mix_ctx, x_ctx, mod, ctx_row, g2, rw, rb, sh_up, sh_dn, w_up_all, w_dn_all, i, final_g,
                         False, tm_ctx)
    return x_lat
```
